```python
import jax, jax.numpy as jnp
from jax import lax
import numpy as np

D_MODEL = 1024
BATCH = 2
SEQ = 8192
DEPTH = 1

CHUNK = 64

D_MIX = D_MODEL
D_POOL = D_MIX // 2
D_RWKV = D_MIX - D_POOL
POOL_WINDOWS = (2, 4, 8, 16)
N_POOL_GROUPS = len(POOL_WINDOWS)
POOL_GROUP = D_POOL // N_POOL_GROUPS
HEAD_SIZE = 64
N_RWKV_HEADS = D_RWKV // HEAD_SIZE
DECAY_LORA = 64
ICLR_LORA = 64
RWKV_SPLITS = (D_RWKV, DECAY_LORA, D_RWKV, D_RWKV, ICLR_LORA, D_RWKV)
D_RWKV_SEG = sum(RWKV_SPLITS)
N_IN = 2 * D_POOL + D_RWKV_SEG
NORM_EPS = 1e-6
GN_EPS = 64e-5
L2_EPS = 1e-12

kernel_name = "hybrid_pool_rwkv7_adaln_block"


def _rmsnorm(x, g):
    xf = x.astype(jnp.float32)
    y = xf * lax.rsqrt(jnp.mean(xf * xf, axis=-1, keepdims=True) + NORM_EPS)
    return (y * g.astype(jnp.float32)).astype(x.dtype)


def _multiscale_pool_diff(u):
    T = u.shape[1]
    cs = jnp.cumsum(u, axis=1)
    pos = jnp.arange(1, T + 1)
    outs = []
    for g, win in enumerate(POOL_WINDOWS):
        sl = slice(g * POOL_GROUP, (g + 1) * POOL_GROUP)
        csg = cs[..., sl]
        lagged = jnp.pad(csg, ((0, 0), (win, 0), (0, 0)))[:, :T]
        count = jnp.minimum(pos, win).astype(u.dtype)
        mean = (csg - lagged) / count[None, :, None]
        outs.append(mean - u[..., sl])
    return jnp.stack(outs, axis=2)


def _rwkv7_scan(r, decay, k, v, kk, a):
    B, T, H, N = r.shape
    xs = tuple(jnp.moveaxis(t, 1, 0) for t in (r, decay, k, v, -kk, kk * a))

    def step(S, inp):
        r_t, w_t, k_t, v_t, am_t, b_t = inp
        sa = jnp.einsum('bhij,bhj->bhi', S, am_t)
        S = (S * w_t[:, :, None, :]
             + sa[..., None] * b_t[:, :, None, :]
             + v_t[..., None] * k_t[:, :, None, :])
        y = jnp.einsum('bhij,bhj->bhi', S, r_t)
        return S, y

    S0 = jnp.zeros((B, H, N, N), jnp.float32)
    _, ys = lax.scan(step, S0, xs)
    return jnp.moveaxis(ys, 0, 1)


def setup_inputs(seed: int = 0) -> dict:
    key = jax.random.key(seed)
    ks = jax.random.split(key, 24)
    f32 = jnp.float32
    L = DEPTH
    nrm = lambda k, s: jax.random.normal(k, s, f32)
    return {
        "x": nrm(ks[0], (BATCH, SEQ, D_MODEL)),
        "c": nrm(ks[1], (BATCH, D_MODEL)),
        "w_ada": nrm(ks[2], (L, D_MODEL, 3 * D_MODEL)) * (0.5 * D_MODEL ** -0.5),
        "b_ada": nrm(ks[3], (L, 3 * D_MODEL)) * 0.01,
        "norm_g": 1.0 + 0.01 * nrm(ks[4], (L, D_MODEL)),
        "w_in": nrm(ks[5], (L, D_MODEL, N_IN)) * D_MODEL ** -0.5,
        "pool_w": nrm(ks[6], (L, N_POOL_GROUPS, POOL_GROUP, POOL_GROUP)) * POOL_GROUP ** -0.5,
        "pool_scale": 1.0 + 0.02 * nrm(ks[7], (L, D_POOL)),
        "mu_shift": jax.random.uniform(ks[8], (L, D_RWKV_SEG), f32),
        "w0": jax.random.uniform(ks[9], (L, D_RWKV), f32, minval=-6.0, maxval=0.0),
        "w_up": nrm(ks[10], (L, DECAY_LORA, D_RWKV)) * 0.1,
        "a0": nrm(ks[11], (L, D_RWKV)) * 0.1,
        "a_up": nrm(ks[12], (L, ICLR_LORA, D_RWKV)) * (0.5 * ICLR_LORA ** -0.5),
        "k_k": 0.85 + 0.02 * nrm(ks[13], (L, D_RWKV)),
        "k_a": 1.0 + 0.02 * nrm(ks[14], (L, D_RWKV)),
        "r_k": nrm(ks[15], (L, D_RWKV)) * 0.1,
        "ln_w": 1.0 + 0.02 * nrm(ks[16], (L, D_RWKV)),
        "ln_b": nrm(ks[17], (L, D_RWKV)) * 0.01,
        "w_out": nrm(ks[18], (L, D_MIX, D_MODEL)) * D_MIX ** -0.5,
        "final_g": 1.0 + 0.01 * nrm(ks[19], (D_MODEL,)),
    }


def reference(x, c, w_ada, b_ada, norm_g, w_in, pool_w, pool_scale, mu_shift, w0,
              w_up, a0, a_up, k_k, k_a, r_k, ln_w, ln_b, w_out, final_g):
    B, T, _ = x.shape
    H, N = N_RWKV_HEADS, HEAD_SIZE
    f32 = jnp.float32
    split_idx = list(np.cumsum(RWKV_SPLITS)[:-1])
    for l in range(DEPTH):
        mod = c @ w_ada[l] + b_ada[l]
        shift, scale, gate = jnp.split(mod, 3, axis=-1)
        h = _rmsnorm(x, norm_g[l]) * (1.0 + scale[:, None, :]) + shift[:, None, :]

        p = h @ w_in[l]
        pool_u = p[..., :D_POOL]
        pool_z = p[..., D_POOL:2 * D_POOL]
        seg = p[..., 2 * D_POOL:]

        diff = _multiscale_pool_diff(pool_u.astype(f32))
        y_pool = jnp.einsum('btgc,gcd->btgd', diff, pool_w[l].astype(f32))
        y_pool = y_pool.reshape(B, T, D_POOL) * pool_scale[l].astype(f32)
        y_pool = y_pool * jax.nn.silu(pool_z.astype(f32))

        seg = seg.astype(f32)
        prev = jnp.pad(seg, ((0, 0), (1, 0), (0, 0)))[:, :T]
        seg = seg + (prev - seg) * mu_shift[l].astype(f32)
        r, w_lo, k, v, a_lo, z = jnp.split(seg, split_idx, axis=-1)
        w_raw = w0[l] + jnp.tanh(w_lo) @ w_up[l].astype(f32)
        w_raw = -jax.nn.softplus(-w_raw) - 0.5
        decay = jnp.exp(-jnp.exp(w_raw))
        a = jax.nn.sigmoid(a0[l] + a_lo @ a_up[l].astype(f32))
        kk = (k * k_k[l]).reshape(B, T, H, N)
        kk = kk / jnp.maximum(jnp.linalg.norm(kk, axis=-1, keepdims=True), L2_EPS)
        k = k * (1.0 + (a - 1.0) * k_a[l])
        rh, kh, vh = (t.reshape(B, T, H, N) for t in (r, k, v))
        yh = _rwkv7_scan(rh, decay.reshape(B, T, H, N), kh, vh, kk,
                         a.reshape(B, T, H, N))
        mu = jnp.mean(yh, axis=-1, keepdims=True)
        var = jnp.mean(jnp.square(yh - mu), axis=-1, keepdims=True)
        yh = (yh - mu) * lax.rsqrt(var + GN_EPS)
        yh = yh * ln_w[l].reshape(H, N) + ln_b[l].reshape(H, N)
        bonus = jnp.sum(rh * kh * r_k[l].reshape(H, N), axis=-1, keepdims=True) * vh
        y_rwkv = (yh + bonus).reshape(B, T, D_RWKV) * jax.nn.silu(z)

        mix = jnp.concatenate([y_pool, y_rwkv], axis=-1).astype(x.dtype)
        out = mix @ w_out[l]
        x = x + gate[:, None, :] * out
    return _rmsnorm(x, final_g)
```

```python
import functools

import numpy as np
import jax
import jax.numpy as jnp
from jax import lax
from jax.experimental import pallas as pl
from jax.experimental.pallas import tpu as pltpu

F32 = jnp.float32
BF16 = jnp.bfloat16

D_MODEL = 1024
D_POOL = 512
D_RWKV = 512
HEAD = 64
LORA = 64
POOL_WINDOWS = (2, 4, 8, 16)
POOL_GROUP = 128
NORM_EPS = 1e-6
GN_EPS = 64e-5
L2_EPS = 1e-12

COL_U = 0
COL_PZ = 512
COL_R = 1024
COL_K = 1536
COL_V = 2048
COL_Z = 2560
COL_LO = 3072
N_IN = 3200
SEG0 = 1024

CHUNK = 64
GROUP = 256
HEADS_PER_GROUP = GROUP // HEAD
N_GROUPS = D_RWKV // GROUP
TB = 256
PAD = 24

ROW_W0, ROW_A0, ROW_KK, ROW_KA, ROW_RK, ROW_LNW, ROW_LNB, ROW_PSCALE = range(8)

VMEM_LIMIT_BYTES = 56 * 1024 * 1024


def _split_hi_lo(x):
    hi = x.astype(BF16)
    lo = (x - hi.astype(F32)).astype(BF16)
    return hi, lo


def _dot(a, b):
    return jnp.dot(a, b, preferred_element_type=F32)


def _dot_hilo(x, w_bf16):
    hi, lo = _split_hi_lo(x)
    return _dot(hi, w_bf16) + _dot(lo, w_bf16)


def _sigmoid(x):
    return 1.0 / (1.0 + jnp.exp(-x))


def _ada_kernel(c_ref, w_ref, b_ref, o_ref):
    c_hi, c_lo = _split_hi_lo(c_ref[...])
    w_hi, w_lo = _split_hi_lo(w_ref[...])
    acc = _dot(c_hi, w_hi) + _dot(c_lo, w_hi) + _dot(c_hi, w_lo)
    o_ref[...] = acc + b_ref[...]


def _ada_mod(c_pad, w_ada, b_ada):
    n = w_ada.shape[1]
    bn = 768
    return pl.pallas_call(
        _ada_kernel,
        grid=(n // bn,),
        in_specs=[
            pl.BlockSpec((c_pad.shape[0], D_MODEL), lambda j: (0, 0)),
            pl.BlockSpec((D_MODEL, bn), lambda j: (0, j)),
            pl.BlockSpec((1, bn), lambda j: (0, j)),
        ],
        out_specs=pl.BlockSpec((c_pad.shape[0], bn), lambda j: (0, j)),
        out_shape=jax.ShapeDtypeStruct((c_pad.shape[0], n), F32),
        name="adaln_mod",
    )(c_pad, w_ada, b_ada)


def _block_diag(x, bdmask):
    xb = x.astype(BF16)
    return jnp.concatenate([xb] * HEADS_PER_GROUP, axis=0) * bdmask


def _diag_blocks(x, head_masks):
    out = jnp.where(head_masks[0], x[0:HEAD, :], 0.0)
    for h in range(1, HEADS_PER_GROUP):
        out = out + jnp.where(head_masks[h], x[h * HEAD:(h + 1) * HEAD, :], 0.0)
    return out


def _fused_kernel(x_ref, mod_ref, ng_ref, win_ref, mu_ref, lhi_ref, llo_ref, vec_ref,
                  pw_ref, wout_ref, fg_ref, bdmask_ref, trimask_ref, eye_ref, tmask_ref, cmat_ref,
                  segones_ref, o_ref,
                  p_ref, sa_ref, sb_ref, at_ref, rt_ref, bt_ref, kt_ref, bdc_ref, kdc_ref,
                  v_ref, cl_ref, y_ref, bon_ref, mix_ref, h_ref):
    t = pl.program_id(1)

    @pl.when(t == 0)
    def _():
        p_ref[0:PAD, :] = jnp.zeros((PAD, N_IN), F32)
        sa_ref[0:8, :] = jnp.zeros((8, D_POOL), F32)
        sb_ref[0:8, :] = jnp.zeros((8, D_POOL), F32)
        h_ref[...] = jnp.zeros(h_ref.shape, F32)

    shift = mod_ref[0:1, :]
    scale = mod_ref[1:2, :]
    gate = mod_ref[2:3, :]

    xb = x_ref[...]
    ms = jnp.mean(xb * xb, axis=-1, keepdims=True)
    hmod = (xb * lax.rsqrt(ms + NORM_EPS) * ng_ref[...]) * (1.0 + scale) + shift
    p_ref[PAD:PAD + TB, :] = _dot(hmod.astype(BF16), win_ref[...])

    def vec(row):
        return vec_ref[row:row + 1, :]

    n_ext = TB + PAD
    sa_ref[8:n_ext, :] = p_ref[8:n_ext, 0:512] + p_ref[7:n_ext - 1, 0:512]
    sb_ref[8:n_ext, 128:512] = sa_ref[8:n_ext, 128:512] + sa_ref[6:n_ext - 2, 128:512]
    sa_ref[8:n_ext, 256:512] = sb_ref[8:n_ext, 256:512] + sb_ref[4:n_ext - 4, 256:512]
    sb_ref[8:n_ext, 384:512] = sa_ref[8:n_ext, 384:512] + sa_ref[0:n_ext - 8, 384:512]
    pos = t * TB + lax.broadcasted_iota(jnp.int32, (TB, 1), 0) + 1
    wsum_refs = (sa_ref, sb_ref, sa_ref, sb_ref)
    diffs = []
    for g, win in enumerate(POOL_WINDOWS):
        lanes = slice(g * POOL_GROUP, (g + 1) * POOL_GROUP)
        cnt = jnp.minimum(pos, win).astype(F32)
        mean = wsum_refs[g][PAD:PAD + TB, lanes] / cnt
        diffs.append(mean - p_ref[PAD:PAD + TB, lanes])
    pz = p_ref[PAD:PAD + TB, COL_PZ:COL_PZ + D_POOL]
    for half in range(2):
        d2 = jnp.concatenate(diffs[2 * half:2 * half + 2], axis=1).astype(BF16)
        yp = _dot(d2, pw_ref[half])
        lanes = slice(256 * half, 256 * half + 256)
        zz = pz[:, lanes]
        yp = yp * vec_ref[ROW_PSCALE:ROW_PSCALE + 1, lanes] * (zz * _sigmoid(zz))
        mix_ref[:, lanes] = yp.astype(BF16)

    def lerp(col, width):
        cur = p_ref[PAD:PAD + TB, col:col + width]
        prev = p_ref[PAD - 1:PAD + TB - 1, col:col + width]
        return cur + (prev - cur) * mu_ref[0:1, col - SEG0:col - SEG0 + width]

    lo = lerp(COL_LO, 2 * LORA)
    lane = lax.broadcasted_iota(jnp.int32, lo.shape, 1)
    lo = jnp.where(lane < LORA, jnp.tanh(lo), lo)
    lo_hi, lo_lo = _split_hi_lo(lo)
    lin = _dot(lo_hi, lhi_ref[...]) + _dot(lo_lo, lhi_ref[...]) + _dot(lo_hi, llo_ref[...])
    logw = -float(np.exp(-0.5)) * _sigmoid(vec(ROW_W0) + lin[:, 0:D_RWKV])
    a = _sigmoid(vec(ROW_A0) + lin[:, D_RWKV:2 * D_RWKV])

    def segsum(xv):
        parts = []
        for half in range(N_GROUPS):
            parts.append(_dot_hilo(xv[:, GROUP * half:GROUP * (half + 1)], segones_ref[...]))
        return jnp.concatenate(parts, axis=1)

    lw_hi, lw_lo = _split_hi_lo(logw)
    cc = _dot(cmat_ref[...], lw_hi) + _dot(cmat_ref[...], lw_lo)
    cum = cc[0:TB, :]
    ctot = cc[TB:2 * TB, :]
    cl_ref[...] = ctot

    k = lerp(COL_K, D_RWKV)
    kkr = k * vec(ROW_KK)
    ssq = segsum(kkr * kkr)
    kk = kkr / jnp.maximum(jnp.sqrt(ssq), L2_EPS)
    k2 = k * (1.0 + (a - 1.0) * vec(ROW_KA))
    bv = kk * a
    p_inv = jnp.exp(-cum)
    p_end = jnp.exp(ctot - cum)
    bt_ref[...] = bv * p_inv
    kt_ref[...] = k2 * p_inv
    bdc_ref[...] = bv * p_end
    kdc_ref[...] = k2 * p_end
    at_ref[...] = -kk * jnp.exp(cum - logw)
    r = lerp(COL_R, D_RWKV)
    rt_ref[...] = r * jnp.exp(cum)
    v = lerp(COL_V, D_RWKV)
    v_ref[...] = v
    bon_ref[...] = segsum(r * k2 * vec(ROW_RK)) * v

    bdmask = bdmask_ref[...]
    lane_blk = lax.shift_right_logical(lax.broadcasted_iota(jnp.int32, (HEAD, GROUP), 1), 6)
    head_masks = [lane_blk == h for h in range(HEADS_PER_GROUP)]
    eye = eye_ref[...]

    def chunk_body(c, hs):
        r0 = pl.multiple_of(c * CHUNK, CHUNK)
        rows = pl.ds(r0, CHUNK)
        new_hs = []
        for g in range(N_GROUPS):
            lanes = slice(GROUP * g, GROUP * (g + 1))
            at = at_ref[rows, lanes]
            rt = rt_ref[rows, lanes]
            bt = bt_ref[rows, lanes]
            kt = kt_ref[rows, lanes]
            vv = v_ref[rows, lanes]
            vbd = _block_diag(vv, bdmask)
            ar = jnp.concatenate([at, rt], axis=0).astype(BF16)
            rhs1 = jnp.concatenate([_block_diag(bt, bdmask), _block_diag(kt, bdmask)], axis=0)
            s = lax.dot_general(ar, rhs1, (((1,), (1,)), ((), ())), preferred_element_type=F32)
            s = s * trimask_ref[...]
            lab = s[0:CHUNK, 0:GROUP]
            lak = s[0:CHUNK, GROUP:2 * GROUP]
            mrb = s[CHUNK:2 * CHUNK, 0:GROUP]
            mrk = s[CHUNK:2 * CHUNK, GROUP:2 * GROUP]
            ldiag = lab * tmask_ref[0]
            tinv = eye + ldiag
            lpow = _dot(ldiag.astype(BF16), _block_diag(ldiag, bdmask))
            for _ in range(2):
                res = _dot(jnp.concatenate([lpow, tinv], axis=0).astype(BF16), _block_diag(lpow, bdmask))
                lpow = res[0:CHUNK, :]
                tinv = tinv + res[CHUNK:2 * CHUNK, :]
            tinv = tinv + _dot(tinv.astype(BF16), _block_diag(lpow, bdmask))
            for level in (1, 2):
                xm = _dot((lab * tmask_ref[level]).astype(BF16), _block_diag(tinv, bdmask))
                tinv = tinv + _dot(tinv.astype(BF16), _block_diag(xm, bdmask))
            lakv = _dot(lak.astype(BF16), vbd)
            wu = _dot(tinv.astype(BF16),
                      jnp.concatenate([_block_diag(at, bdmask), _block_diag(lakv, bdmask)], axis=1))
            w = wu[:, 0:GROUP]
            u0 = wu[:, GROUP:2 * GROUP]
            qy = _dot(mrb.astype(BF16),
                      jnp.concatenate([_block_diag(w, bdmask), _block_diag(u0, bdmask)], axis=1))
            q = rt + qy[:, 0:GROUP]
            y0 = qy[:, GROUP:2 * GROUP] + _dot(mrk.astype(BF16), vbd)
            bkd_t = jnp.concatenate([bdc_ref[rows, lanes], kdc_ref[rows, lanes]], axis=0).T
            rhs5 = jnp.concatenate(
                [jnp.concatenate([w, u0], axis=1),
                 jnp.concatenate([jnp.zeros_like(vv), vv], axis=1)], axis=0).astype(BF16)
            x1 = _dot(bkd_t.astype(BF16), rhs5)
            p_c = jnp.exp(cl_ref[pl.ds(r0, 1), lanes])
            a_mat = eye * p_c + _diag_blocks(x1[:, 0:GROUP], head_masks)
            g_mat = _diag_blocks(x1[:, GROUP:2 * GROUP], head_masks)
            res = _dot(jnp.concatenate([a_mat, q], axis=0).astype(BF16), _block_diag(hs[g], bdmask))
            new_hs.append(res[0:CHUNK, :] + g_mat)
            y_ref[rows, lanes] = res[CHUNK:2 * CHUNK, :] + y0
        return tuple(new_hs)

    hs = lax.fori_loop(0, TB // CHUNK, chunk_body, tuple(h_ref[g] for g in range(N_GROUPS)))
    for g in range(N_GROUPS):
        h_ref[g] = hs[g]

    yh = y_ref[...]
    mu = segsum(yh) * (1.0 / HEAD)
    dlt = yh - mu
    var = segsum(dlt * dlt) * (1.0 / HEAD)
    yn = dlt * lax.rsqrt(var + GN_EPS) * vec(ROW_LNW) + vec(ROW_LNB)
    z = lerp(COL_Z, D_RWKV)
    y_rwkv = (yn + bon_ref[...]) * (z * _sigmoid(z))
    mix_ref[:, D_POOL:D_POOL + D_RWKV] = y_rwkv.astype(BF16)

    out = _dot(mix_ref[...], wout_ref[...])
    xo = x_ref[...] + gate * out
    ms2 = jnp.mean(xo * xo, axis=-1, keepdims=True)
    o_ref[...] = xo * lax.rsqrt(ms2 + NORM_EPS) * fg_ref[...]

    p_ref[8:PAD, :] = p_ref[TB + 8:TB + PAD, :]


def _constants():
    hb = np.arange(GROUP) // HEAD
    bdmask = (hb[:, None] == hb[None, :]).astype(np.float32)
    ti = np.arange(CHUNK)[:, None]
    si = (np.arange(2 * GROUP) % HEAD)[None, :]
    trimask = np.concatenate([(ti > si), (ti >= si)], axis=0).astype(np.float32)
    sg = (np.arange(GROUP) % HEAD)[None, :]
    eye = (ti == sg).astype(np.float32)
    m16 = (ti // 16) == (sg // 16)
    m32 = (ti // 32) == (sg // 32)
    tmask = np.stack([m16, m32 & ~m16, ~m32], axis=0).astype(np.float32)
    rr = np.arange(TB)
    same = (rr[:, None] // CHUNK) == (rr[None, :] // CHUNK)
    tril = same & (rr[None, :] <= rr[:, None])
    cmat = np.concatenate([tril, same], axis=0).astype(np.float32)
    return bdmask, trimask, eye, tmask, cmat


def kernel(x, c, w_ada, b_ada, norm_g, w_in, pool_w, pool_scale, mu_shift, w0, w_up, a0, a_up,
           k_k, k_a, r_k, ln_w, ln_b, w_out, final_g):
    B, T, _ = x.shape
    assert w_ada.shape[0] == 1 and T % TB == 0
    l = 0

    c_pad = jnp.zeros((8, D_MODEL), F32).at[0:B].set(c)
    mod = _ada_mod(c_pad, w_ada[l], b_ada[l][None, :])[0:B].reshape(B, 3, D_MODEL)

    s0 = 2 * D_POOL
    spans = [(0, s0), (s0, s0 + 512), (s0 + 576, s0 + 1088), (s0 + 1088, s0 + 1600),
             (s0 + 1664, s0 + 2176), (s0 + 512, s0 + 576), (s0 + 1600, s0 + 1664)]
    w_in_p = jnp.concatenate([w_in[l][:, a:b] for a, b in spans], axis=1).astype(BF16)
    mu_p = jnp.concatenate([mu_shift[l][a - s0:b - s0] for a, b in spans[1:]], axis=0)[None, :]
    lora = jnp.zeros((2 * LORA, 2 * D_RWKV), F32)
    lora = lora.at[0:LORA, 0:D_RWKV].set(w_up[l]).at[LORA:, D_RWKV:].set(a_up[l])
    lora_hi, lora_lo = _split_hi_lo(lora)
    vec = jnp.stack([w0[l], a0[l], k_k[l], k_a[l], r_k[l], ln_w[l], ln_b[l], pool_scale[l]], axis=0)
    pw = jnp.zeros((2, 256, 256), F32)
    for g in range(4):
        o = (g % 2) * POOL_GROUP
        pw = pw.at[g // 2, o:o + POOL_GROUP, o:o + POOL_GROUP].set(pool_w[l, g])
    pw = pw.astype(BF16)
    w_out_b = w_out[l].astype(BF16)

    bdmask, trimask, eye, tmask, cmat = _constants()
    bdmask_b = jnp.asarray(bdmask, BF16)
    segones = bdmask_b
    trimask = jnp.asarray(trimask)
    eye = jnp.asarray(eye)
    tmask = jnp.asarray(tmask)
    cmat = jnp.asarray(cmat, BF16)

    def full(a):
        nd = a.ndim
        return pl.BlockSpec(a.shape, lambda b, t, _nd=nd: (0,) * _nd)

    consts = (norm_g[l][None, :], w_in_p, mu_p, lora_hi, lora_lo, vec, pw, w_out_b,
              final_g[None, :], bdmask_b, trimask, eye, tmask, cmat, segones)
    in_specs = [
        pl.BlockSpec((None, TB, D_MODEL), lambda b, t: (b, t, 0)),
        pl.BlockSpec((None, 3, D_MODEL), lambda b, t: (b, 0, 0)),
    ] + [full(a) for a in consts]

    blk = lambda: pltpu.VMEM((TB, D_RWKV), F32)
    scratch = [
        pltpu.VMEM((TB + PAD, N_IN), F32),
        pltpu.VMEM((TB + PAD, D_POOL), F32),
        pltpu.VMEM((TB + PAD, D_POOL), F32),
        blk(), blk(), blk(), blk(), blk(), blk(),
        blk(), blk(), blk(), blk(),
        pltpu.VMEM((TB, D_MODEL), BF16),
        pltpu.VMEM((N_GROUPS, HEAD, GROUP), F32),
    ]
    return pl.pallas_call(
        _fused_kernel,
        grid=(B, T // TB),
        in_specs=in_specs,
        out_specs=pl.BlockSpec((None, TB, D_MODEL), lambda b, t: (b, t, 0)),
        out_shape=jax.ShapeDtypeStruct((B, T, D_MODEL), x.dtype),
        scratch_shapes=scratch,
        compiler_params=pltpu.CompilerParams(
            dimension_semantics=("arbitrary", "arbitrary"),
            vmem_limit_bytes=VMEM_LIMIT_BYTES),
        name="hybrid_block",
    )(x, mod, *consts)
```

```python
import functools

import numpy as np
import jax
import jax.numpy as jnp
from jax import lax
from jax.experimental import pallas as pl
from jax.experimental.pallas import tpu as pltpu

F32 = jnp.float32
BF16 = jnp.bfloat16

D_MODEL = 1024
D_POOL = 512
D_RWKV = 512
HEAD = 64
LORA = 64
POOL_WINDOWS = (2, 4, 8, 16)
POOL_GROUP = 128
NORM_EPS = 1e-6
GN_EPS = 64e-5
L2_EPS = 1e-12

COL_U = 0
COL_PZ = 512
COL_R = 1024
COL_K = 1536
COL_V = 2048
COL_Z = 2560
COL_LO = 3072
N_IN = 3200
SEG0 = 1024

CHUNK = 64
GROUP = 256
HEADS_PER_GROUP = GROUP // HEAD
N_GROUPS = D_RWKV // GROUP
TB = 256
PAD = 24

ROW_W0, ROW_A0, ROW_KK, ROW_KA, ROW_RK, ROW_LNW, ROW_LNB, ROW_PSCALE = range(8)

VMEM_LIMIT_BYTES = 56 * 1024 * 1024


def _split_hi_lo(x):
    hi = x.astype(BF16)
    lo = (x - hi.astype(F32)).astype(BF16)
    return hi, lo


def _dot(a, b):
    return jnp.dot(a, b, preferred_element_type=F32)


def _dot_hilo(x, w_bf16):
    hi, lo = _split_hi_lo(x)
    return _dot(hi, w_bf16) + _dot(lo, w_bf16)


def _sigmoid(x):
    return 1.0 / (1.0 + jnp.exp(-x))


def _ada_kernel(c_ref, w_ref, b_ref, o_ref):
    c_hi, c_lo = _split_hi_lo(c_ref[...])
    w_hi, w_lo = _split_hi_lo(w_ref[...])
    acc = _dot(c_hi, w_hi) + _dot(c_lo, w_hi) + _dot(c_hi, w_lo)
    o_ref[...] = acc + b_ref[...]


def _ada_mod(c_pad, w_ada, b_ada):
    n = w_ada.shape[1]
    bn = 768
    return pl.pallas_call(
        _ada_kernel,
        grid=(n // bn,),
        in_specs=[
            pl.BlockSpec((c_pad.shape[0], D_MODEL), lambda j: (0, 0)),
            pl.BlockSpec((D_MODEL, bn), lambda j: (0, j)),
            pl.BlockSpec((1, bn), lambda j: (0, j)),
        ],
        out_specs=pl.BlockSpec((c_pad.shape[0], bn), lambda j: (0, j)),
        out_shape=jax.ShapeDtypeStruct((c_pad.shape[0], n), F32),
        name="adaln_mod",
    )(c_pad, w_ada, b_ada)


def _block_diag(x, bdmask):
    xb = x.astype(BF16)
    return jnp.concatenate([xb] * HEADS_PER_GROUP, axis=0) * bdmask


def _diag_blocks(x, head_masks):
    out = jnp.where(head_masks[0], x[0:HEAD, :], 0.0)
    for h in range(1, HEADS_PER_GROUP):
        out = out + jnp.where(head_masks[h], x[h * HEAD:(h + 1) * HEAD, :], 0.0)
    return out


def _fused_kernel(x_ref, mod_ref, ng_ref, win_ref, mu_ref, lhi_ref, llo_ref, vec_ref,
                  pw_ref, wout_ref, fg_ref, bdmask_ref, trimask_ref, eye_ref, tmask_ref, cmat_ref,
                  segones_ref, o_ref,
                  p_ref, sa_ref, sb_ref, at_ref, rt_ref, bt_ref, kt_ref, bdc_ref, kdc_ref,
                  v_ref, cl_ref, y_ref, bon_ref, mix_ref, h_ref):
    t = pl.program_id(1)

    @pl.when(t == 0)
    def _():
        p_ref[0:PAD, :] = jnp.zeros((PAD, N_IN), F32)
        sa_ref[0:8, :] = jnp.zeros((8, D_POOL), F32)
        sb_ref[0:8, :] = jnp.zeros((8, D_POOL), F32)
        h_ref[...] = jnp.zeros(h_ref.shape, F32)

    shift = mod_ref[0:1, :]
    scale = mod_ref[1:2, :]
    gate = mod_ref[2:3, :]

    xb = x_ref[...]
    ms = jnp.mean(xb * xb, axis=-1, keepdims=True)
    hmod = (xb * lax.rsqrt(ms + NORM_EPS) * ng_ref[...]) * (1.0 + scale) + shift
    p_ref[PAD:PAD + TB, :] = _dot(hmod.astype(BF16), win_ref[...])

    def vec(row):
        return vec_ref[row:row + 1, :]

    n_ext = TB + PAD
    sa_ref[8:n_ext, :] = p_ref[8:n_ext, 0:512] + p_ref[7:n_ext - 1, 0:512]
    sb_ref[8:n_ext, 128:512] = sa_ref[8:n_ext, 128:512] + sa_ref[6:n_ext - 2, 128:512]
    sa_ref[8:n_ext, 256:512] = sb_ref[8:n_ext, 256:512] + sb_ref[4:n_ext - 4, 256:512]
    sb_ref[8:n_ext, 384:512] = sa_ref[8:n_ext, 384:512] + sa_ref[0:n_ext - 8, 384:512]
    pos = t * TB + lax.broadcasted_iota(jnp.int32, (TB, 1), 0) + 1
    wsum_refs = (sa_ref, sb_ref, sa_ref, sb_ref)
    diffs = []
    for g, win in enumerate(POOL_WINDOWS):
        lanes = slice(g * POOL_GROUP, (g + 1) * POOL_GROUP)
        cnt = jnp.minimum(pos, win).astype(F32)
        mean = wsum_refs[g][PAD:PAD + TB, lanes] / cnt
        diffs.append(mean - p_ref[PAD:PAD + TB, lanes])
    pz = p_ref[PAD:PAD + TB, COL_PZ:COL_PZ + D_POOL]
    for half in range(2):
        d2 = jnp.concatenate(diffs[2 * half:2 * half + 2], axis=1).astype(BF16)
        yp = _dot(d2, pw_ref[half])
        lanes = slice(256 * half, 256 * half + 256)
        zz = pz[:, lanes]
        yp = yp * vec_ref[ROW_PSCALE:ROW_PSCALE + 1, lanes] * (zz * _sigmoid(zz))
        mix_ref[:, lanes] = yp.astype(BF16)

    def lerp(col, width):
        cur = p_ref[PAD:PAD + TB, col:col + width]
        prev = p_ref[PAD - 1:PAD + TB - 1, col:col + width]
        return cur + (prev - cur) * mu_ref[0:1, col - SEG0:col - SEG0 + width]

    lo = lerp(COL_LO, 2 * LORA)
    lane = lax.broadcasted_iota(jnp.int32, lo.shape, 1)
    lo = jnp.where(lane < LORA, jnp.tanh(lo), lo)
    lo_hi, lo_lo = _split_hi_lo(lo)
    lin = _dot(lo_hi, lhi_ref[...]) + _dot(lo_lo, lhi_ref[...]) + _dot(lo_hi, llo_ref[...])
    logw = -float(np.exp(-0.5)) * _sigmoid(vec(ROW_W0) + lin[:, 0:D_RWKV])
    a = _sigmoid(vec(ROW_A0) + lin[:, D_RWKV:2 * D_RWKV])

    def segsum(xv):
        parts = []
        for half in range(N_GROUPS):
            parts.append(_dot_hilo(xv[:, GROUP * half:GROUP * (half + 1)], segones_ref[...]))
        return jnp.concatenate(parts, axis=1)

    lw_hi, lw_lo = _split_hi_lo(logw)
    cc = _dot(cmat_ref[...], lw_hi) + _dot(cmat_ref[...], lw_lo)
    cum = cc[0:TB, :]
    ctot = cc[TB:2 * TB, :]
    cl_ref[...] = ctot

    k = lerp(COL_K, D_RWKV)
    kkr = k * vec(ROW_KK)
    ssq = segsum(kkr * kkr)
    kk = kkr / jnp.maximum(jnp.sqrt(ssq), L2_EPS)
    k2 = k * (1.0 + (a - 1.0) * vec(ROW_KA))
    bv = kk * a
    p_inv = jnp.exp(-cum)
    p_end = jnp.exp(ctot - cum)
    bt_ref[...] = bv * p_inv
    kt_ref[...] = k2 * p_inv
    bdc_ref[...] = bv * p_end
    kdc_ref[...] = k2 * p_end
    at_ref[...] = -kk * jnp.exp(cum - logw)
    r = lerp(COL_R, D_RWKV)
    rt_ref[...] = r * jnp.exp(cum)
    v = lerp(COL_V, D_RWKV)
    v_ref[...] = v
    bon_ref[...] = segsum(r * k2 * vec(ROW_RK)) * v

    bdmask = bdmask_ref[...]
    lane_blk = lax.shift_right_logical(lax.broadcasted_iota(jnp.int32, (HEAD, GROUP), 1), 6)
    head_masks = [lane_blk == h for h in range(HEADS_PER_GROUP)]
    eye = eye_ref[...]

    n_chunks = TB // CHUNK
    inst = [(c, g) for c in range(n_chunks) for g in range(N_GROUPS)]

    def bd(xv):
        return _block_diag(xv, bdmask)

    def rows_of(c):
        return slice(c * CHUNK, (c + 1) * CHUNK)

    def lanes_of(g):
        return slice(GROUP * g, GROUP * (g + 1))

    lab, lak, mrb, mrk, vbd = {}, {}, {}, {}, {}
    for i in inst:
        rows, lanes = rows_of(i[0]), lanes_of(i[1])
        ar = jnp.concatenate([at_ref[rows, lanes], rt_ref[rows, lanes]], axis=0).astype(BF16)
        rhs1 = jnp.concatenate([bd(bt_ref[rows, lanes]), bd(kt_ref[rows, lanes])], axis=0)
        s = lax.dot_general(ar, rhs1, (((1,), (1,)), ((), ())), preferred_element_type=F32)
        s = s * trimask_ref[...]
        lab[i] = s[0:CHUNK, 0:GROUP]
        lak[i] = s[0:CHUNK, GROUP:2 * GROUP]
        mrb[i] = s[CHUNK:2 * CHUNK, 0:GROUP]
        mrk[i] = s[CHUNK:2 * CHUNK, GROUP:2 * GROUP]
        vbd[i] = bd(v_ref[rows, lanes])

    tinv, lpow, lakv = {}, {}, {}
    for i in inst:
        ldiag = lab[i] * tmask_ref[0]
        tinv[i] = eye + ldiag
        lpow[i] = _dot(ldiag.astype(BF16), bd(ldiag))
        lakv[i] = _dot(lak[i].astype(BF16), vbd[i])
    for _ in range(2):
        for i in inst:
            res = _dot(jnp.concatenate([lpow[i], tinv[i]], axis=0).astype(BF16), bd(lpow[i]))
            lpow[i] = res[0:CHUNK, :]
            tinv[i] = tinv[i] + res[CHUNK:2 * CHUNK, :]
    for i in inst:
        tinv[i] = tinv[i] + _dot(tinv[i].astype(BF16), bd(lpow[i]))
    for level in (1, 2):
        xm = {}
        for i in inst:
            xm[i] = _dot((lab[i] * tmask_ref[level]).astype(BF16), bd(tinv[i]))
        for i in inst:
            tinv[i] = tinv[i] + _dot(tinv[i].astype(BF16), bd(xm[i]))

    w, u0 = {}, {}
    for i in inst:
        rows, lanes = rows_of(i[0]), lanes_of(i[1])
        wu = _dot(tinv[i].astype(BF16),
                  jnp.concatenate([bd(at_ref[rows, lanes]), bd(lakv[i])], axis=1))
        w[i] = wu[:, 0:GROUP]
        u0[i] = wu[:, GROUP:2 * GROUP]
    q, y0, a_mat, g_mat = {}, {}, {}, {}
    for i in inst:
        rows, lanes = rows_of(i[0]), lanes_of(i[1])
        qy = _dot(mrb[i].astype(BF16), jnp.concatenate([bd(w[i]), bd(u0[i])], axis=1))
        q[i] = rt_ref[rows, lanes] + qy[:, 0:GROUP]
        y0[i] = qy[:, GROUP:2 * GROUP] + _dot(mrk[i].astype(BF16), vbd[i])
    for i in inst:
        rows, lanes = rows_of(i[0]), lanes_of(i[1])
        vv = v_ref[rows, lanes]
        bkd_t = jnp.concatenate([bdc_ref[rows, lanes], kdc_ref[rows, lanes]], axis=0).T
        rhs5 = jnp.concatenate(
            [jnp.concatenate([w[i], u0[i]], axis=1),
             jnp.concatenate([jnp.zeros_like(vv), vv], axis=1)], axis=0).astype(BF16)
        x1 = _dot(bkd_t.astype(BF16), rhs5)
        p_c = jnp.exp(cl_ref[i[0] * CHUNK:i[0] * CHUNK + 1, lanes])
        a_mat[i] = eye * p_c + _diag_blocks(x1[:, 0:GROUP], head_masks)
        g_mat[i] = _diag_blocks(x1[:, GROUP:2 * GROUP], head_masks)

    hs = [h_ref[g] for g in range(N_GROUPS)]
    for c in range(n_chunks):
        for g in range(N_GROUPS):
            i = (c, g)
            res = _dot(jnp.concatenate([a_mat[i], q[i]], axis=0).astype(BF16), bd(hs[g]))
            hs[g] = res[0:CHUNK, :] + g_mat[i]
            y_ref[rows_of(c), lanes_of(g)] = res[CHUNK:2 * CHUNK, :] + y0[i]
    for g in range(N_GROUPS):
        h_ref[g] = hs[g]

    yh = y_ref[...]
    mu = segsum(yh) * (1.0 / HEAD)
    dlt = yh - mu
    var = segsum(dlt * dlt) * (1.0 / HEAD)
    yn = dlt * lax.rsqrt(var + GN_EPS) * vec(ROW_LNW) + vec(ROW_LNB)
    z = lerp(COL_Z, D_RWKV)
    y_rwkv = (yn + bon_ref[...]) * (z * _sigmoid(z))
    mix_ref[:, D_POOL:D_POOL + D_RWKV] = y_rwkv.astype(BF16)

    out = _dot(mix_ref[...], wout_ref[...])
    xo = x_ref[...] + gate * out
    ms2 = jnp.mean(xo * xo, axis=-1, keepdims=True)
    o_ref[...] = xo * lax.rsqrt(ms2 + NORM_EPS) * fg_ref[...]

    p_ref[8:PAD, :] = p_ref[TB + 8:TB + PAD, :]


def _constants():
    hb = np.arange(GROUP) // HEAD
    bdmask = (hb[:, None] == hb[None, :]).astype(np.float32)
    ti = np.arange(CHUNK)[:, None]
    si = (np.arange(2 * GROUP) % HEAD)[None, :]
    trimask = np.concatenate([(ti > si), (ti >= si)], axis=0).astype(np.float32)
    sg = (np.arange(GROUP) % HEAD)[None, :]
    eye = (ti == sg).astype(np.float32)
    m16 = (ti // 16) == (sg // 16)
    m32 = (ti // 32) == (sg // 32)
    tmask = np.stack([m16, m32 & ~m16, ~m32], axis=0).astype(np.float32)
    rr = np.arange(TB)
    same = (rr[:, None] // CHUNK) == (rr[None, :] // CHUNK)
    tril = same & (rr[None, :] <= rr[:, None])
    cmat = np.concatenate([tril, same], axis=0).astype(np.float32)
    return bdmask, trimask, eye, tmask, cmat


def kernel(x, c, w_ada, b_ada, norm_g, w_in, pool_w, pool_scale, mu_shift, w0, w_up, a0, a_up,
           k_k, k_a, r_k, ln_w, ln_b, w_out, final_g):
    B, T, _ = x.shape
    assert w_ada.shape[0] == 1 and T % TB == 0
    l = 0

    c_pad = jnp.zeros((8, D_MODEL), F32).at[0:B].set(c)
    mod = _ada_mod(c_pad, w_ada[l], b_ada[l][None, :])[0:B].reshape(B, 3, D_MODEL)

    s0 = 2 * D_POOL
    spans = [(0, s0), (s0, s0 + 512), (s0 + 576, s0 + 1088), (s0 + 1088, s0 + 1600),
             (s0 + 1664, s0 + 2176), (s0 + 512, s0 + 576), (s0 + 1600, s0 + 1664)]
    w_in_p = jnp.concatenate([w_in[l][:, a:b] for a, b in spans], axis=1).astype(BF16)
    mu_p = jnp.concatenate([mu_shift[l][a - s0:b - s0] for a, b in spans[1:]], axis=0)[None, :]
    lora = jnp.zeros((2 * LORA, 2 * D_RWKV), F32)
    lora = lora.at[0:LORA, 0:D_RWKV].set(w_up[l]).at[LORA:, D_RWKV:].set(a_up[l])
    lora_hi, lora_lo = _split_hi_lo(lora)
    vec = jnp.stack([w0[l], a0[l], k_k[l], k_a[l], r_k[l], ln_w[l], ln_b[l], pool_scale[l]], axis=0)
    pw = jnp.zeros((2, 256, 256), F32)
    for g in range(4):
        o = (g % 2) * POOL_GROUP
        pw = pw.at[g // 2, o:o + POOL_GROUP, o:o + POOL_GROUP].set(pool_w[l, g])
    pw = pw.astype(BF16)
    w_out_b = w_out[l].astype(BF16)

    bdmask, trimask, eye, tmask, cmat = _constants()
    bdmask_b = jnp.asarray(bdmask, BF16)
    segones = bdmask_b
    trimask = jnp.asarray(trimask)
    eye = jnp.asarray(eye)
    tmask = jnp.asarray(tmask)
    cmat = jnp.asarray(cmat, BF16)

    def full(a):
        nd = a.ndim
        return pl.BlockSpec(a.shape, lambda b, t, _nd=nd: (0,) * _nd)

    consts = (norm_g[l][None, :], w_in_p, mu_p, lora_hi, lora_lo, vec, pw, w_out_b,
              final_g[None, :], bdmask_b, trimask, eye, tmask, cmat, segones)
    in_specs = [
        pl.BlockSpec((None, TB, D_MODEL), lambda b, t: (b, t, 0)),
        pl.BlockSpec((None, 3, D_MODEL), lambda b, t: (b, 0, 0)),
    ] + [full(a) for a in consts]

    blk = lambda: pltpu.VMEM((TB, D_RWKV), F32)
    scratch = [
        pltpu.VMEM((TB + PAD, N_IN), F32),
        pltpu.VMEM((TB + PAD, D_POOL), F32),
        pltpu.VMEM((TB + PAD, D_POOL), F32),
        blk(), blk(), blk(), blk(), blk(), blk(),
        blk(), blk(), blk(), blk(),
        pltpu.VMEM((TB, D_MODEL), BF16),
        pltpu.VMEM((N_GROUPS, HEAD, GROUP), F32),
    ]
    return pl.pallas_call(
        _fused_kernel,
        grid=(B, T // TB),
        in_specs=in_specs,
        out_specs=pl.BlockSpec((None, TB, D_MODEL), lambda b, t: (b, t, 0)),
        out_shape=jax.ShapeDtypeStruct((B, T, D_MODEL), x.dtype),
        scratch_shapes=scratch,
        compiler_params=pltpu.CompilerParams(
            dimension_semantics=("arbitrary", "arbitrary"),
            vmem_limit_bytes=VMEM_LIMIT_BYTES),
        name="hybrid_block",
    )(x, mod, *consts)
```

```python
import functools

import numpy as np
import jax
import jax.numpy as jnp
from jax import lax
from jax.experimental import pallas as pl
from jax.experimental.pallas import tpu as pltpu

F32 = jnp.float32
BF16 = jnp.bfloat16

D_MODEL = 1024
D_POOL = 512
D_RWKV = 512
HEAD = 64
LORA = 64
POOL_WINDOWS = (2, 4, 8, 16)
POOL_GROUP = 128
NORM_EPS = 1e-6
GN_EPS = 64e-5
L2_EPS = 1e-12

COL_U = 0
COL_PZ = 512
COL_R = 1024
COL_K = 1536
COL_V = 2048
COL_Z = 2560
COL_LO = 3072
N_IN = 3200
SEG0 = 1024

CHUNK = 64
GROUP = 256
HEADS_PER_GROUP = GROUP // HEAD
N_GROUPS = D_RWKV // GROUP
TB = 256
PAD = 24

ROW_W0, ROW_A0, ROW_KK, ROW_KA, ROW_RK, ROW_LNW, ROW_LNB, ROW_PSCALE = range(8)

VMEM_LIMIT_BYTES = 56 * 1024 * 1024


def _split_hi_lo(x):
    hi = x.astype(BF16)
    lo = (x - hi.astype(F32)).astype(BF16)
    return hi, lo


def _dot(a, b):
    return jnp.dot(a, b, preferred_element_type=F32)


def _dot_hilo(x, w_bf16):
    hi, lo = _split_hi_lo(x)
    return _dot(hi, w_bf16) + _dot(lo, w_bf16)


def _sigmoid(x):
    return 1.0 / (1.0 + jnp.exp(-x))


def _ada_kernel(c_ref, w_ref, b_ref, o_ref):
    c_hi, c_lo = _split_hi_lo(c_ref[...])
    w_hi, w_lo = _split_hi_lo(w_ref[...])
    acc = _dot(c_hi, w_hi) + _dot(c_lo, w_hi) + _dot(c_hi, w_lo)
    o_ref[...] = acc + b_ref[...]


def _ada_mod(c_pad, w_ada, b_ada):
    n = w_ada.shape[1]
    bn = 768
    return pl.pallas_call(
        _ada_kernel,
        grid=(n // bn,),
        in_specs=[
            pl.BlockSpec((c_pad.shape[0], D_MODEL), lambda j: (0, 0)),
            pl.BlockSpec((D_MODEL, bn), lambda j: (0, j)),
            pl.BlockSpec((1, bn), lambda j: (0, j)),
        ],
        out_specs=pl.BlockSpec((c_pad.shape[0], bn), lambda j: (0, j)),
        out_shape=jax.ShapeDtypeStruct((c_pad.shape[0], n), F32),
        name="adaln_mod",
    )(c_pad, w_ada, b_ada)


def _head_blocks(x, bdmask):
    xb = x.astype(BF16)
    return [xb * bdmask[h * HEAD:(h + 1) * HEAD, :] for h in range(HEADS_PER_GROUP)]


def _block_diag(x, bdmask):
    return jnp.concatenate(_head_blocks(x, bdmask), axis=0)


T_HEAD_ORDER = (0, 2, 1, 3)


def _heads_to_rows_t(x):
    xt = jnp.concatenate([x[:, 0:2 * HEAD], x[:, 2 * HEAD:4 * HEAD]], axis=0).T
    return jnp.concatenate([xt[0:HEAD, :], xt[HEAD:2 * HEAD, :]], axis=1)


def _fused_kernel(x_ref, mod_ref, ng_ref, win_ref, mu_ref, lhi_ref, llo_ref, vec_ref,
                  pw_ref, wout_ref, fg_ref, bdmask_ref, trimask_ref, eye_ref, tmask_ref, cmat_ref,
                  segones_ref, o_ref,
                  p_ref, sa_ref, sb_ref, at_ref, rt_ref, bt_ref, kt_ref, bdc_ref, kdc_ref,
                  v_ref, y_ref, bon_ref, mix_ref, h_ref):
    t = pl.program_id(1)

    @pl.when(t == 0)
    def _():
        p_ref[0:PAD, :] = jnp.zeros((PAD, N_IN), F32)
        sa_ref[0:8, :] = jnp.zeros((8, D_POOL), F32)
        sb_ref[0:8, :] = jnp.zeros((8, D_POOL), F32)
        h_ref[...] = jnp.zeros(h_ref.shape, F32)

    shift = mod_ref[0:1, :]
    scale = mod_ref[1:2, :]
    gate = mod_ref[2:3, :]

    xb = x_ref[...]
    ms = jnp.mean(xb * xb, axis=-1, keepdims=True)
    hmod = (xb * lax.rsqrt(ms + NORM_EPS) * ng_ref[...]) * (1.0 + scale) + shift
    p_ref[PAD:PAD + TB, :] = _dot(hmod.astype(BF16), win_ref[...])

    def vec(row):
        return vec_ref[row:row + 1, :]

    n_ext = TB + PAD
    sa_ref[8:n_ext, :] = p_ref[8:n_ext, 0:512] + p_ref[7:n_ext - 1, 0:512]
    sb_ref[8:n_ext, 128:512] = sa_ref[8:n_ext, 128:512] + sa_ref[6:n_ext - 2, 128:512]
    sa_ref[8:n_ext, 256:512] = sb_ref[8:n_ext, 256:512] + sb_ref[4:n_ext - 4, 256:512]
    sb_ref[8:n_ext, 384:512] = sa_ref[8:n_ext, 384:512] + sa_ref[0:n_ext - 8, 384:512]
    pos = t * TB + lax.broadcasted_iota(jnp.int32, (TB, 1), 0) + 1
    wsum_refs = (sa_ref, sb_ref, sa_ref, sb_ref)
    diffs = []
    for g, win in enumerate(POOL_WINDOWS):
        lanes = slice(g * POOL_GROUP, (g + 1) * POOL_GROUP)
        cnt = jnp.minimum(pos, win).astype(F32)
        mean = wsum_refs[g][PAD:PAD + TB, lanes] / cnt
        diffs.append(mean - p_ref[PAD:PAD + TB, lanes])
    pz = p_ref[PAD:PAD + TB, COL_PZ:COL_PZ + D_POOL]
    for half in range(2):
        d2 = jnp.concatenate(diffs[2 * half:2 * half + 2], axis=1).astype(BF16)
        yp = _dot(d2, pw_ref[half])
        lanes = slice(256 * half, 256 * half + 256)
        zz = pz[:, lanes]
        yp = yp * vec_ref[ROW_PSCALE:ROW_PSCALE + 1, lanes] * (zz * _sigmoid(zz))
        mix_ref[:, lanes] = yp.astype(BF16)

    def lerp(col, width):
        cur = p_ref[PAD:PAD + TB, col:col + width]
        prev = p_ref[PAD - 1:PAD + TB - 1, col:col + width]
        return cur + (prev - cur) * mu_ref[0:1, col - SEG0:col - SEG0 + width]

    lo = lerp(COL_LO, 2 * LORA)
    lane = lax.broadcasted_iota(jnp.int32, lo.shape, 1)
    lo = jnp.where(lane < LORA, jnp.tanh(lo), lo)
    lo_hi, lo_lo = _split_hi_lo(lo)
    lin = _dot(lo_hi, lhi_ref[...]) + _dot(lo_lo, lhi_ref[...]) + _dot(lo_hi, llo_ref[...])
    logw = -float(np.exp(-0.5)) * _sigmoid(vec(ROW_W0) + lin[:, 0:D_RWKV])
    a = _sigmoid(vec(ROW_A0) + lin[:, D_RWKV:2 * D_RWKV])

    def segsum(xv):
        parts = []
        for half in range(N_GROUPS):
            parts.append(_dot_hilo(xv[:, GROUP * half:GROUP * (half + 1)], segones_ref[...]))
        return jnp.concatenate(parts, axis=1)

    lw_hi, lw_lo = _split_hi_lo(logw)
    cum = _dot(cmat_ref[...], lw_hi) + _dot(cmat_ref[...], lw_lo)
    n_chunks = TB // CHUNK
    cum_end = [cum[(c + 1) * CHUNK - 1:(c + 1) * CHUNK, :] for c in range(n_chunks)]
    ctot = jnp.concatenate([jnp.broadcast_to(ce, (CHUNK, D_RWKV)) for ce in cum_end], axis=0)
    p_chunk = [jnp.exp(ce) for ce in cum_end]

    k = lerp(COL_K, D_RWKV)
    kkr = k * vec(ROW_KK)
    ssq = segsum(kkr * kkr)
    kk = kkr / jnp.maximum(jnp.sqrt(ssq), L2_EPS)
    k2 = k * (1.0 + (a - 1.0) * vec(ROW_KA))
    bv = kk * a
    p_inv = jnp.exp(-cum)
    p_end = jnp.exp(ctot - cum)
    bt_ref[...] = bv * p_inv
    kt_ref[...] = k2 * p_inv
    bdc_ref[...] = bv * p_end
    kdc_ref[...] = k2 * p_end
    at_ref[...] = -kk * jnp.exp(cum - logw)
    r = lerp(COL_R, D_RWKV)
    rt_ref[...] = r * jnp.exp(cum)
    v = lerp(COL_V, D_RWKV)
    v_ref[...] = v
    bon_ref[...] = segsum(r * k2 * vec(ROW_RK)) * v

    bdmask = bdmask_ref[...]
    eye = eye_ref[...]

    inst = [(c, g) for c in range(n_chunks) for g in range(N_GROUPS)]

    def blocks(xv):
        return _head_blocks(xv, bdmask)

    def bd(xv):
        return _block_diag(xv, bdmask)

    def bdt(blks):
        return jnp.concatenate([blks[h] for h in T_HEAD_ORDER], axis=0)

    def rows_of(c):
        return slice(c * CHUNK, (c + 1) * CHUNK)

    def lanes_of(g):
        return slice(GROUP * g, GROUP * (g + 1))

    lab, lak, mrb, mrk = {}, {}, {}, {}
    for i in inst:
        rows, lanes = rows_of(i[0]), lanes_of(i[1])
        ar = jnp.concatenate([at_ref[rows, lanes], rt_ref[rows, lanes]], axis=0).astype(BF16)
        rhs1 = jnp.concatenate([bdt(blocks(_heads_to_rows_t(bt_ref[rows, lanes]))),
                                bdt(blocks(_heads_to_rows_t(kt_ref[rows, lanes])))], axis=1)
        s = _dot(ar, rhs1) * trimask_ref[...]
        lab[i] = s[0:CHUNK, 0:GROUP]
        lak[i] = s[0:CHUNK, GROUP:2 * GROUP]
        mrb[i] = s[CHUNK:2 * CHUNK, 0:GROUP]
        mrk[i] = s[CHUNK:2 * CHUNK, GROUP:2 * GROUP]

    tinv, lpow, lakv, mrkv, kv = {}, {}, {}, {}, {}
    for i in inst:
        rows, lanes = rows_of(i[0]), lanes_of(i[1])
        ldiag = lab[i] * tmask_ref[0]
        tinv[i] = eye + ldiag
        lpow[i] = _dot(ldiag.astype(BF16), bd(ldiag))
        kd_t = _heads_to_rows_t(kdc_ref[rows, lanes])
        xv = _dot(jnp.concatenate([lak[i], mrk[i], kd_t], axis=0).astype(BF16),
                  bdt(blocks(v_ref[rows, lanes])))
        lakv[i] = xv[0:CHUNK, :]
        mrkv[i] = xv[CHUNK:2 * CHUNK, :]
        kv[i] = xv[2 * CHUNK:3 * CHUNK, :]
    for _ in range(2):
        for i in inst:
            res = _dot(jnp.concatenate([lpow[i], tinv[i]], axis=0).astype(BF16), bd(lpow[i]))
            lpow[i] = res[0:CHUNK, :]
            tinv[i] = tinv[i] + res[CHUNK:2 * CHUNK, :]
    for i in inst:
        tinv[i] = tinv[i] + _dot(tinv[i].astype(BF16), bd(lpow[i]))
    for level in (1, 2):
        xm = {}
        for i in inst:
            xm[i] = _dot((lab[i] * tmask_ref[level]).astype(BF16), bd(tinv[i]))
        for i in inst:
            tinv[i] = tinv[i] + _dot(tinv[i].astype(BF16), bd(xm[i]))

    w, u0 = {}, {}
    for i in inst:
        rows, lanes = rows_of(i[0]), lanes_of(i[1])
        wu = _dot(tinv[i].astype(BF16),
                  jnp.concatenate([bdt(blocks(at_ref[rows, lanes])), bdt(blocks(lakv[i]))], axis=1))
        w[i] = wu[:, 0:GROUP]
        u0[i] = wu[:, GROUP:2 * GROUP]
    q, y0, a_mat, g_mat = {}, {}, {}, {}
    for i in inst:
        rows, lanes = rows_of(i[0]), lanes_of(i[1])
        bd_t = _heads_to_rows_t(bdc_ref[rows, lanes])
        res = _dot(jnp.concatenate([mrb[i], bd_t], axis=0).astype(BF16),
                   jnp.concatenate([bdt(blocks(w[i])), bdt(blocks(u0[i]))], axis=1))
        q[i] = rt_ref[rows, lanes] + res[0:CHUNK, 0:GROUP]
        y0[i] = res[0:CHUNK, GROUP:2 * GROUP] + mrkv[i]
        a_mat[i] = eye * p_chunk[i[0]][:, lanes] + res[CHUNK:2 * CHUNK, 0:GROUP]
        g_mat[i] = res[CHUNK:2 * CHUNK, GROUP:2 * GROUP] + kv[i]

    hs = [h_ref[g] for g in range(N_GROUPS)]
    for c in range(n_chunks):
        for g in range(N_GROUPS):
            i = (c, g)
            res = _dot(jnp.concatenate([a_mat[i], q[i]], axis=0).astype(BF16), bd(hs[g]))
            hs[g] = res[0:CHUNK, :] + g_mat[i]
            y_ref[rows_of(c), lanes_of(g)] = res[CHUNK:2 * CHUNK, :] + y0[i]
    for g in range(N_GROUPS):
        h_ref[g] = hs[g]

    yh = y_ref[...]
    mu = segsum(yh) * (1.0 / HEAD)
    dlt = yh - mu
    var = segsum(dlt * dlt) * (1.0 / HEAD)
    yn = dlt * lax.rsqrt(var + GN_EPS) * vec(ROW_LNW) + vec(ROW_LNB)
    z = lerp(COL_Z, D_RWKV)
    y_rwkv = (yn + bon_ref[...]) * (z * _sigmoid(z))
    mix_ref[:, D_POOL:D_POOL + D_RWKV] = y_rwkv.astype(BF16)

    out = _dot(mix_ref[...], wout_ref[...])
    xo = x_ref[...] + gate * out
    ms2 = jnp.mean(xo * xo, axis=-1, keepdims=True)
    o_ref[...] = xo * lax.rsqrt(ms2 + NORM_EPS) * fg_ref[...]

    p_ref[8:PAD, :] = p_ref[TB + 8:TB + PAD, :]


def _constants():
    hb = np.arange(GROUP) // HEAD
    bdmask = (hb[:, None] == hb[None, :]).astype(np.float32)
    ti = np.arange(CHUNK)[:, None]
    si = (np.arange(2 * GROUP) % HEAD)[None, :]
    trimask = np.concatenate([(ti > si), (ti >= si)], axis=0).astype(np.float32)
    sg = (np.arange(GROUP) % HEAD)[None, :]
    eye = (ti == sg).astype(np.float32)
    m16 = (ti // 16) == (sg // 16)
    m32 = (ti // 32) == (sg // 32)
    tmask = np.stack([m16, m32 & ~m16, ~m32], axis=0).astype(np.float32)
    rr = np.arange(TB)
    same = (rr[:, None] // CHUNK) == (rr[None, :] // CHUNK)
    tril = same & (rr[None, :] <= rr[:, None])
    cmat = tril.astype(np.float32)
    return bdmask, trimask, eye, tmask, cmat


def kernel(x, c, w_ada, b_ada, norm_g, w_in, pool_w, pool_scale, mu_shift, w0, w_up, a0, a_up,
           k_k, k_a, r_k, ln_w, ln_b, w_out, final_g):
    B, T, _ = x.shape
    assert w_ada.shape[0] == 1 and T % TB == 0
    l = 0

    c_pad = jnp.zeros((8, D_MODEL), F32).at[0:B].set(c)
    mod = _ada_mod(c_pad, w_ada[l], b_ada[l][None, :])[0:B].reshape(B, 3, D_MODEL)

    s0 = 2 * D_POOL
    spans = [(0, s0), (s0, s0 + 512), (s0 + 576, s0 + 1088), (s0 + 1088, s0 + 1600),
             (s0 + 1664, s0 + 2176), (s0 + 512, s0 + 576), (s0 + 1600, s0 + 1664)]
    w_in_p = jnp.concatenate([w_in[l][:, a:b] for a, b in spans], axis=1).astype(BF16)
    mu_p = jnp.concatenate([mu_shift[l][a - s0:b - s0] for a, b in spans[1:]], axis=0)[None, :]
    lora = jnp.zeros((2 * LORA, 2 * D_RWKV), F32)
    lora = lora.at[0:LORA, 0:D_RWKV].set(w_up[l]).at[LORA:, D_RWKV:].set(a_up[l])
    lora_hi, lora_lo = _split_hi_lo(lora)
    vec = jnp.stack([w0[l], a0[l], k_k[l], k_a[l], r_k[l], ln_w[l], ln_b[l], pool_scale[l]], axis=0)
    pw = jnp.zeros((2, 256, 256), F32)
    for g in range(4):
        o = (g % 2) * POOL_GROUP
        pw = pw.at[g // 2, o:o + POOL_GROUP, o:o + POOL_GROUP].set(pool_w[l, g])
    pw = pw.astype(BF16)
    w_out_b = w_out[l].astype(BF16)

    bdmask, trimask, eye, tmask, cmat = _constants()
    bdmask_b = jnp.asarray(bdmask, BF16)
    segones = bdmask_b
    trimask = jnp.asarray(trimask)
    eye = jnp.asarray(eye)
    tmask = jnp.asarray(tmask)
    cmat = jnp.asarray(cmat, BF16)

    def full(a):
        nd = a.ndim
        return pl.BlockSpec(a.shape, lambda b, t, _nd=nd: (0,) * _nd)

    consts = (norm_g[l][None, :], w_in_p, mu_p, lora_hi, lora_lo, vec, pw, w_out_b,
              final_g[None, :], bdmask_b, trimask, eye, tmask, cmat, segones)
    in_specs = [
        pl.BlockSpec((None, TB, D_MODEL), lambda b, t: (b, t, 0)),
        pl.BlockSpec((None, 3, D_MODEL), lambda b, t: (b, 0, 0)),
    ] + [full(a) for a in consts]

    blk = lambda: pltpu.VMEM((TB, D_RWKV), F32)
    scratch = [
        pltpu.VMEM((TB + PAD, N_IN), F32),
        pltpu.VMEM((TB + PAD, D_POOL), F32),
        pltpu.VMEM((TB + PAD, D_POOL), F32),
        blk(), blk(), blk(), blk(), blk(), blk(),
        blk(), blk(), blk(),
        pltpu.VMEM((TB, D_MODEL), BF16),
        pltpu.VMEM((N_GROUPS, HEAD, GROUP), F32),
    ]
    return pl.pallas_call(
        _fused_kernel,
        grid=(B, T // TB),
        in_specs=in_specs,
        out_specs=pl.BlockSpec((None, TB, D_MODEL), lambda b, t: (b, t, 0)),
        out_shape=jax.ShapeDtypeStruct((B, T, D_MODEL), x.dtype),
        scratch_shapes=scratch,
        compiler_params=pltpu.CompilerParams(
            dimension_semantics=("arbitrary", "arbitrary"),
            vmem_limit_bytes=VMEM_LIMIT_BYTES),
        name="hybrid_block",
    )(x, mod, *consts)
```

```python
import functools

import numpy as np
import jax
import jax.numpy as jnp
from jax import lax
from jax.experimental import pallas as pl
from jax.experimental.pallas import tpu as pltpu

F32 = jnp.float32
BF16 = jnp.bfloat16

D_MODEL = 1024
D_POOL = 512
D_RWKV = 512
HEAD = 64
LORA = 64
POOL_WINDOWS = (2, 4, 8, 16)
POOL_GROUP = 128
NORM_EPS = 1e-6
GN_EPS = 64e-5
L2_EPS = 1e-12

COL_U = 0
COL_PZ = 512
COL_R = 1024
COL_K = 1536
COL_V = 2048
COL_Z = 2560
COL_LO = 3072
N_IN = 3200
SEG0 = 1024

CHUNK = 64
GROUP = 256
HEADS_PER_GROUP = GROUP // HEAD
N_GROUPS = D_RWKV // GROUP
TB = 256
PAD = 24

ROW_W0, ROW_A0, ROW_KK, ROW_KA, ROW_RK, ROW_LNW, ROW_LNB, ROW_PSCALE = range(8)

VMEM_LIMIT_BYTES = 56 * 1024 * 1024


def _split_hi_lo(x):
    hi = x.astype(BF16)
    lo = (x - hi.astype(F32)).astype(BF16)
    return hi, lo


def _dot(a, b):
    return jnp.dot(a, b, preferred_element_type=F32)


def _sigmoid(x):
    return 0.5 * jnp.tanh(0.5 * x) + 0.5


def _ada_kernel(c_ref, w_ref, b_ref, o_ref):
    c_hi, c_lo = _split_hi_lo(c_ref[...])
    w_hi, w_lo = _split_hi_lo(w_ref[...])
    acc = _dot(c_hi, w_hi) + _dot(c_lo, w_hi) + _dot(c_hi, w_lo)
    o_ref[...] = acc + b_ref[...]


def _ada_mod(c_pad, w_ada, b_ada):
    n = w_ada.shape[1]
    bn = 768
    return pl.pallas_call(
        _ada_kernel,
        grid=(n // bn,),
        in_specs=[
            pl.BlockSpec((c_pad.shape[0], D_MODEL), lambda j: (0, 0)),
            pl.BlockSpec((D_MODEL, bn), lambda j: (0, j)),
            pl.BlockSpec((1, bn), lambda j: (0, j)),
        ],
        out_specs=pl.BlockSpec((c_pad.shape[0], bn), lambda j: (0, j)),
        out_shape=jax.ShapeDtypeStruct((c_pad.shape[0], n), F32),
        name="adaln_mod",
    )(c_pad, w_ada, b_ada)


def _head_blocks(x, bdmask):
    xb = x.astype(BF16)
    return [xb * bdmask[h * HEAD:(h + 1) * HEAD, :] for h in range(HEADS_PER_GROUP)]


def _block_diag(x, bdmask):
    return jnp.concatenate(_head_blocks(x, bdmask), axis=0)


T_HEAD_ORDER = (0, 2, 1, 3)


def _heads_to_rows_t(x):
    xt = jnp.concatenate([x[:, 0:2 * HEAD], x[:, 2 * HEAD:4 * HEAD]], axis=0).T
    return jnp.concatenate([xt[0:HEAD, :], xt[HEAD:2 * HEAD, :]], axis=1)


def _fused_kernel(x_ref, mod_ref, ng_ref, win_ref, mu_ref, lora_ref, vec_ref,
                  pw_ref, wout_ref, fg_ref, bdmask_ref, trimask_ref, eye_ref, tmask_ref, cmat_ref,
                  segones_ref, o_ref,
                  p_ref, sa_ref, sb_ref, at_ref, rt_ref, bt_ref, kt_ref, bdc_ref, kdc_ref,
                  v_ref, y_ref, bon_ref, mix_ref, h_ref):
    t = pl.program_id(1)

    @pl.when(t == 0)
    def _():
        p_ref[0:PAD, :] = jnp.zeros((PAD, N_IN), F32)
        sa_ref[0:8, :] = jnp.zeros((8, D_POOL), F32)
        sb_ref[0:8, :] = jnp.zeros((8, D_POOL), F32)
        h_ref[...] = jnp.zeros(h_ref.shape, F32)

    shift = mod_ref[0:1, :]
    scale = mod_ref[1:2, :]
    gate = mod_ref[2:3, :]

    xb = x_ref[...]
    ms = jnp.mean(xb * xb, axis=-1, keepdims=True)
    hmod = (xb * lax.rsqrt(ms + NORM_EPS) * ng_ref[...]) * (1.0 + scale) + shift
    p_ref[PAD:PAD + TB, :] = _dot(hmod.astype(BF16), win_ref[...])

    def vec(row):
        return vec_ref[row:row + 1, :]

    n_ext = TB + PAD
    sa_ref[8:n_ext, :] = p_ref[8:n_ext, 0:512] + p_ref[7:n_ext - 1, 0:512]
    sb_ref[8:n_ext, 128:512] = sa_ref[8:n_ext, 128:512] + sa_ref[6:n_ext - 2, 128:512]
    sa_ref[8:n_ext, 256:512] = sb_ref[8:n_ext, 256:512] + sb_ref[4:n_ext - 4, 256:512]
    sb_ref[8:n_ext, 384:512] = sa_ref[8:n_ext, 384:512] + sa_ref[0:n_ext - 8, 384:512]
    pos = t * TB + lax.broadcasted_iota(jnp.int32, (TB, 1), 0) + 1
    wsum_refs = (sa_ref, sb_ref, sa_ref, sb_ref)
    diffs = []
    for g, win in enumerate(POOL_WINDOWS):
        lanes = slice(g * POOL_GROUP, (g + 1) * POOL_GROUP)
        cnt = jnp.minimum(pos, win).astype(F32)
        mean = wsum_refs[g][PAD:PAD + TB, lanes] / cnt
        diffs.append(mean - p_ref[PAD:PAD + TB, lanes])
    pz = p_ref[PAD:PAD + TB, COL_PZ:COL_PZ + D_POOL]
    for half in range(2):
        d2 = jnp.concatenate(diffs[2 * half:2 * half + 2], axis=1).astype(BF16)
        yp = _dot(d2, pw_ref[half])
        lanes = slice(256 * half, 256 * half + 256)
        zz = pz[:, lanes]
        yp = yp * vec_ref[ROW_PSCALE:ROW_PSCALE + 1, lanes] * (zz * _sigmoid(zz))
        mix_ref[:, lanes] = yp.astype(BF16)

    def lerp(col, width):
        cur = p_ref[PAD:PAD + TB, col:col + width]
        prev = p_ref[PAD - 1:PAD + TB - 1, col:col + width]
        return cur + (prev - cur) * mu_ref[0:1, col - SEG0:col - SEG0 + width]

    lo = lerp(COL_LO, 2 * LORA)
    lane = lax.broadcasted_iota(jnp.int32, lo.shape, 1)
    lo = jnp.where(lane < LORA, jnp.tanh(lo), lo)
    lo_hi, lo_lo = _split_hi_lo(lo)
    lin = _dot(jnp.concatenate([lo_hi, lo_lo], axis=1), lora_ref[...])
    logw = -float(np.exp(-0.5)) * _sigmoid(vec(ROW_W0) + lin[:, 0:D_RWKV])
    a = _sigmoid(vec(ROW_A0) + lin[:, D_RWKV:2 * D_RWKV])

    def segsum(xv):
        parts = []
        for half in range(N_GROUPS):
            parts.append(_dot(xv[:, GROUP * half:GROUP * (half + 1)].astype(BF16), segones_ref[...]))
        return jnp.concatenate(parts, axis=1)

    lw_hi, lw_lo = _split_hi_lo(logw)
    cum = _dot(cmat_ref[...], lw_hi) + _dot(cmat_ref[...], lw_lo)
    n_chunks = TB // CHUNK
    cum_end = [cum[(c + 1) * CHUNK - 1:(c + 1) * CHUNK, :] for c in range(n_chunks)]
    ctot = jnp.concatenate([jnp.broadcast_to(ce, (CHUNK, D_RWKV)) for ce in cum_end], axis=0)
    p_chunk = [jnp.exp(ce) for ce in cum_end]

    k = lerp(COL_K, D_RWKV)
    kkr = k * vec(ROW_KK)
    ssq = segsum(kkr * kkr)
    kk = kkr * lax.rsqrt(jnp.maximum(ssq, L2_EPS * L2_EPS))
    k2 = k * (1.0 + (a - 1.0) * vec(ROW_KA))
    bv = kk * a
    p_inv = jnp.exp(-cum)
    p_end = jnp.exp(ctot - cum)
    bt_ref[...] = bv * p_inv
    kt_ref[...] = k2 * p_inv
    bdc_ref[...] = bv * p_end
    kdc_ref[...] = k2 * p_end
    at_ref[...] = -kk * jnp.exp(cum - logw)
    r = lerp(COL_R, D_RWKV)
    rt_ref[...] = r * jnp.exp(cum)
    v = lerp(COL_V, D_RWKV)
    v_ref[...] = v
    bon_ref[...] = segsum(r * k2 * vec(ROW_RK)) * v

    bdmask = bdmask_ref[...]
    eye = eye_ref[...]

    inst = [(c, g) for c in range(n_chunks) for g in range(N_GROUPS)]

    def blocks(xv):
        return _head_blocks(xv, bdmask)

    def bd(xv):
        return _block_diag(xv, bdmask)

    def bdt(blks):
        return jnp.concatenate([blks[h] for h in T_HEAD_ORDER], axis=0)

    def rows_of(c):
        return slice(c * CHUNK, (c + 1) * CHUNK)

    def lanes_of(g):
        return slice(GROUP * g, GROUP * (g + 1))

    lab, lak, mrb, mrk = {}, {}, {}, {}
    for i in inst:
        rows, lanes = rows_of(i[0]), lanes_of(i[1])
        ar = jnp.concatenate([at_ref[rows, lanes], rt_ref[rows, lanes]], axis=0).astype(BF16)
        rhs1 = jnp.concatenate([bdt(blocks(_heads_to_rows_t(bt_ref[rows, lanes]))),
                                bdt(blocks(_heads_to_rows_t(kt_ref[rows, lanes])))], axis=1)
        s = _dot(ar, rhs1) * trimask_ref[...]
        lab[i] = s[0:CHUNK, 0:GROUP]
        lak[i] = s[0:CHUNK, GROUP:2 * GROUP]
        mrb[i] = s[CHUNK:2 * CHUNK, 0:GROUP]
        mrk[i] = s[CHUNK:2 * CHUNK, GROUP:2 * GROUP]

    tinv, lpow, lakv, mrkv, kv = {}, {}, {}, {}, {}
    for i in inst:
        rows, lanes = rows_of(i[0]), lanes_of(i[1])
        ldiag = lab[i] * tmask_ref[0]
        tinv[i] = eye + ldiag
        lpow[i] = _dot(ldiag.astype(BF16), bd(ldiag))
        kd_t = _heads_to_rows_t(kdc_ref[rows, lanes])
        xv = _dot(jnp.concatenate([lak[i], mrk[i], kd_t], axis=0).astype(BF16),
                  bdt(blocks(v_ref[rows, lanes])))
        lakv[i] = xv[0:CHUNK, :]
        mrkv[i] = xv[CHUNK:2 * CHUNK, :]
        kv[i] = xv[2 * CHUNK:3 * CHUNK, :]
    for _ in range(2):
        for i in inst:
            res = _dot(jnp.concatenate([lpow[i], tinv[i]], axis=0).astype(BF16), bd(lpow[i]))
            lpow[i] = res[0:CHUNK, :]
            tinv[i] = tinv[i] + res[CHUNK:2 * CHUNK, :]
    for i in inst:
        tinv[i] = tinv[i] + _dot(tinv[i].astype(BF16), bd(lpow[i]))
    for level in (1, 2):
        xm = {}
        for i in inst:
            xm[i] = _dot((lab[i] * tmask_ref[level]).astype(BF16), bd(tinv[i]))
        for i in inst:
            tinv[i] = tinv[i] + _dot(tinv[i].astype(BF16), bd(xm[i]))

    w, u0 = {}, {}
    for i in inst:
        rows, lanes = rows_of(i[0]), lanes_of(i[1])
        wu = _dot(tinv[i].astype(BF16),
                  jnp.concatenate([bdt(blocks(at_ref[rows, lanes])), bdt(blocks(lakv[i]))], axis=1))
        w[i] = wu[:, 0:GROUP]
        u0[i] = wu[:, GROUP:2 * GROUP]
    q, y0, a_mat, g_mat = {}, {}, {}, {}
    for i in inst:
        rows, lanes = rows_of(i[0]), lanes_of(i[1])
        bd_t = _heads_to_rows_t(bdc_ref[rows, lanes])
        res = _dot(jnp.concatenate([mrb[i], bd_t], axis=0).astype(BF16),
                   jnp.concatenate([bdt(blocks(w[i])), bdt(blocks(u0[i]))], axis=1))
        q[i] = rt_ref[rows, lanes] + res[0:CHUNK, 0:GROUP]
        y0[i] = res[0:CHUNK, GROUP:2 * GROUP] + mrkv[i]
        a_mat[i] = eye * p_chunk[i[0]][:, lanes] + res[CHUNK:2 * CHUNK, 0:GROUP]
        g_mat[i] = res[CHUNK:2 * CHUNK, GROUP:2 * GROUP] + kv[i]

    hs = [h_ref[g] for g in range(N_GROUPS)]
    for c in range(n_chunks):
        for g in range(N_GROUPS):
            i = (c, g)
            res = _dot(jnp.concatenate([a_mat[i], q[i]], axis=0).astype(BF16), bd(hs[g]))
            hs[g] = res[0:CHUNK, :] + g_mat[i]
            y_ref[rows_of(c), lanes_of(g)] = res[CHUNK:2 * CHUNK, :] + y0[i]
    for g in range(N_GROUPS):
        h_ref[g] = hs[g]

    yh = y_ref[...]
    mu = segsum(yh) * (1.0 / HEAD)
    dlt = yh - mu
    var = segsum(dlt * dlt) * (1.0 / HEAD)
    yn = dlt * lax.rsqrt(var + GN_EPS) * vec(ROW_LNW) + vec(ROW_LNB)
    z = lerp(COL_Z, D_RWKV)
    y_rwkv = (yn + bon_ref[...]) * (z * _sigmoid(z))
    mix_ref[:, D_POOL:D_POOL + D_RWKV] = y_rwkv.astype(BF16)

    out = _dot(mix_ref[...], wout_ref[...])
    xo = x_ref[...] + gate * out
    ms2 = jnp.mean(xo * xo, axis=-1, keepdims=True)
    o_ref[...] = xo * lax.rsqrt(ms2 + NORM_EPS) * fg_ref[...]

    p_ref[8:PAD, :] = p_ref[TB + 8:TB + PAD, :]


def _constants():
    hb = np.arange(GROUP) // HEAD
    bdmask = (hb[:, None] == hb[None, :]).astype(np.float32)
    ti = np.arange(CHUNK)[:, None]
    si = (np.arange(2 * GROUP) % HEAD)[None, :]
    trimask = np.concatenate([(ti > si), (ti >= si)], axis=0).astype(np.float32)
    sg = (np.arange(GROUP) % HEAD)[None, :]
    eye = (ti == sg).astype(np.float32)
    m16 = (ti // 16) == (sg // 16)
    m32 = (ti // 32) == (sg // 32)
    tmask = np.stack([m16, m32 & ~m16, ~m32], axis=0).astype(np.float32)
    rr = np.arange(TB)
    same = (rr[:, None] // CHUNK) == (rr[None, :] // CHUNK)
    tril = same & (rr[None, :] <= rr[:, None])
    cmat = tril.astype(np.float32)
    return bdmask, trimask, eye, tmask, cmat


def kernel(x, c, w_ada, b_ada, norm_g, w_in, pool_w, pool_scale, mu_shift, w0, w_up, a0, a_up,
           k_k, k_a, r_k, ln_w, ln_b, w_out, final_g):
    B, T, _ = x.shape
    assert w_ada.shape[0] == 1 and T % TB == 0
    l = 0

    c_pad = jnp.zeros((8, D_MODEL), F32).at[0:B].set(c)
    mod = _ada_mod(c_pad, w_ada[l], b_ada[l][None, :])[0:B].reshape(B, 3, D_MODEL)

    s0 = 2 * D_POOL
    spans = [(0, s0), (s0, s0 + 512), (s0 + 576, s0 + 1088), (s0 + 1088, s0 + 1600),
             (s0 + 1664, s0 + 2176), (s0 + 512, s0 + 576), (s0 + 1600, s0 + 1664)]
    w_in_p = jnp.concatenate([w_in[l][:, a:b] for a, b in spans], axis=1).astype(BF16)
    mu_p = jnp.concatenate([mu_shift[l][a - s0:b - s0] for a, b in spans[1:]], axis=0)[None, :]
    lora = jnp.zeros((2 * LORA, 2 * D_RWKV), F32)
    lora = lora.at[0:LORA, 0:D_RWKV].set(w_up[l]).at[LORA:, D_RWKV:].set(a_up[l])
    lora_b = lora.astype(BF16)
    lora2 = jnp.concatenate([lora_b, lora_b], axis=0)
    vec = jnp.stack([w0[l], a0[l], k_k[l], k_a[l], r_k[l], ln_w[l], ln_b[l], pool_scale[l]], axis=0)
    pw = jnp.zeros((2, 256, 256), F32)
    for g in range(4):
        o = (g % 2) * POOL_GROUP
        pw = pw.at[g // 2, o:o + POOL_GROUP, o:o + POOL_GROUP].set(pool_w[l, g])
    pw = pw.astype(BF16)
    w_out_b = w_out[l].astype(BF16)

    bdmask, trimask, eye, tmask, cmat = _constants()
    bdmask_b = jnp.asarray(bdmask, BF16)
    segones = bdmask_b
    trimask = jnp.asarray(trimask)
    eye = jnp.asarray(eye)
    tmask = jnp.asarray(tmask)
    cmat = jnp.asarray(cmat, BF16)

    def full(a):
        nd = a.ndim
        return pl.BlockSpec(a.shape, lambda b, t, _nd=nd: (0,) * _nd)

    consts = (norm_g[l][None, :], w_in_p, mu_p, lora2, vec, pw, w_out_b,
              final_g[None, :], bdmask_b, trimask, eye, tmask, cmat, segones)
    in_specs = [
        pl.BlockSpec((None, TB, D_MODEL), lambda b, t: (b, t, 0)),
        pl.BlockSpec((None, 3, D_MODEL), lambda b, t: (b, 0, 0)),
    ] + [full(a) for a in consts]

    blk = lambda: pltpu.VMEM((TB, D_RWKV), F32)
    scratch = [
        pltpu.VMEM((TB + PAD, N_IN), F32),
        pltpu.VMEM((TB + PAD, D_POOL), F32),
        pltpu.VMEM((TB + PAD, D_POOL), F32),
        blk(), blk(), blk(), blk(), blk(), blk(),
        blk(), blk(), blk(),
        pltpu.VMEM((TB, D_MODEL), BF16),
        pltpu.VMEM((N_GROUPS, HEAD, GROUP), F32),
    ]
    return pl.pallas_call(
        _fused_kernel,
        grid=(B, T // TB),
        in_specs=in_specs,
        out_specs=pl.BlockSpec((None, TB, D_MODEL), lambda b, t: (b, t, 0)),
        out_shape=jax.ShapeDtypeStruct((B, T, D_MODEL), x.dtype),
        scratch_shapes=scratch,
        compiler_params=pltpu.CompilerParams(
            dimension_semantics=("arbitrary", "arbitrary"),
            vmem_limit_bytes=VMEM_LIMIT_BYTES),
        name="hybrid_block",
    )(x, mod, *consts)
```

```python
import functools

import numpy as np
import jax
import jax.numpy as jnp
from jax import lax
from jax.experimental import pallas as pl
from jax.experimental.pallas import tpu as pltpu

F32 = jnp.float32
BF16 = jnp.bfloat16

D_MODEL = 1024
D_POOL = 512
D_RWKV = 512
HEAD = 64
LORA = 64
POOL_WINDOWS = (2, 4, 8, 16)
POOL_GROUP = 128
NORM_EPS = 1e-6
GN_EPS = 64e-5
L2_EPS = 1e-12

COL_U = 0
COL_PZ = 512
COL_R = 1024
COL_K = 1536
COL_V = 2048
COL_Z = 2560
COL_LO = 3072
N_IN = 3200
SEG0 = 1024

CHUNK = 64
GROUP = 256
HEADS_PER_GROUP = GROUP // HEAD
N_GROUPS = D_RWKV // GROUP
TB = 512
CUM_ROWS = 256
PAD = 24

ROW_W0, ROW_A0, ROW_KK, ROW_KA, ROW_RK, ROW_LNW, ROW_LNB, ROW_PSCALE = range(8)

VMEM_LIMIT_BYTES = 56 * 1024 * 1024


def _split_hi_lo(x):
    hi = x.astype(BF16)
    lo = (x - hi.astype(F32)).astype(BF16)
    return hi, lo


def _dot(a, b):
    return jnp.dot(a, b, preferred_element_type=F32)


def _sigmoid(x):
    return 0.5 * jnp.tanh(0.5 * x) + 0.5


def _ada_kernel(c_ref, w_ref, b_ref, o_ref):
    c_hi, c_lo = _split_hi_lo(c_ref[...])
    w_hi, w_lo = _split_hi_lo(w_ref[...])
    acc = _dot(c_hi, w_hi) + _dot(c_lo, w_hi) + _dot(c_hi, w_lo)
    o_ref[...] = acc + b_ref[...]


def _ada_mod(c_pad, w_ada, b_ada):
    n = w_ada.shape[1]
    bn = 768
    return pl.pallas_call(
        _ada_kernel,
        grid=(n // bn,),
        in_specs=[
            pl.BlockSpec((c_pad.shape[0], D_MODEL), lambda j: (0, 0)),
            pl.BlockSpec((D_MODEL, bn), lambda j: (0, j)),
            pl.BlockSpec((1, bn), lambda j: (0, j)),
        ],
        out_specs=pl.BlockSpec((c_pad.shape[0], bn), lambda j: (0, j)),
        out_shape=jax.ShapeDtypeStruct((c_pad.shape[0], n), F32),
        name="adaln_mod",
    )(c_pad, w_ada, b_ada)


def _head_blocks(x, bdmask):
    xb = x.astype(BF16)
    return [xb * bdmask[h * HEAD:(h + 1) * HEAD, :] for h in range(HEADS_PER_GROUP)]


def _block_diag(x, bdmask):
    return jnp.concatenate(_head_blocks(x, bdmask), axis=0)


T_HEAD_ORDER = (0, 2, 1, 3)


def _heads_to_rows_t(x):
    xt = jnp.concatenate([x[:, 0:2 * HEAD], x[:, 2 * HEAD:4 * HEAD]], axis=0).T
    return jnp.concatenate([xt[0:HEAD, :], xt[HEAD:2 * HEAD, :]], axis=1)


def _fused_kernel(x_ref, mod_ref, ng_ref, win_ref, mu_ref, lora_ref, vec_ref,
                  pw_ref, wout_ref, fg_ref, bdmask_ref, trimask_ref, eye_ref, tmask_ref, cmat_ref,
                  segones_ref, o_ref,
                  p_ref, sa_ref, sb_ref, at_ref, rt_ref, bt_ref, kt_ref, bdc_ref, kdc_ref,
                  v_ref, y_ref, bon_ref, mix_ref, h_ref):
    t = pl.program_id(1)

    @pl.when(t == 0)
    def _():
        p_ref[0:PAD, :] = jnp.zeros((PAD, N_IN), F32)
        sa_ref[0:8, :] = jnp.zeros((8, D_POOL), F32)
        sb_ref[0:8, :] = jnp.zeros((8, D_POOL), F32)
        h_ref[...] = jnp.zeros(h_ref.shape, F32)

    shift = mod_ref[0:1, :]
    scale = mod_ref[1:2, :]
    gate = mod_ref[2:3, :]

    xb = x_ref[...]
    ms = jnp.mean(xb * xb, axis=-1, keepdims=True)
    hmod = (xb * lax.rsqrt(ms + NORM_EPS) * ng_ref[...]) * (1.0 + scale) + shift
    p_ref[PAD:PAD + TB, :] = _dot(hmod.astype(BF16), win_ref[...])

    def vec(row):
        return vec_ref[row:row + 1, :]

    n_ext = TB + PAD
    sa_ref[8:n_ext, :] = p_ref[8:n_ext, 0:512] + p_ref[7:n_ext - 1, 0:512]
    sb_ref[8:n_ext, 128:512] = sa_ref[8:n_ext, 128:512] + sa_ref[6:n_ext - 2, 128:512]
    sa_ref[8:n_ext, 256:512] = sb_ref[8:n_ext, 256:512] + sb_ref[4:n_ext - 4, 256:512]
    sb_ref[8:n_ext, 384:512] = sa_ref[8:n_ext, 384:512] + sa_ref[0:n_ext - 8, 384:512]
    pos = t * TB + lax.broadcasted_iota(jnp.int32, (TB, 1), 0) + 1
    wsum_refs = (sa_ref, sb_ref, sa_ref, sb_ref)
    diffs = []
    for g, win in enumerate(POOL_WINDOWS):
        lanes = slice(g * POOL_GROUP, (g + 1) * POOL_GROUP)
        cnt = jnp.minimum(pos, win).astype(F32)
        mean = wsum_refs[g][PAD:PAD + TB, lanes] / cnt
        diffs.append(mean - p_ref[PAD:PAD + TB, lanes])
    pz = p_ref[PAD:PAD + TB, COL_PZ:COL_PZ + D_POOL]
    for half in range(2):
        d2 = jnp.concatenate(diffs[2 * half:2 * half + 2], axis=1).astype(BF16)
        yp = _dot(d2, pw_ref[half])
        lanes = slice(256 * half, 256 * half + 256)
        zz = pz[:, lanes]
        yp = yp * vec_ref[ROW_PSCALE:ROW_PSCALE + 1, lanes] * (zz * _sigmoid(zz))
        mix_ref[:, lanes] = yp.astype(BF16)

    def lerp(col, width):
        cur = p_ref[PAD:PAD + TB, col:col + width]
        prev = p_ref[PAD - 1:PAD + TB - 1, col:col + width]
        return cur + (prev - cur) * mu_ref[0:1, col - SEG0:col - SEG0 + width]

    lo = lerp(COL_LO, 2 * LORA)
    lane = lax.broadcasted_iota(jnp.int32, lo.shape, 1)
    lo = jnp.where(lane < LORA, jnp.tanh(lo), lo)
    lo_hi, lo_lo = _split_hi_lo(lo)
    lin = _dot(jnp.concatenate([lo_hi, lo_lo], axis=1), lora_ref[...])
    logw = -float(np.exp(-0.5)) * _sigmoid(vec(ROW_W0) + lin[:, 0:D_RWKV])
    a = _sigmoid(vec(ROW_A0) + lin[:, D_RWKV:2 * D_RWKV])

    def segsum(xv):
        parts = []
        for half in range(N_GROUPS):
            parts.append(_dot(xv[:, GROUP * half:GROUP * (half + 1)].astype(BF16), segones_ref[...]))
        return jnp.concatenate(parts, axis=1)

    lw_hi, lw_lo = _split_hi_lo(logw)
    cum = jnp.concatenate(
        [_dot(cmat_ref[...], lw_hi[rb:rb + CUM_ROWS, :]) + _dot(cmat_ref[...], lw_lo[rb:rb + CUM_ROWS, :])
         for rb in range(0, TB, CUM_ROWS)], axis=0)
    n_chunks = TB // CHUNK
    cum_end = [cum[(c + 1) * CHUNK - 1:(c + 1) * CHUNK, :] for c in range(n_chunks)]
    ctot = jnp.concatenate([jnp.broadcast_to(ce, (CHUNK, D_RWKV)) for ce in cum_end], axis=0)
    p_chunk = [jnp.exp(ce) for ce in cum_end]

    k = lerp(COL_K, D_RWKV)
    kkr = k * vec(ROW_KK)
    ssq = segsum(kkr * kkr)
    kk = kkr * lax.rsqrt(jnp.maximum(ssq, L2_EPS * L2_EPS))
    k2 = k * (1.0 + (a - 1.0) * vec(ROW_KA))
    bv = kk * a
    p_inv = jnp.exp(-cum)
    p_end = jnp.exp(ctot - cum)
    bt_ref[...] = bv * p_inv
    kt_ref[...] = k2 * p_inv
    bdc_ref[...] = bv * p_end
    kdc_ref[...] = k2 * p_end
    at_ref[...] = -kk * jnp.exp(cum - logw)
    r = lerp(COL_R, D_RWKV)
    rt_ref[...] = r * jnp.exp(cum)
    v = lerp(COL_V, D_RWKV)
    v_ref[...] = v
    bon_ref[...] = segsum(r * k2 * vec(ROW_RK)) * v

    bdmask = bdmask_ref[...]
    eye = eye_ref[...]

    inst = [(c, g) for c in range(n_chunks) for g in range(N_GROUPS)]

    def blocks(xv):
        return _head_blocks(xv, bdmask)

    def bd(xv):
        return _block_diag(xv, bdmask)

    def bdt(blks):
        return jnp.concatenate([blks[h] for h in T_HEAD_ORDER], axis=0)

    def rows_of(c):
        return slice(c * CHUNK, (c + 1) * CHUNK)

    def lanes_of(g):
        return slice(GROUP * g, GROUP * (g + 1))

    lab, lak, mrb, mrk = {}, {}, {}, {}
    for i in inst:
        rows, lanes = rows_of(i[0]), lanes_of(i[1])
        ar = jnp.concatenate([at_ref[rows, lanes], rt_ref[rows, lanes]], axis=0).astype(BF16)
        rhs1 = jnp.concatenate([bdt(blocks(_heads_to_rows_t(bt_ref[rows, lanes]))),
                                bdt(blocks(_heads_to_rows_t(kt_ref[rows, lanes])))], axis=1)
        s = _dot(ar, rhs1) * trimask_ref[...]
        lab[i] = s[0:CHUNK, 0:GROUP]
        lak[i] = s[0:CHUNK, GROUP:2 * GROUP]
        mrb[i] = s[CHUNK:2 * CHUNK, 0:GROUP]
        mrk[i] = s[CHUNK:2 * CHUNK, GROUP:2 * GROUP]

    tinv, lpow, lakv, mrkv, kv = {}, {}, {}, {}, {}
    for i in inst:
        rows, lanes = rows_of(i[0]), lanes_of(i[1])
        ldiag = lab[i] * tmask_ref[0]
        tinv[i] = eye + ldiag
        lpow[i] = _dot(ldiag.astype(BF16), bd(ldiag))
        kd_t = _heads_to_rows_t(kdc_ref[rows, lanes])
        xv = _dot(jnp.concatenate([lak[i], mrk[i], kd_t], axis=0).astype(BF16),
                  bdt(blocks(v_ref[rows, lanes])))
        lakv[i] = xv[0:CHUNK, :]
        mrkv[i] = xv[CHUNK:2 * CHUNK, :]
        kv[i] = xv[2 * CHUNK:3 * CHUNK, :]
    for _ in range(2):
        for i in inst:
            res = _dot(jnp.concatenate([lpow[i], tinv[i]], axis=0).astype(BF16), bd(lpow[i]))
            lpow[i] = res[0:CHUNK, :]
            tinv[i] = tinv[i] + res[CHUNK:2 * CHUNK, :]
    for i in inst:
        tinv[i] = tinv[i] + _dot(tinv[i].astype(BF16), bd(lpow[i]))
    for level in (1, 2):
        xm = {}
        for i in inst:
            xm[i] = _dot((lab[i] * tmask_ref[level]).astype(BF16), bd(tinv[i]))
        for i in inst:
            tinv[i] = tinv[i] + _dot(tinv[i].astype(BF16), bd(xm[i]))

    w, u0 = {}, {}
    for i in inst:
        rows, lanes = rows_of(i[0]), lanes_of(i[1])
        wu = _dot(tinv[i].astype(BF16),
                  jnp.concatenate([bdt(blocks(at_ref[rows, lanes])), bdt(blocks(lakv[i]))], axis=1))
        w[i] = wu[:, 0:GROUP]
        u0[i] = wu[:, GROUP:2 * GROUP]
    q, y0, a_mat, g_mat = {}, {}, {}, {}
    for i in inst:
        rows, lanes = rows_of(i[0]), lanes_of(i[1])
        bd_t = _heads_to_rows_t(bdc_ref[rows, lanes])
        res = _dot(jnp.concatenate([mrb[i], bd_t], axis=0).astype(BF16),
                   jnp.concatenate([bdt(blocks(w[i])), bdt(blocks(u0[i]))], axis=1))
        q[i] = rt_ref[rows, lanes] + res[0:CHUNK, 0:GROUP]
        y0[i] = res[0:CHUNK, GROUP:2 * GROUP] + mrkv[i]
        a_mat[i] = eye * p_chunk[i[0]][:, lanes] + res[CHUNK:2 * CHUNK, 0:GROUP]
        g_mat[i] = res[CHUNK:2 * CHUNK, GROUP:2 * GROUP] + kv[i]

    hs = [h_ref[g] for g in range(N_GROUPS)]
    for c in range(n_chunks):
        for g in range(N_GROUPS):
            i = (c, g)
            res = _dot(jnp.concatenate([a_mat[i], q[i]], axis=0).astype(BF16), bd(hs[g]))
            hs[g] = res[0:CHUNK, :] + g_mat[i]
            y_ref[rows_of(c), lanes_of(g)] = res[CHUNK:2 * CHUNK, :] + y0[i]
    for g in range(N_GROUPS):
        h_ref[g] = hs[g]

    yh = y_ref[...]
    mu = segsum(yh) * (1.0 / HEAD)
    dlt = yh - mu
    var = segsum(dlt * dlt) * (1.0 / HEAD)
    yn = dlt * lax.rsqrt(var + GN_EPS) * vec(ROW_LNW) + vec(ROW_LNB)
    z = lerp(COL_Z, D_RWKV)
    y_rwkv = (yn + bon_ref[...]) * (z * _sigmoid(z))
    mix_ref[:, D_POOL:D_POOL + D_RWKV] = y_rwkv.astype(BF16)

    out = _dot(mix_ref[...], wout_ref[...])
    xo = x_ref[...] + gate * out
    ms2 = jnp.mean(xo * xo, axis=-1, keepdims=True)
    o_ref[...] = xo * lax.rsqrt(ms2 + NORM_EPS) * fg_ref[...]

    p_ref[8:PAD, :] = p_ref[TB + 8:TB + PAD, :]


def _constants():
    hb = np.arange(GROUP) // HEAD
    bdmask = (hb[:, None] == hb[None, :]).astype(np.float32)
    ti = np.arange(CHUNK)[:, None]
    si = (np.arange(2 * GROUP) % HEAD)[None, :]
    trimask = np.concatenate([(ti > si), (ti >= si)], axis=0).astype(np.float32)
    sg = (np.arange(GROUP) % HEAD)[None, :]
    eye = (ti == sg).astype(np.float32)
    m16 = (ti // 16) == (sg // 16)
    m32 = (ti // 32) == (sg // 32)
    tmask = np.stack([m16, m32 & ~m16, ~m32], axis=0).astype(np.float32)
    rr = np.arange(CUM_ROWS)
    same = (rr[:, None] // CHUNK) == (rr[None, :] // CHUNK)
    tril = same & (rr[None, :] <= rr[:, None])
    cmat = tril.astype(np.float32)
    return bdmask, trimask, eye, tmask, cmat


def kernel(x, c, w_ada, b_ada, norm_g, w_in, pool_w, pool_scale, mu_shift, w0, w_up, a0, a_up,
           k_k, k_a, r_k, ln_w, ln_b, w_out, final_g):
    B, T, _ = x.shape
    assert w_ada.shape[0] == 1 and T % TB == 0
    l = 0

    c_pad = jnp.zeros((8, D_MODEL), F32).at[0:B].set(c)
    mod = _ada_mod(c_pad, w_ada[l], b_ada[l][None, :])[0:B].reshape(B, 3, D_MODEL)

    s0 = 2 * D_POOL
    spans = [(0, s0), (s0, s0 + 512), (s0 + 576, s0 + 1088), (s0 + 1088, s0 + 1600),
             (s0 + 1664, s0 + 2176), (s0 + 512, s0 + 576), (s0 + 1600, s0 + 1664)]
    w_in_p = jnp.concatenate([w_in[l][:, a:b] for a, b in spans], axis=1).astype(BF16)
    mu_p = jnp.concatenate([mu_shift[l][a - s0:b - s0] for a, b in spans[1:]], axis=0)[None, :]
    lora = jnp.zeros((2 * LORA, 2 * D_RWKV), F32)
    lora = lora.at[0:LORA, 0:D_RWKV].set(w_up[l]).at[LORA:, D_RWKV:].set(a_up[l])
    lora_b = lora.astype(BF16)
    lora2 = jnp.concatenate([lora_b, lora_b], axis=0)
    vec = jnp.stack([w0[l], a0[l], k_k[l], k_a[l], r_k[l], ln_w[l], ln_b[l], pool_scale[l]], axis=0)
    pw = jnp.zeros((2, 256, 256), F32)
    for g in range(4):
        o = (g % 2) * POOL_GROUP
        pw = pw.at[g // 2, o:o + POOL_GROUP, o:o + POOL_GROUP].set(pool_w[l, g])
    pw = pw.astype(BF16)
    w_out_b = w_out[l].astype(BF16)

    bdmask, trimask, eye, tmask, cmat = _constants()
    bdmask_b = jnp.asarray(bdmask, BF16)
    segones = bdmask_b
    trimask = jnp.asarray(trimask)
    eye = jnp.asarray(eye)
    tmask = jnp.asarray(tmask)
    cmat = jnp.asarray(cmat, BF16)

    def full(a):
        nd = a.ndim
        return pl.BlockSpec(a.shape, lambda b, t, _nd=nd: (0,) * _nd)

    consts = (norm_g[l][None, :], w_in_p, mu_p, lora2, vec, pw, w_out_b,
              final_g[None, :], bdmask_b, trimask, eye, tmask, cmat, segones)
    in_specs = [
        pl.BlockSpec((None, TB, D_MODEL), lambda b, t: (b, t, 0)),
        pl.BlockSpec((None, 3, D_MODEL), lambda b, t: (b, 0, 0)),
    ] + [full(a) for a in consts]

    blk = lambda: pltpu.VMEM((TB, D_RWKV), F32)
    scratch = [
        pltpu.VMEM((TB + PAD, N_IN), F32),
        pltpu.VMEM((TB + PAD, D_POOL), F32),
        pltpu.VMEM((TB + PAD, D_POOL), F32),
        blk(), blk(), blk(), blk(), blk(), blk(),
        blk(), blk(), blk(),
        pltpu.VMEM((TB, D_MODEL), BF16),
        pltpu.VMEM((N_GROUPS, HEAD, GROUP), F32),
    ]
    return pl.pallas_call(
        _fused_kernel,
        grid=(B, T // TB),
        in_specs=in_specs,
        out_specs=pl.BlockSpec((None, TB, D_MODEL), lambda b, t: (b, t, 0)),
        out_shape=jax.ShapeDtypeStruct((B, T, D_MODEL), x.dtype),
        scratch_shapes=scratch,
        compiler_params=pltpu.CompilerParams(
            dimension_semantics=("arbitrary", "arbitrary"),
            vmem_limit_bytes=VMEM_LIMIT_BYTES),
        name="hybrid_block",
    )(x, mod, *consts)
```

```python
import functools

import numpy as np
import jax
import jax.numpy as jnp
from jax import lax
from jax.experimental import pallas as pl
from jax.experimental.pallas import tpu as pltpu

F32 = jnp.float32
BF16 = jnp.bfloat16

D_MODEL = 1024
D_POOL = 512
D_RWKV = 512
HEAD = 64
LORA = 64
POOL_WINDOWS = (2, 4, 8, 16)
POOL_GROUP = 128
NORM_EPS = 1e-6
GN_EPS = 64e-5
L2_EPS = 1e-12

COL_U = 0
COL_PZ = 512
COL_R = 1024
COL_K = 1536
COL_V = 2048
COL_Z = 2560
COL_LO = 3072
N_IN = 3200
SEG0 = 1024

CHUNK = 64
GROUP = 256
HEADS_PER_GROUP = GROUP // HEAD
N_GROUPS = D_RWKV // GROUP
TB = 512
CUM_ROWS = 256
PAD = 24

ROW_W0, ROW_A0, ROW_KK, ROW_KA, ROW_RK, ROW_LNW, ROW_LNB, ROW_PSCALE = range(8)

VMEM_LIMIT_BYTES = 56 * 1024 * 1024


def _split_hi_lo(x):
    hi = x.astype(BF16)
    lo = (x - hi.astype(F32)).astype(BF16)
    return hi, lo


def _dot(a, b):
    return jnp.dot(a, b, preferred_element_type=F32)


def _sigmoid(x):
    return 0.5 * jnp.tanh(0.5 * x) + 0.5


ADA_BN = 512
LANES = 128


def _ada_kernel(cb_ref, w_ref, b_ref, o_ref):
    w = w_ref[...]
    rows = []
    for b in range(cb_ref.shape[0]):
        cb = cb_ref[b]
        cols = [jnp.sum(w[:, j:j + LANES] * cb, axis=0, keepdims=True) for j in range(0, ADA_BN, LANES)]
        rows.append(jnp.concatenate(cols, axis=1))
    o_ref[...] = jnp.concatenate(rows, axis=0) + b_ref[...]


def _ada_mod(c, w_ada, b_ada, l):
    nb, n = c.shape[0], w_ada.shape[2]
    cb = jnp.broadcast_to(c[:, :, None], (nb, D_MODEL, LANES))
    return pl.pallas_call(
        _ada_kernel,
        grid=(n // ADA_BN,),
        in_specs=[
            pl.BlockSpec((nb, D_MODEL, LANES), lambda j: (0, 0, 0)),
            pl.BlockSpec((None, D_MODEL, ADA_BN), lambda j: (l, 0, j)),
            pl.BlockSpec((None, 1, ADA_BN), lambda j: (l, 0, j)),
        ],
        out_specs=pl.BlockSpec((nb, ADA_BN), lambda j: (0, j)),
        out_shape=jax.ShapeDtypeStruct((nb, n), F32),
        name="adaln_mod",
    )(cb, w_ada, b_ada.reshape(b_ada.shape[0], 1, n))


def _head_blocks(x, bdmask):
    xb = x.astype(BF16)
    return [xb * bdmask[h * HEAD:(h + 1) * HEAD, :] for h in range(HEADS_PER_GROUP)]


def _block_diag(x, bdmask):
    return jnp.concatenate(_head_blocks(x, bdmask), axis=0)


T_HEAD_ORDER = (0, 2, 1, 3)


def _heads_to_rows_t(x):
    xt = jnp.concatenate([x[:, 0:2 * HEAD], x[:, 2 * HEAD:4 * HEAD]], axis=0).T
    return jnp.concatenate([xt[0:HEAD, :], xt[HEAD:2 * HEAD, :]], axis=1)


def _fused_kernel(x_ref, mod_ref, ng_ref, win_ref, mu_ref, wup_ref, aup_ref,
                  w0_ref, a0_ref, kk_ref, ka_ref, rk_ref, lnw_ref, lnb_ref, pscale_ref,
                  pw_ref, wout_ref, fg_ref, bdmask_ref, trimask_ref, eye_ref, tmask_ref, cmat_ref,
                  segones_ref, o_ref,
                  p_ref, sa_ref, sb_ref, at_ref, rt_ref, bt_ref, kt_ref, bdc_ref, kdc_ref,
                  v_ref, y_ref, bon_ref, mix_ref, h_ref):
    t = pl.program_id(1)

    @pl.when(t == 0)
    def _():
        p_ref[0:PAD, :] = jnp.zeros((PAD, N_IN), F32)
        sa_ref[0:8, :] = jnp.zeros((8, D_POOL), F32)
        sb_ref[0:8, :] = jnp.zeros((8, D_POOL), F32)
        h_ref[...] = jnp.zeros(h_ref.shape, F32)

    shift = mod_ref[0:1, :]
    scale = mod_ref[1:2, :]
    gate = mod_ref[2:3, :]

    xb = x_ref[...]
    ms = jnp.mean(xb * xb, axis=-1, keepdims=True)
    hmod = (xb * lax.rsqrt(ms + NORM_EPS) * ng_ref[...]) * (1.0 + scale) + shift
    p_ref[PAD:PAD + TB, :] = _dot(hmod.astype(BF16), win_ref[...])

    vec_refs = (w0_ref, a0_ref, kk_ref, ka_ref, rk_ref, lnw_ref, lnb_ref, pscale_ref)

    def vec(row):
        return vec_refs[row][...]

    def diag2(m0, m1):
        z = jnp.zeros_like(m0)
        return jnp.concatenate([jnp.concatenate([m0, z], axis=1),
                                jnp.concatenate([z, m1], axis=1)], axis=0)

    n_ext = TB + PAD
    sa_ref[8:n_ext, :] = p_ref[8:n_ext, 0:512] + p_ref[7:n_ext - 1, 0:512]
    sb_ref[8:n_ext, 128:512] = sa_ref[8:n_ext, 128:512] + sa_ref[6:n_ext - 2, 128:512]
    sa_ref[8:n_ext, 256:512] = sb_ref[8:n_ext, 256:512] + sb_ref[4:n_ext - 4, 256:512]
    sb_ref[8:n_ext, 384:512] = sa_ref[8:n_ext, 384:512] + sa_ref[0:n_ext - 8, 384:512]
    pos = t * TB + lax.broadcasted_iota(jnp.int32, (TB, 1), 0) + 1
    wsum_refs = (sa_ref, sb_ref, sa_ref, sb_ref)
    diffs = []
    for g, win in enumerate(POOL_WINDOWS):
        lanes = slice(g * POOL_GROUP, (g + 1) * POOL_GROUP)
        cnt = jnp.minimum(pos, win).astype(F32)
        mean = wsum_refs[g][PAD:PAD + TB, lanes] / cnt
        diffs.append(mean - p_ref[PAD:PAD + TB, lanes])
    pz = p_ref[PAD:PAD + TB, COL_PZ:COL_PZ + D_POOL]
    for half in range(2):
        d2 = jnp.concatenate(diffs[2 * half:2 * half + 2], axis=1).astype(BF16)
        yp = _dot(d2, diag2(pw_ref[2 * half].astype(BF16), pw_ref[2 * half + 1].astype(BF16)))
        lanes = slice(256 * half, 256 * half + 256)
        zz = pz[:, lanes]
        yp = yp * pscale_ref[:, lanes] * (zz * _sigmoid(zz))
        mix_ref[:, lanes] = yp.astype(BF16)

    def lerp(col, width):
        cur = p_ref[PAD:PAD + TB, col:col + width]
        prev = p_ref[PAD - 1:PAD + TB - 1, col:col + width]
        return cur + (prev - cur) * mu_ref[0:1, col - SEG0:col - SEG0 + width]

    lo = lerp(COL_LO, 2 * LORA)
    lane = lax.broadcasted_iota(jnp.int32, lo.shape, 1)
    lo = jnp.where(lane < LORA, jnp.tanh(lo), lo)
    lo_hi, lo_lo = _split_hi_lo(lo)
    lora = diag2(wup_ref[...].astype(BF16), aup_ref[...].astype(BF16))
    lin = _dot(jnp.concatenate([lo_hi, lo_lo], axis=1), jnp.concatenate([lora, lora], axis=0))
    logw = -float(np.exp(-0.5)) * _sigmoid(vec(ROW_W0) + lin[:, 0:D_RWKV])
    a = _sigmoid(vec(ROW_A0) + lin[:, D_RWKV:2 * D_RWKV])

    def segsum(xv):
        parts = []
        for half in range(N_GROUPS):
            parts.append(_dot(xv[:, GROUP * half:GROUP * (half + 1)].astype(BF16), segones_ref[...]))
        return jnp.concatenate(parts, axis=1)

    lw_hi, lw_lo = _split_hi_lo(logw)
    cum = jnp.concatenate(
        [_dot(cmat_ref[...], lw_hi[rb:rb + CUM_ROWS, :]) + _dot(cmat_ref[...], lw_lo[rb:rb + CUM_ROWS, :])
         for rb in range(0, TB, CUM_ROWS)], axis=0)
    n_chunks = TB // CHUNK
    cum_end = [cum[(c + 1) * CHUNK - 1:(c + 1) * CHUNK, :] for c in range(n_chunks)]
    ctot = jnp.concatenate([jnp.broadcast_to(ce, (CHUNK, D_RWKV)) for ce in cum_end], axis=0)
    p_chunk = [jnp.exp(ce) for ce in cum_end]

    k = lerp(COL_K, D_RWKV)
    kkr = k * vec(ROW_KK)
    ssq = segsum(kkr * kkr)
    kk = kkr * lax.rsqrt(jnp.maximum(ssq, L2_EPS * L2_EPS))
    k2 = k * (1.0 + (a - 1.0) * vec(ROW_KA))
    bv = kk * a
    p_inv = jnp.exp(-cum)
    p_end = jnp.exp(ctot - cum)
    bt_ref[...] = bv * p_inv
    kt_ref[...] = k2 * p_inv
    bdc_ref[...] = bv * p_end
    kdc_ref[...] = k2 * p_end
    at_ref[...] = -kk * jnp.exp(cum - logw)
    r = lerp(COL_R, D_RWKV)
    rt_ref[...] = r * jnp.exp(cum)
    v = lerp(COL_V, D_RWKV)
    v_ref[...] = v
    bon_ref[...] = segsum(r * k2 * vec(ROW_RK)) * v

    bdmask = bdmask_ref[...]
    eye = eye_ref[...]

    inst = [(c, g) for c in range(n_chunks) for g in range(N_GROUPS)]

    def blocks(xv):
        return _head_blocks(xv, bdmask)

    def bd(xv):
        return _block_diag(xv, bdmask)

    def bdt(blks):
        return jnp.concatenate([blks[h] for h in T_HEAD_ORDER], axis=0)

    def rows_of(c):
        return slice(c * CHUNK, (c + 1) * CHUNK)

    def lanes_of(g):
        return slice(GROUP * g, GROUP * (g + 1))

    lab, lak, mrb, mrk = {}, {}, {}, {}
    for i in inst:
        rows, lanes = rows_of(i[0]), lanes_of(i[1])
        ar = jnp.concatenate([at_ref[rows, lanes], rt_ref[rows, lanes]], axis=0).astype(BF16)
        rhs1 = jnp.concatenate([bdt(blocks(_heads_to_rows_t(bt_ref[rows, lanes]))),
                                bdt(blocks(_heads_to_rows_t(kt_ref[rows, lanes])))], axis=1)
        s = _dot(ar, rhs1) * trimask_ref[...]
        lab[i] = s[0:CHUNK, 0:GROUP]
        lak[i] = s[0:CHUNK, GROUP:2 * GROUP]
        mrb[i] = s[CHUNK:2 * CHUNK, 0:GROUP]
        mrk[i] = s[CHUNK:2 * CHUNK, GROUP:2 * GROUP]

    tinv, lpow, lakv, mrkv, kv = {}, {}, {}, {}, {}
    for i in inst:
        rows, lanes = rows_of(i[0]), lanes_of(i[1])
        ldiag = lab[i] * tmask_ref[0]
        tinv[i] = eye + ldiag
        lpow[i] = _dot(ldiag.astype(BF16), bd(ldiag))
        kd_t = _heads_to_rows_t(kdc_ref[rows, lanes])
        xv = _dot(jnp.concatenate([lak[i], mrk[i], kd_t], axis=0).astype(BF16),
                  bdt(blocks(v_ref[rows, lanes])))
        lakv[i] = xv[0:CHUNK, :]
        mrkv[i] = xv[CHUNK:2 * CHUNK, :]
        kv[i] = xv[2 * CHUNK:3 * CHUNK, :]
    for _ in range(2):
        for i in inst:
            res = _dot(jnp.concatenate([lpow[i], tinv[i]], axis=0).astype(BF16), bd(lpow[i]))
            lpow[i] = res[0:CHUNK, :]
            tinv[i] = tinv[i] + res[CHUNK:2 * CHUNK, :]
    for i in inst:
        tinv[i] = tinv[i] + _dot(tinv[i].astype(BF16), bd(lpow[i]))
    for level in (1, 2):
        xm = {}
        for i in inst:
            xm[i] = _dot((lab[i] * tmask_ref[level]).astype(BF16), bd(tinv[i]))
        for i in inst:
            tinv[i] = tinv[i] + _dot(tinv[i].astype(BF16), bd(xm[i]))

    w, u0 = {}, {}
    for i in inst:
        rows, lanes = rows_of(i[0]), lanes_of(i[1])
        wu = _dot(tinv[i].astype(BF16),
                  jnp.concatenate([bdt(blocks(at_ref[rows, lanes])), bdt(blocks(lakv[i]))], axis=1))
        w[i] = wu[:, 0:GROUP]
        u0[i] = wu[:, GROUP:2 * GROUP]
    q, y0, a_mat, g_mat = {}, {}, {}, {}
    for i in inst:
        rows, lanes = rows_of(i[0]), lanes_of(i[1])
        bd_t = _heads_to_rows_t(bdc_ref[rows, lanes])
        res = _dot(jnp.concatenate([mrb[i], bd_t], axis=0).astype(BF16),
                   jnp.concatenate([bdt(blocks(w[i])), bdt(blocks(u0[i]))], axis=1))
        q[i] = rt_ref[rows, lanes] + res[0:CHUNK, 0:GROUP]
        y0[i] = res[0:CHUNK, GROUP:2 * GROUP] + mrkv[i]
        a_mat[i] = eye * p_chunk[i[0]][:, lanes] + res[CHUNK:2 * CHUNK, 0:GROUP]
        g_mat[i] = res[CHUNK:2 * CHUNK, GROUP:2 * GROUP] + kv[i]

    hs = [h_ref[g] for g in range(N_GROUPS)]
    for c in range(n_chunks):
        for g in range(N_GROUPS):
            i = (c, g)
            res = _dot(jnp.concatenate([a_mat[i], q[i]], axis=0).astype(BF16), bd(hs[g]))
            hs[g] = res[0:CHUNK, :] + g_mat[i]
            y_ref[rows_of(c), lanes_of(g)] = res[CHUNK:2 * CHUNK, :] + y0[i]
    for g in range(N_GROUPS):
        h_ref[g] = hs[g]

    yh = y_ref[...]
    mu = segsum(yh) * (1.0 / HEAD)
    dlt = yh - mu
    var = segsum(dlt * dlt) * (1.0 / HEAD)
    yn = dlt * lax.rsqrt(var + GN_EPS) * vec(ROW_LNW) + vec(ROW_LNB)
    z = lerp(COL_Z, D_RWKV)
    y_rwkv = (yn + bon_ref[...]) * (z * _sigmoid(z))
    mix_ref[:, D_POOL:D_POOL + D_RWKV] = y_rwkv.astype(BF16)

    out = _dot(mix_ref[...], wout_ref[...])
    xo = x_ref[...] + gate * out
    ms2 = jnp.mean(xo * xo, axis=-1, keepdims=True)
    o_ref[...] = xo * lax.rsqrt(ms2 + NORM_EPS) * fg_ref[...]

    p_ref[8:PAD, :] = p_ref[TB + 8:TB + PAD, :]


def _constants():
    hb = np.arange(GROUP) // HEAD
    bdmask = (hb[:, None] == hb[None, :]).astype(np.float32)
    ti = np.arange(CHUNK)[:, None]
    si = (np.arange(2 * GROUP) % HEAD)[None, :]
    trimask = np.concatenate([(ti > si), (ti >= si)], axis=0).astype(np.float32)
    sg = (np.arange(GROUP) % HEAD)[None, :]
    eye = (ti == sg).astype(np.float32)
    m16 = (ti // 16) == (sg // 16)
    m32 = (ti // 32) == (sg // 32)
    tmask = np.stack([m16, m32 & ~m16, ~m32], axis=0).astype(np.float32)
    rr = np.arange(CUM_ROWS)
    same = (rr[:, None] // CHUNK) == (rr[None, :] // CHUNK)
    tril = same & (rr[None, :] <= rr[:, None])
    cmat = tril.astype(np.float32)
    return bdmask, trimask, eye, tmask, cmat


def kernel(x, c, w_ada, b_ada, norm_g, w_in, pool_w, pool_scale, mu_shift, w0, w_up, a0, a_up,
           k_k, k_a, r_k, ln_w, ln_b, w_out, final_g):
    B, T, _ = x.shape
    assert w_ada.shape[0] == 1 and T % TB == 0
    l = 0

    mod = _ada_mod(c, w_ada, b_ada, l).reshape(B, 3, D_MODEL)

    s0 = 2 * D_POOL
    spans = [(0, s0), (s0, s0 + 512), (s0 + 576, s0 + 1088), (s0 + 1088, s0 + 1600),
             (s0 + 1664, s0 + 2176), (s0 + 512, s0 + 576), (s0 + 1600, s0 + 1664)]
    w_in_p = jnp.concatenate([w_in[l, :, a:b].astype(BF16) for a, b in spans], axis=1)
    mu_p = jnp.concatenate([mu_shift[l, a - s0:b - s0] for a, b in spans[1:]], axis=0)[None, :]

    bdmask, trimask, eye, tmask, cmat = _constants()
    bdmask_b = jnp.asarray(bdmask, BF16)
    segones = bdmask_b
    trimask = jnp.asarray(trimask)
    eye = jnp.asarray(eye)
    tmask = jnp.asarray(tmask)
    cmat = jnp.asarray(cmat, BF16)

    def full(a):
        nd = a.ndim
        return pl.BlockSpec(a.shape, lambda b, t, _nd=nd: (0,) * _nd)

    def layer(a):
        a = a.reshape(a.shape[0], 1, a.shape[1]) if a.ndim == 2 else a
        nd = a.ndim
        return a, pl.BlockSpec((None,) + a.shape[1:], lambda b, t, _nd=nd: (l,) + (0,) * (_nd - 1))

    operands = [
        (x, pl.BlockSpec((None, TB, D_MODEL), lambda b, t: (b, t, 0))),
        (mod, pl.BlockSpec((None, 3, D_MODEL), lambda b, t: (b, 0, 0))),
        layer(norm_g), (w_in_p, full(w_in_p)), (mu_p, full(mu_p)), layer(w_up), layer(a_up),
        layer(w0), layer(a0), layer(k_k), layer(k_a), layer(r_k), layer(ln_w), layer(ln_b),
        layer(pool_scale), layer(pool_w), layer(w_out.astype(BF16)),
        (final_g[None, :], full(final_g[None, :])),
    ] + [(a, full(a)) for a in (bdmask_b, trimask, eye, tmask, cmat, segones)]
    in_specs = [spec for _, spec in operands]

    blk = lambda: pltpu.VMEM((TB, D_RWKV), F32)
    scratch = [
        pltpu.VMEM((TB + PAD, N_IN), F32),
        pltpu.VMEM((TB + PAD, D_POOL), F32),
        pltpu.VMEM((TB + PAD, D_POOL), F32),
        blk(), blk(), blk(), blk(), blk(), blk(),
        blk(), blk(), blk(),
        pltpu.VMEM((TB, D_MODEL), BF16),
        pltpu.VMEM((N_GROUPS, HEAD, GROUP), F32),
    ]
    return pl.pallas_call(
        _fused_kernel,
        grid=(B, T // TB),
        in_specs=in_specs,
        out_specs=pl.BlockSpec((None, TB, D_MODEL), lambda b, t: (b, t, 0)),
        out_shape=jax.ShapeDtypeStruct((B, T, D_MODEL), x.dtype),
        scratch_shapes=scratch,
        compiler_params=pltpu.CompilerParams(
            dimension_semantics=("arbitrary", "arbitrary"),
            vmem_limit_bytes=VMEM_LIMIT_BYTES),
        name="hybrid_block",
    )(*[a for a, _ in operands])
```

```python
import functools

import numpy as np
import jax
import jax.numpy as jnp
from jax import lax
from jax.experimental import pallas as pl
from jax.experimental.pallas import tpu as pltpu

F32 = jnp.float32
BF16 = jnp.bfloat16

D_MODEL = 1024
D_POOL = 512
D_RWKV = 512
HEAD = 64
LORA = 64
POOL_WINDOWS = (2, 4, 8, 16)
POOL_GROUP = 128
NORM_EPS = 1e-6
GN_EPS = 64e-5
L2_EPS = 1e-12

COL_U = 0
COL_PZ = 512
COL_R = 1024
COL_K = 1536
COL_V = 2048
COL_Z = 2560
COL_LO = 3072
N_IN = 3200
SEG0 = 1024

CHUNK = 64
GROUP = 256
HEADS_PER_GROUP = GROUP // HEAD
N_GROUPS = D_RWKV // GROUP
TB = 512
CUM_ROWS = 256
STAGE_BATCHES = 1
PAD = 24

ROW_W0, ROW_A0, ROW_KK, ROW_KA, ROW_RK, ROW_LNW, ROW_LNB, ROW_PSCALE = range(8)

VMEM_LIMIT_BYTES = 56 * 1024 * 1024


def _split_hi_lo(x):
    hi = x.astype(BF16)
    lo = (x - hi.astype(F32)).astype(BF16)
    return hi, lo


def _dot(a, b):
    return jnp.dot(a, b, preferred_element_type=F32)


def _sigmoid(x):
    return 0.5 * jnp.tanh(0.5 * x) + 0.5


ADA_BN = 512
LANES = 128


def _ada_kernel(cb_ref, w_ref, b_ref, o_ref):
    w = w_ref[...]
    rows = []
    for b in range(cb_ref.shape[0]):
        cb = cb_ref[b]
        cols = [jnp.sum(w[:, j:j + LANES] * cb, axis=0, keepdims=True) for j in range(0, ADA_BN, LANES)]
        rows.append(jnp.concatenate(cols, axis=1))
    o_ref[...] = jnp.concatenate(rows, axis=0) + b_ref[...]


def _ada_mod(c, w_ada, b_ada, l):
    nb, n = c.shape[0], w_ada.shape[2]
    cb = jnp.broadcast_to(c[:, :, None], (nb, D_MODEL, LANES))
    return pl.pallas_call(
        _ada_kernel,
        grid=(n // ADA_BN,),
        in_specs=[
            pl.BlockSpec((nb, D_MODEL, LANES), lambda j: (0, 0, 0)),
            pl.BlockSpec((None, D_MODEL, ADA_BN), lambda j: (l, 0, j)),
            pl.BlockSpec((None, 1, ADA_BN), lambda j: (l, 0, j)),
        ],
        out_specs=pl.BlockSpec((nb, ADA_BN), lambda j: (0, j)),
        out_shape=jax.ShapeDtypeStruct((nb, n), F32),
        name="adaln_mod",
    )(cb, w_ada, b_ada.reshape(b_ada.shape[0], 1, n))


def _head_blocks(x, bdmask):
    xb = x.astype(BF16)
    return [xb * bdmask[h * HEAD:(h + 1) * HEAD, :] for h in range(HEADS_PER_GROUP)]


def _block_diag(x, bdmask):
    return jnp.concatenate(_head_blocks(x, bdmask), axis=0)


T_HEAD_ORDER = (0, 2, 1, 3)


def _heads_to_rows_t(x):
    xt = jnp.concatenate([x[:, 0:2 * HEAD], x[:, 2 * HEAD:4 * HEAD]], axis=0).T
    return jnp.concatenate([xt[0:HEAD, :], xt[HEAD:2 * HEAD, :]], axis=1)


def _fused_kernel(x_ref, mod_ref, ng_ref, win_ref, mu_ref, wup_ref, aup_ref,
                  w0_ref, a0_ref, kk_ref, ka_ref, rk_ref, lnw_ref, lnb_ref, pscale_ref,
                  pw_ref, wout_ref, fg_ref, bdmask_ref, trimask_ref, eye_ref, tmask_ref, cmat_ref,
                  segones_ref, o_ref,
                  p_ref, sa_ref, sb_ref, at_ref, rt_ref, bt_ref, kt_ref, bdc_ref, kdc_ref,
                  v_ref, bon_ref, mix_ref, hb_ref, h_ref):
    t = pl.program_id(1)

    @pl.when(t == 0)
    def _():
        p_ref[0:PAD, :] = jnp.zeros((PAD, N_IN), F32)
        sa_ref[0:8, :] = jnp.zeros((8, D_POOL), F32)
        sb_ref[0:8, :] = jnp.zeros((8, D_POOL), F32)
        h_ref[...] = jnp.zeros(h_ref.shape, F32)

    shift = mod_ref[0:1, :]
    scale = mod_ref[1:2, :]
    gate = mod_ref[2:3, :]

    xb = x_ref[...]
    ms = jnp.mean(xb * xb, axis=-1, keepdims=True)
    hmod = (xb * lax.rsqrt(ms + NORM_EPS) * ng_ref[...]) * (1.0 + scale) + shift
    hb_ref[...] = hmod.astype(BF16)

    def project(col, width):
        p_ref[PAD:PAD + TB, col:col + width] = _dot(hb_ref[...], win_ref[:, col:col + width])

    project(COL_LO, 2 * LORA)
    project(COL_R, D_RWKV)

    vec_refs = (w0_ref, a0_ref, kk_ref, ka_ref, rk_ref, lnw_ref, lnb_ref, pscale_ref)

    def vec(row):
        return vec_refs[row][...]

    def diag2(m0, m1):
        z = jnp.zeros_like(m0)
        return jnp.concatenate([jnp.concatenate([m0, z], axis=1),
                                jnp.concatenate([z, m1], axis=1)], axis=0)

    def lerp(col, width, row0=0, nrows=TB):
        cur = p_ref[PAD + row0:PAD + row0 + nrows, col:col + width]
        prev = p_ref[PAD + row0 - 1:PAD + row0 + nrows - 1, col:col + width]
        return cur + (prev - cur) * mu_ref[0:1, col - SEG0:col - SEG0 + width]

    lo = lerp(COL_LO, 2 * LORA)
    lane = lax.broadcasted_iota(jnp.int32, lo.shape, 1)
    lo = jnp.where(lane < LORA, jnp.tanh(lo), lo)
    lo_hi, lo_lo = _split_hi_lo(lo)
    lora = diag2(wup_ref[...].astype(BF16), aup_ref[...].astype(BF16))
    lin = _dot(jnp.concatenate([lo_hi, lo_lo], axis=1), jnp.concatenate([lora, lora], axis=0))
    logw = -float(np.exp(-0.5)) * _sigmoid(vec(ROW_W0) + lin[:, 0:D_RWKV])
    a = _sigmoid(vec(ROW_A0) + lin[:, D_RWKV:2 * D_RWKV])

    def segsum(xv):
        parts = []
        for half in range(N_GROUPS):
            parts.append(_dot(xv[:, GROUP * half:GROUP * (half + 1)].astype(BF16), segones_ref[...]))
        return jnp.concatenate(parts, axis=1)

    lw_hi, lw_lo = _split_hi_lo(logw)
    cum = jnp.concatenate(
        [_dot(cmat_ref[...], lw_hi[rb:rb + CUM_ROWS, :]) + _dot(cmat_ref[...], lw_lo[rb:rb + CUM_ROWS, :])
         for rb in range(0, TB, CUM_ROWS)], axis=0)
    n_chunks = TB // CHUNK
    cum_end = [cum[(c + 1) * CHUNK - 1:(c + 1) * CHUNK, :] for c in range(n_chunks)]
    ctot = jnp.concatenate([jnp.broadcast_to(ce, (CHUNK, D_RWKV)) for ce in cum_end], axis=0)
    p_chunk = [jnp.exp(ce) for ce in cum_end]

    project(COL_K, D_RWKV)
    r = lerp(COL_R, D_RWKV)
    rt_ref[...] = r * jnp.exp(cum)
    bon_ref[...] = r * vec(ROW_RK)

    project(COL_V, D_RWKV)
    k = lerp(COL_K, D_RWKV)
    kkr = k * vec(ROW_KK)
    ssq = segsum(kkr * kkr)
    kk = kkr * lax.rsqrt(jnp.maximum(ssq, L2_EPS * L2_EPS))
    k2 = k * (1.0 + (a - 1.0) * vec(ROW_KA))
    bv = kk * a
    p_inv = jnp.exp(-cum)
    p_end = jnp.exp(ctot - cum)
    bt_ref[...] = bv * p_inv
    kt_ref[...] = k2 * p_inv
    bdc_ref[...] = bv * p_end
    kdc_ref[...] = k2 * p_end
    at_ref[...] = -kk * jnp.exp(cum - logw)
    bonus_rk = segsum(bon_ref[...] * k2)

    project(COL_U, D_POOL)
    v = lerp(COL_V, D_RWKV)
    v_ref[...] = v
    bon_ref[...] = bonus_rk * v

    project(COL_PZ, D_POOL)
    n_ext = TB + PAD
    sa_ref[8:n_ext, :] = p_ref[8:n_ext, 0:512] + p_ref[7:n_ext - 1, 0:512]
    sb_ref[8:n_ext, 128:512] = sa_ref[8:n_ext, 128:512] + sa_ref[6:n_ext - 2, 128:512]
    sa_ref[8:n_ext, 256:512] = sb_ref[8:n_ext, 256:512] + sb_ref[4:n_ext - 4, 256:512]
    sb_ref[8:n_ext, 384:512] = sa_ref[8:n_ext, 384:512] + sa_ref[0:n_ext - 8, 384:512]
    pos = t * TB + lax.broadcasted_iota(jnp.int32, (TB, 1), 0) + 1
    wsum_refs = (sa_ref, sb_ref, sa_ref, sb_ref)
    diffs = []
    for g, win in enumerate(POOL_WINDOWS):
        lanes = slice(g * POOL_GROUP, (g + 1) * POOL_GROUP)
        cnt = jnp.minimum(pos, win).astype(F32)
        mean = wsum_refs[g][PAD:PAD + TB, lanes] / cnt
        diffs.append(mean - p_ref[PAD:PAD + TB, lanes])
    project(COL_Z, D_RWKV)
    pz = p_ref[PAD:PAD + TB, COL_PZ:COL_PZ + D_POOL]
    for half in range(2):
        d2 = jnp.concatenate(diffs[2 * half:2 * half + 2], axis=1).astype(BF16)
        yp = _dot(d2, diag2(pw_ref[2 * half].astype(BF16), pw_ref[2 * half + 1].astype(BF16)))
        lanes = slice(256 * half, 256 * half + 256)
        zz = pz[:, lanes]
        yp = yp * pscale_ref[:, lanes] * (zz * _sigmoid(zz))
        mix_ref[:, lanes] = yp.astype(BF16)

    bdmask = bdmask_ref[...]
    eye = eye_ref[...]

    def blocks(xv):
        return _head_blocks(xv, bdmask)

    def bd(xv):
        return _block_diag(xv, bdmask)

    def bdt(blks):
        return jnp.concatenate([blks[h] for h in T_HEAD_ORDER], axis=0)

    def rows_of(c):
        return slice(c * CHUNK, (c + 1) * CHUNK)

    def lanes_of(g):
        return slice(GROUP * g, GROUP * (g + 1))

    lab, lak, mrb, mrk = {}, {}, {}, {}
    tinv, lpow, lakv, mrkv, kv, xm = {}, {}, {}, {}, {}, {}
    w, u0, q, y0, a_mat, g_mat = {}, {}, {}, {}, {}, {}

    def st_scores(i):
        rows, lanes = rows_of(i[0]), lanes_of(i[1])
        ar = jnp.concatenate([at_ref[rows, lanes], rt_ref[rows, lanes]], axis=0).astype(BF16)
        rhs1 = jnp.concatenate([bdt(blocks(_heads_to_rows_t(bt_ref[rows, lanes]))),
                                bdt(blocks(_heads_to_rows_t(kt_ref[rows, lanes])))], axis=1)
        s = _dot(ar, rhs1) * trimask_ref[...]
        lab[i] = s[0:CHUNK, 0:GROUP]
        lak[i] = s[0:CHUNK, GROUP:2 * GROUP]
        mrb[i] = s[CHUNK:2 * CHUNK, 0:GROUP]
        mrk[i] = s[CHUNK:2 * CHUNK, GROUP:2 * GROUP]

    def st_inv_start(i):
        rows, lanes = rows_of(i[0]), lanes_of(i[1])
        ldiag = lab[i] * tmask_ref[0]
        tinv[i] = eye + ldiag
        lpow[i] = _dot(ldiag.astype(BF16), bd(ldiag))
        kd_t = _heads_to_rows_t(kdc_ref[rows, lanes])
        xv = _dot(jnp.concatenate([lak[i], mrk[i], kd_t], axis=0).astype(BF16),
                  bdt(blocks(v_ref[rows, lanes])))
        lakv[i] = xv[0:CHUNK, :]
        mrkv[i] = xv[CHUNK:2 * CHUNK, :]
        kv[i] = xv[2 * CHUNK:3 * CHUNK, :]

    def st_neumann(i):
        res = _dot(jnp.concatenate([lpow[i], tinv[i]], axis=0).astype(BF16), bd(lpow[i]))
        lpow[i] = res[0:CHUNK, :]
        tinv[i] = tinv[i] + res[CHUNK:2 * CHUNK, :]

    def st_neumann_last(i):
        tinv[i] = tinv[i] + _dot(tinv[i].astype(BF16), bd(lpow[i]))

    def st_merge_x(level):
        def stage(i):
            xm[i] = _dot((lab[i] * tmask_ref[level]).astype(BF16), bd(tinv[i]))
        return stage

    def st_merge_t(i):
        tinv[i] = tinv[i] + _dot(tinv[i].astype(BF16), bd(xm[i]))

    def st_wu(i):
        rows, lanes = rows_of(i[0]), lanes_of(i[1])
        wu = _dot(tinv[i].astype(BF16),
                  jnp.concatenate([bdt(blocks(at_ref[rows, lanes])), bdt(blocks(lakv[i]))], axis=1))
        w[i] = wu[:, 0:GROUP]
        u0[i] = wu[:, GROUP:2 * GROUP]

    def st_transition(i):
        rows, lanes = rows_of(i[0]), lanes_of(i[1])
        bd_t = _heads_to_rows_t(bdc_ref[rows, lanes])
        res = _dot(jnp.concatenate([mrb[i], bd_t], axis=0).astype(BF16),
                   jnp.concatenate([bdt(blocks(w[i])), bdt(blocks(u0[i]))], axis=1))
        q[i] = rt_ref[rows, lanes] + res[0:CHUNK, 0:GROUP]
        y0[i] = res[0:CHUNK, GROUP:2 * GROUP] + mrkv[i]
        a_mat[i] = eye * p_chunk[i[0]][:, lanes] + res[CHUNK:2 * CHUNK, 0:GROUP]
        g_mat[i] = res[CHUNK:2 * CHUNK, GROUP:2 * GROUP] + kv[i]

    stages = [st_scores, st_inv_start, st_neumann, st_neumann, st_neumann_last,
              st_merge_x(1), st_merge_t, st_merge_x(2), st_merge_t, st_wu, st_transition]

    hs = [h_ref[g] for g in range(N_GROUPS)]

    y_chunks = {}

    def chain_step(c):
        ys = []
        for g in range(N_GROUPS):
            i = (c, g)
            res = _dot(jnp.concatenate([a_mat[i], q[i]], axis=0).astype(BF16), bd(hs[g]))
            hs[g] = res[0:CHUNK, :] + g_mat[i]
            ys.append(res[CHUNK:2 * CHUNK, :] + y0[i])
        y_chunks[c] = jnp.concatenate(ys, axis=1)

    half = n_chunks // STAGE_BATCHES
    pending = []
    for first in range(0, n_chunks, half):
        insts = [(c, g) for c in range(first, first + half) for g in range(N_GROUPS)]
        for si, stage in enumerate(stages):
            for i in insts:
                stage(i)
            if pending and si % 2 == 1:
                chain_step(pending.pop(0))
        pending += list(range(first, first + half))
    for c in pending:
        chain_step(c)
    for g in range(N_GROUPS):
        h_ref[g] = hs[g]

    yh = jnp.concatenate([y_chunks[c] for c in range(n_chunks)], axis=0)
    mu = segsum(yh) * (1.0 / HEAD)
    dlt = yh - mu
    var = segsum(dlt * dlt) * (1.0 / HEAD)
    yn = dlt * lax.rsqrt(var + GN_EPS) * vec(ROW_LNW) + vec(ROW_LNB)
    z = lerp(COL_Z, D_RWKV)
    y_rwkv = (yn + bon_ref[...]) * (z * _sigmoid(z))
    mix_ref[:, D_POOL:D_POOL + D_RWKV] = y_rwkv.astype(BF16)

    out = _dot(mix_ref[...], wout_ref[...])
    xo = x_ref[...] + gate * out
    ms2 = jnp.mean(xo * xo, axis=-1, keepdims=True)
    o_ref[...] = xo * lax.rsqrt(ms2 + NORM_EPS) * fg_ref[...]

    p_ref[8:PAD, :] = p_ref[TB + 8:TB + PAD, :]


def _constants():
    hb = np.arange(GROUP) // HEAD
    bdmask = (hb[:, None] == hb[None, :]).astype(np.float32)
    ti = np.arange(CHUNK)[:, None]
    si = (np.arange(2 * GROUP) % HEAD)[None, :]
    trimask = np.concatenate([(ti > si), (ti >= si)], axis=0).astype(np.float32)
    sg = (np.arange(GROUP) % HEAD)[None, :]
    eye = (ti == sg).astype(np.float32)
    m16 = (ti // 16) == (sg // 16)
    m32 = (ti // 32) == (sg // 32)
    tmask = np.stack([m16, m32 & ~m16, ~m32], axis=0).astype(np.float32)
    rr = np.arange(CUM_ROWS)
    same = (rr[:, None] // CHUNK) == (rr[None, :] // CHUNK)
    tril = same & (rr[None, :] <= rr[:, None])
    cmat = tril.astype(np.float32)
    return bdmask, trimask, eye, tmask, cmat


def kernel(x, c, w_ada, b_ada, norm_g, w_in, pool_w, pool_scale, mu_shift, w0, w_up, a0, a_up,
           k_k, k_a, r_k, ln_w, ln_b, w_out, final_g):
    B, T, _ = x.shape
    assert w_ada.shape[0] == 1 and T % TB == 0
    l = 0

    mod = _ada_mod(c, w_ada, b_ada, l).reshape(B, 3, D_MODEL)

    s0 = 2 * D_POOL
    spans = [(0, s0), (s0, s0 + 512), (s0 + 576, s0 + 1088), (s0 + 1088, s0 + 1600),
             (s0 + 1664, s0 + 2176), (s0 + 512, s0 + 576), (s0 + 1600, s0 + 1664)]
    w_in_p = jnp.concatenate([w_in[l, :, a:b].astype(BF16) for a, b in spans], axis=1)
    mu_p = jnp.concatenate([mu_shift[l, a - s0:b - s0] for a, b in spans[1:]], axis=0)[None, :]

    bdmask, trimask, eye, tmask, cmat = _constants()
    bdmask_b = jnp.asarray(bdmask, BF16)
    segones = bdmask_b
    trimask = jnp.asarray(trimask)
    eye = jnp.asarray(eye)
    tmask = jnp.asarray(tmask)
    cmat = jnp.asarray(cmat, BF16)

    def full(a):
        nd = a.ndim
        return pl.BlockSpec(a.shape, lambda b, t, _nd=nd: (0,) * _nd)

    def layer(a):
        a = a.reshape(a.shape[0], 1, a.shape[1]) if a.ndim == 2 else a
        nd = a.ndim
        return a, pl.BlockSpec((None,) + a.shape[1:], lambda b, t, _nd=nd: (l,) + (0,) * (_nd - 1))

    operands = [
        (x, pl.BlockSpec((None, TB, D_MODEL), lambda b, t: (b, t, 0))),
        (mod, pl.BlockSpec((None, 3, D_MODEL), lambda b, t: (b, 0, 0))),
        layer(norm_g), (w_in_p, full(w_in_p)), (mu_p, full(mu_p)), layer(w_up), layer(a_up),
        layer(w0), layer(a0), layer(k_k), layer(k_a), layer(r_k), layer(ln_w), layer(ln_b),
        layer(pool_scale), layer(pool_w), layer(w_out.astype(BF16)),
        (final_g[None, :], full(final_g[None, :])),
    ] + [(a, full(a)) for a in (bdmask_b, trimask, eye, tmask, cmat, segones)]
    in_specs = [spec for _, spec in operands]

    blk = lambda: pltpu.VMEM((TB, D_RWKV), F32)
    scratch = [
        pltpu.VMEM((TB + PAD, N_IN), F32),
        pltpu.VMEM((TB + PAD, D_POOL), F32),
        pltpu.VMEM((TB + PAD, D_POOL), F32),
        blk(), blk(), blk(), blk(), blk(), blk(),
        blk(), blk(),
        pltpu.VMEM((TB, D_MODEL), BF16),
        pltpu.VMEM((TB, D_MODEL), BF16),
        pltpu.VMEM((N_GROUPS, HEAD, GROUP), F32),
    ]
    return pl.pallas_call(
        _fused_kernel,
        grid=(B, T // TB),
        in_specs=in_specs,
        out_specs=pl.BlockSpec((None, TB, D_MODEL), lambda b, t: (b, t, 0)),
        out_shape=jax.ShapeDtypeStruct((B, T, D_MODEL), x.dtype),
        scratch_shapes=scratch,
        compiler_params=pltpu.CompilerParams(
            dimension_semantics=("arbitrary", "arbitrary"),
            vmem_limit_bytes=VMEM_LIMIT_BYTES),
        name="hybrid_block",
    )(*[a for a, _ in operands])
```

```python
import functools

import numpy as np
import jax
import jax.numpy as jnp
from jax import lax
from jax.experimental import pallas as pl
from jax.experimental.pallas import tpu as pltpu

F32 = jnp.float32
BF16 = jnp.bfloat16

D_MODEL = 1024
D_POOL = 512
D_RWKV = 512
HEAD = 64
LORA = 64
POOL_WINDOWS = (2, 4, 8, 16)
POOL_GROUP = 128
NORM_EPS = 1e-6
GN_EPS = 64e-5
L2_EPS = 1e-12

COL_U = 0
COL_PZ = 512
COL_R = 1024
COL_K = 1536
COL_V = 2048
COL_Z = 2560
COL_LO = 3072
N_IN = 3200
SEG0 = 1024

CHUNK = 64
GROUP = 256
HEADS_PER_GROUP = GROUP // HEAD
N_GROUPS = D_RWKV // GROUP
TB = 512
CUM_ROWS = 256
STAGE_BATCHES = 1
PAD = 24

ROW_W0, ROW_A0, ROW_KK, ROW_KA, ROW_RK, ROW_LNW, ROW_LNB, ROW_PSCALE = range(8)

VMEM_LIMIT_BYTES = 56 * 1024 * 1024


def _split_hi_lo(x):
    hi = x.astype(BF16)
    lo = (x - hi.astype(F32)).astype(BF16)
    return hi, lo


def _dot(a, b):
    return jnp.dot(a, b, preferred_element_type=F32)


def _sigmoid(x):
    return 0.5 * jnp.tanh(0.5 * x) + 0.5


ADA_BN = 512
LANES = 128


def _ada_kernel(cb_ref, w_ref, b_ref, o_ref):
    w = w_ref[...]
    rows = []
    for b in range(cb_ref.shape[0]):
        cb = cb_ref[b]
        cols = [jnp.sum(w[:, j:j + LANES] * cb, axis=0, keepdims=True) for j in range(0, ADA_BN, LANES)]
        rows.append(jnp.concatenate(cols, axis=1))
    o_ref[...] = jnp.concatenate(rows, axis=0) + b_ref[...]


def _ada_mod(c, w_ada, b_ada, l):
    nb, n = c.shape[0], w_ada.shape[2]
    cb = jnp.broadcast_to(c[:, :, None], (nb, D_MODEL, LANES))
    return pl.pallas_call(
        _ada_kernel,
        grid=(n // ADA_BN,),
        in_specs=[
            pl.BlockSpec((nb, D_MODEL, LANES), lambda j: (0, 0, 0)),
            pl.BlockSpec((None, D_MODEL, ADA_BN), lambda j: (l, 0, j)),
            pl.BlockSpec((None, 1, ADA_BN), lambda j: (l, 0, j)),
        ],
        out_specs=pl.BlockSpec((nb, ADA_BN), lambda j: (0, j)),
        out_shape=jax.ShapeDtypeStruct((nb, n), F32),
        name="adaln_mod",
    )(cb, w_ada, b_ada.reshape(b_ada.shape[0], 1, n))


def _head_blocks(x, bdmask):
    xb = x.astype(BF16)
    return [xb * bdmask[h * HEAD:(h + 1) * HEAD, :] for h in range(HEADS_PER_GROUP)]


def _block_diag(x, bdmask):
    return jnp.concatenate(_head_blocks(x, bdmask), axis=0)


T_HEAD_ORDER = (0, 2, 1, 3)


def _heads_to_rows_t(x):
    xt = jnp.concatenate([x[:, 0:2 * HEAD], x[:, 2 * HEAD:4 * HEAD]], axis=0).T
    return jnp.concatenate([xt[0:HEAD, :], xt[HEAD:2 * HEAD, :]], axis=1)


def _fused_kernel(x_ref, mod_ref, ng_ref, win_ref, mu_ref, wup_ref, aup_ref,
                  w0_ref, a0_ref, kk_ref, ka_ref, rk_ref, lnw_ref, lnb_ref, pscale_ref,
                  pw_ref, wout_ref, fg_ref, bdmask_ref, trimask_ref, eye_ref, tmask_ref, cmat_ref,
                  segones_ref, o_ref,
                  p_ref, sa_ref, sb_ref, at_ref, rt_ref, bt_ref, kt_ref, bdc_ref, kdc_ref,
                  v_ref, bon_ref, mix_ref, hb_ref, h_ref):
    t = pl.program_id(1)

    @pl.when(t == 0)
    def _():
        p_ref[0:PAD, :] = jnp.zeros((PAD, N_IN), F32)
        sa_ref[0:8, :] = jnp.zeros((8, D_POOL), F32)
        sb_ref[0:8, :] = jnp.zeros((8, D_POOL), F32)
        h_ref[...] = jnp.zeros(h_ref.shape, F32)

    shift = mod_ref[0:1, :]
    scale = mod_ref[1:2, :]
    gate = mod_ref[2:3, :]

    xb = x_ref[...]
    ms = jnp.mean(xb * xb, axis=-1, keepdims=True)
    hmod = xb * lax.rsqrt(ms + NORM_EPS) * (ng_ref[...] * (1.0 + scale)) + shift
    hb_ref[...] = hmod.astype(BF16)

    def project(col, width):
        p_ref[PAD:PAD + TB, col:col + width] = _dot(hb_ref[...], win_ref[:, col:col + width])

    project(COL_LO, 2 * LORA)
    project(COL_R, D_RWKV)

    vec_refs = (w0_ref, a0_ref, kk_ref, ka_ref, rk_ref, lnw_ref, lnb_ref, pscale_ref)

    def vec(row):
        return vec_refs[row][...]

    def diag2(m0, m1):
        z = jnp.zeros_like(m0)
        return jnp.concatenate([jnp.concatenate([m0, z], axis=1),
                                jnp.concatenate([z, m1], axis=1)], axis=0)

    def lerp(col, width, row0=0, nrows=TB):
        ext = p_ref[PAD + row0 - 8:PAD + row0 + nrows, col:col + width]
        cur = ext[8:, :]
        prev = pltpu.roll(ext, 1, axis=0)[8:, :]
        return cur + (prev - cur) * mu_ref[0:1, col - SEG0:col - SEG0 + width]

    lo = lerp(COL_LO, 2 * LORA)
    lane = lax.broadcasted_iota(jnp.int32, lo.shape, 1)
    lo = jnp.where(lane < LORA, jnp.tanh(lo), lo)
    lo_hi, lo_lo = _split_hi_lo(lo)
    lora = diag2(wup_ref[...].astype(BF16), aup_ref[...].astype(BF16))
    lin = _dot(jnp.concatenate([lo_hi, lo_lo], axis=1), jnp.concatenate([lora, lora], axis=0))
    logw = -float(np.exp(-0.5) * np.log2(np.e)) * _sigmoid(vec(ROW_W0) + lin[:, 0:D_RWKV])
    a = _sigmoid(vec(ROW_A0) + lin[:, D_RWKV:2 * D_RWKV])

    def segsum(xv):
        parts = []
        for half in range(N_GROUPS):
            parts.append(_dot(xv[:, GROUP * half:GROUP * (half + 1)].astype(BF16), segones_ref[...]))
        return jnp.concatenate(parts, axis=1)

    lw_hi, lw_lo = _split_hi_lo(logw)
    cum = jnp.concatenate(
        [_dot(cmat_ref[...], lw_hi[rb:rb + CUM_ROWS, :]) + _dot(cmat_ref[...], lw_lo[rb:rb + CUM_ROWS, :])
         for rb in range(0, TB, CUM_ROWS)], axis=0)
    n_chunks = TB // CHUNK
    cum_end = [cum[(c + 1) * CHUNK - 1:(c + 1) * CHUNK, :] for c in range(n_chunks)]
    p_chunk = [jnp.exp2(ce) for ce in cum_end]
    p_tot = jnp.concatenate([jnp.broadcast_to(pc, (CHUNK, D_RWKV)) for pc in p_chunk], axis=0)

    project(COL_K, D_RWKV)
    r = lerp(COL_R, D_RWKV)
    rt_ref[...] = r * jnp.exp2(cum)
    bon_ref[...] = r * vec(ROW_RK)

    project(COL_V, D_RWKV)
    k = lerp(COL_K, D_RWKV)
    kkr = k * vec(ROW_KK)
    ssq = segsum(kkr * kkr)
    kk = kkr * lax.rsqrt(jnp.maximum(ssq, L2_EPS * L2_EPS))
    k2 = k * (a * vec(ROW_KA) + (1.0 - vec(ROW_KA)))
    p_inv = jnp.exp2(-cum)
    bt = kk * a * p_inv
    kt = k2 * p_inv
    bt_ref[...] = bt
    kt_ref[...] = kt
    bdc_ref[...] = bt * p_tot
    kdc_ref[...] = kt * p_tot
    at_ref[...] = -kk * jnp.exp2(cum - logw)
    bonus_rk = segsum(bon_ref[...] * k2)

    v = lerp(COL_V, D_RWKV)
    v_ref[...] = v
    bon_ref[...] = bonus_rk * v

    diffs = []

    def pool_windows():
        n_ext = TB + PAD
        sa_ref[8:n_ext, :] = p_ref[8:n_ext, 0:512] + p_ref[7:n_ext - 1, 0:512]
        sb_ref[8:n_ext, 128:512] = sa_ref[8:n_ext, 128:512] + sa_ref[6:n_ext - 2, 128:512]
        sa_ref[8:n_ext, 256:512] = sb_ref[8:n_ext, 256:512] + sb_ref[4:n_ext - 4, 256:512]
        sb_ref[8:n_ext, 384:512] = sa_ref[8:n_ext, 384:512] + sa_ref[0:n_ext - 8, 384:512]

    def pool_diffs():
        pos = t * TB + lax.broadcasted_iota(jnp.int32, (TB, 1), 0) + 1
        wsum_refs = (sa_ref, sb_ref, sa_ref, sb_ref)
        for g, win in enumerate(POOL_WINDOWS):
            lanes = slice(g * POOL_GROUP, (g + 1) * POOL_GROUP)
            cnt = jnp.minimum(pos, win).astype(F32)
            mean = wsum_refs[g][PAD:PAD + TB, lanes] / cnt
            diffs.append(mean - p_ref[PAD:PAD + TB, lanes])

    def pool_out(half):
        lanes = slice(256 * half, 256 * half + 256)
        d2 = jnp.concatenate(diffs[2 * half:2 * half + 2], axis=1).astype(BF16)
        yp = _dot(d2, diag2(pw_ref[2 * half].astype(BF16), pw_ref[2 * half + 1].astype(BF16)))
        zz = p_ref[PAD:PAD + TB, COL_PZ + 256 * half:COL_PZ + 256 * half + 256]
        yp = yp * pscale_ref[:, lanes] * (zz * _sigmoid(zz))
        mix_ref[:, lanes] = yp.astype(BF16)

    half_w = D_POOL // 2
    fillers = [
        lambda: project(COL_U, half_w),
        lambda: project(COL_U + half_w, half_w),
        lambda: (project(COL_PZ, half_w), pool_windows()),
        lambda: (project(COL_PZ + half_w, half_w), pool_diffs()),
        lambda: (project(COL_Z, half_w), pool_out(0)),
        lambda: (project(COL_Z + half_w, half_w), pool_out(1)),
    ]

    bdmask = bdmask_ref[...]
    eye = eye_ref[...]

    def blocks(xv):
        return _head_blocks(xv, bdmask)

    def bd(xv):
        return _block_diag(xv, bdmask)

    def bdt(blks):
        return jnp.concatenate([blks[h] for h in T_HEAD_ORDER], axis=0)

    def rows_of(c):
        return slice(c * CHUNK, (c + 1) * CHUNK)

    def lanes_of(g):
        return slice(GROUP * g, GROUP * (g + 1))

    lab, lak, mrb, mrk = {}, {}, {}, {}
    tinv, lpow, lakv, mrkv, kv, xm = {}, {}, {}, {}, {}, {}
    w, u0, q, y0, a_mat, g_mat = {}, {}, {}, {}, {}, {}

    def st_scores(i):
        rows, lanes = rows_of(i[0]), lanes_of(i[1])
        ar = jnp.concatenate([at_ref[rows, lanes], rt_ref[rows, lanes]], axis=0).astype(BF16)
        rhs1 = jnp.concatenate([bdt(blocks(_heads_to_rows_t(bt_ref[rows, lanes]))),
                                bdt(blocks(_heads_to_rows_t(kt_ref[rows, lanes])))], axis=1)
        s = _dot(ar, rhs1) * trimask_ref[...]
        lab[i] = s[0:CHUNK, 0:GROUP]
        lak[i] = s[0:CHUNK, GROUP:2 * GROUP]
        mrb[i] = s[CHUNK:2 * CHUNK, 0:GROUP]
        mrk[i] = s[CHUNK:2 * CHUNK, GROUP:2 * GROUP]

    def st_inv_start(i):
        rows, lanes = rows_of(i[0]), lanes_of(i[1])
        ldiag = lab[i] * tmask_ref[0]
        tinv[i] = eye + ldiag
        lpow[i] = _dot(ldiag.astype(BF16), bd(ldiag))
        kd_t = _heads_to_rows_t(kdc_ref[rows, lanes])
        xv = _dot(jnp.concatenate([lak[i], mrk[i], kd_t], axis=0).astype(BF16),
                  bdt(blocks(v_ref[rows, lanes])))
        lakv[i] = xv[0:CHUNK, :]
        mrkv[i] = xv[CHUNK:2 * CHUNK, :]
        kv[i] = xv[2 * CHUNK:3 * CHUNK, :]

    def st_neumann(i):
        res = _dot(jnp.concatenate([lpow[i], tinv[i]], axis=0).astype(BF16), bd(lpow[i]))
        lpow[i] = res[0:CHUNK, :]
        tinv[i] = tinv[i] + res[CHUNK:2 * CHUNK, :]

    def st_neumann_last(i):
        tinv[i] = tinv[i] + _dot(tinv[i].astype(BF16), bd(lpow[i]))

    def st_merge_x(level):
        def stage(i):
            xm[i] = _dot((lab[i] * tmask_ref[level]).astype(BF16), bd(tinv[i]))
        return stage

    def st_merge_t(i):
        tinv[i] = tinv[i] + _dot(tinv[i].astype(BF16), bd(xm[i]))

    def st_wu(i):
        rows, lanes = rows_of(i[0]), lanes_of(i[1])
        wu = _dot(tinv[i].astype(BF16),
                  jnp.concatenate([bdt(blocks(at_ref[rows, lanes])), bdt(blocks(lakv[i]))], axis=1))
        w[i] = wu[:, 0:GROUP]
        u0[i] = wu[:, GROUP:2 * GROUP]

    def st_transition(i):
        rows, lanes = rows_of(i[0]), lanes_of(i[1])
        bd_t = _heads_to_rows_t(bdc_ref[rows, lanes])
        res = _dot(jnp.concatenate([mrb[i], bd_t], axis=0).astype(BF16),
                   jnp.concatenate([bdt(blocks(w[i])), bdt(blocks(u0[i]))], axis=1))
        q[i] = rt_ref[rows, lanes] + res[0:CHUNK, 0:GROUP]
        y0[i] = res[0:CHUNK, GROUP:2 * GROUP] + mrkv[i]
        a_mat[i] = eye * p_chunk[i[0]][:, lanes] + res[CHUNK:2 * CHUNK, 0:GROUP]
        g_mat[i] = res[CHUNK:2 * CHUNK, GROUP:2 * GROUP] + kv[i]

    stages = [st_scores, st_inv_start, st_neumann, st_neumann, st_neumann_last,
              st_merge_x(1), st_merge_t, st_merge_x(2), st_merge_t, st_wu, st_transition]

    hs = [h_ref[g] for g in range(N_GROUPS)]

    y_chunks = {}

    def chain_step(c):
        ys = []
        for g in range(N_GROUPS):
            i = (c, g)
            res = _dot(jnp.concatenate([a_mat[i], q[i]], axis=0).astype(BF16), bd(hs[g]))
            hs[g] = res[0:CHUNK, :] + g_mat[i]
            ys.append(res[CHUNK:2 * CHUNK, :] + y0[i])
        y_chunks[c] = jnp.concatenate(ys, axis=1)

    half = n_chunks // STAGE_BATCHES
    pending = []
    for first in range(0, n_chunks, half):
        insts = [(c, g) for c in range(first, first + half) for g in range(N_GROUPS)]
        for si, stage in enumerate(stages):
            for i in insts:
                stage(i)
            if pending and si % 2 == 1:
                chain_step(pending.pop(0))
        pending += list(range(first, first + half))
    for c in pending:
        chain_step(c)
        if fillers:
            fillers.pop(0)()
    while fillers:
        fillers.pop(0)()
    for g in range(N_GROUPS):
        h_ref[g] = hs[g]

    yh = jnp.concatenate([y_chunks[c] for c in range(n_chunks)], axis=0)
    mu = segsum(yh) * (1.0 / HEAD)
    dlt = yh - mu
    var = segsum(dlt * dlt) * (1.0 / HEAD)
    yn = dlt * lax.rsqrt(var + GN_EPS) * vec(ROW_LNW) + vec(ROW_LNB)
    z = lerp(COL_Z, D_RWKV)
    y_rwkv = (yn + bon_ref[...]) * (z * _sigmoid(z))
    mix_ref[:, D_POOL:D_POOL + D_RWKV] = y_rwkv.astype(BF16)

    out = _dot(mix_ref[...], wout_ref[...])
    xo = x_ref[...] + gate * out
    ms2 = jnp.mean(xo * xo, axis=-1, keepdims=True)
    o_ref[...] = xo * lax.rsqrt(ms2 + NORM_EPS) * fg_ref[...]

    p_ref[8:PAD, :] = p_ref[TB + 8:TB + PAD, :]


def _constants():
    hb = np.arange(GROUP) // HEAD
    bdmask = (hb[:, None] == hb[None, :]).astype(np.float32)
    ti = np.arange(CHUNK)[:, None]
    si = (np.arange(2 * GROUP) % HEAD)[None, :]
    trimask = np.concatenate([(ti > si), (ti >= si)], axis=0).astype(np.float32)
    sg = (np.arange(GROUP) % HEAD)[None, :]
    eye = (ti == sg).astype(np.float32)
    m16 = (ti // 16) == (sg // 16)
    m32 = (ti // 32) == (sg // 32)
    tmask = np.stack([m16, m32 & ~m16, ~m32], axis=0).astype(np.float32)
    rr = np.arange(CUM_ROWS)
    same = (rr[:, None] // CHUNK) == (rr[None, :] // CHUNK)
    tril = same & (rr[None, :] <= rr[:, None])
    cmat = tril.astype(np.float32)
    return bdmask, trimask, eye, tmask, cmat


def kernel(x, c, w_ada, b_ada, norm_g, w_in, pool_w, pool_scale, mu_shift, w0, w_up, a0, a_up,
           k_k, k_a, r_k, ln_w, ln_b, w_out, final_g):
    B, T, _ = x.shape
    assert w_ada.shape[0] == 1 and T % TB == 0
    l = 0

    mod = _ada_mod(c, w_ada, b_ada, l).reshape(B, 3, D_MODEL)

    s0 = 2 * D_POOL
    spans = [(0, s0), (s0, s0 + 512), (s0 + 576, s0 + 1088), (s0 + 1088, s0 + 1600),
             (s0 + 1664, s0 + 2176), (s0 + 512, s0 + 576), (s0 + 1600, s0 + 1664)]
    w_in_p = jnp.concatenate([w_in[l, :, a:b].astype(BF16) for a, b in spans], axis=1)
    mu_p = jnp.concatenate([mu_shift[l, a - s0:b - s0] for a, b in spans[1:]], axis=0)[None, :]

    bdmask, trimask, eye, tmask, cmat = _constants()
    bdmask_b = jnp.asarray(bdmask, BF16)
    segones = bdmask_b
    trimask = jnp.asarray(trimask)
    eye = jnp.asarray(eye)
    tmask = jnp.asarray(tmask)
    cmat = jnp.asarray(cmat, BF16)

    def full(a):
        nd = a.ndim
        return pl.BlockSpec(a.shape, lambda b, t, _nd=nd: (0,) * _nd)

    def layer(a):
        a = a.reshape(a.shape[0], 1, a.shape[1]) if a.ndim == 2 else a
        nd = a.ndim
        return a, pl.BlockSpec((None,) + a.shape[1:], lambda b, t, _nd=nd: (l,) + (0,) * (_nd - 1))

    operands = [
        (x, pl.BlockSpec((None, TB, D_MODEL), lambda b, t: (b, t, 0))),
        (mod, pl.BlockSpec((None, 3, D_MODEL), lambda b, t: (b, 0, 0))),
        layer(norm_g), (w_in_p, full(w_in_p)), (mu_p, full(mu_p)), layer(w_up), layer(a_up),
        layer(w0), layer(a0), layer(k_k), layer(k_a), layer(r_k), layer(ln_w), layer(ln_b),
        layer(pool_scale), layer(pool_w), layer(w_out.astype(BF16)),
        (final_g[None, :], full(final_g[None, :])),
    ] + [(a, full(a)) for a in (bdmask_b, trimask, eye, tmask, cmat, segones)]
    in_specs = [spec for _, spec in operands]

    blk = lambda: pltpu.VMEM((TB, D_RWKV), F32)
    scratch = [
        pltpu.VMEM((TB + PAD, N_IN), F32),
        pltpu.VMEM((TB + PAD, D_POOL), F32),
        pltpu.VMEM((TB + PAD, D_POOL), F32),
        blk(), blk(), blk(), blk(), blk(), blk(),
        blk(), blk(),
        pltpu.VMEM((TB, D_MODEL), BF16),
        pltpu.VMEM((TB, D_MODEL), BF16),
        pltpu.VMEM((N_GROUPS, HEAD, GROUP), F32),
    ]
    return pl.pallas_call(
        _fused_kernel,
        grid=(B, T // TB),
        in_specs=in_specs,
        out_specs=pl.BlockSpec((None, TB, D_MODEL), lambda b, t: (b, t, 0)),
        out_shape=jax.ShapeDtypeStruct((B, T, D_MODEL), x.dtype),
        scratch_shapes=scratch,
        compiler_params=pltpu.CompilerParams(
            dimension_semantics=("arbitrary", "arbitrary"),
            vmem_limit_bytes=VMEM_LIMIT_BYTES),
        name="hybrid_block",
    )(*[a for a, _ in operands])
```

```python
import functools

import numpy as np
import jax
import jax.numpy as jnp
from jax import lax
from jax.experimental import pallas as pl
from jax.experimental.pallas import tpu as pltpu

F32 = jnp.float32
BF16 = jnp.bfloat16

D_MODEL = 1024
D_POOL = 512
D_RWKV = 512
HEAD = 64
LORA = 64
POOL_WINDOWS = (2, 4, 8, 16)
POOL_GROUP = 128
NORM_EPS = 1e-6
GN_EPS = 64e-5
L2_EPS = 1e-12

COL_U = 0
COL_PZ = 512
COL_R = 1024
COL_K = 1536
COL_V = 2048
COL_Z = 2560
COL_LO = 3072
N_IN = 3200
SEG0 = 1024

CHUNK = 64
GROUP = 256
HEADS_PER_GROUP = GROUP // HEAD
N_GROUPS = D_RWKV // GROUP
TB = 512
CUM_ROWS = 256
STAGE_BATCHES = 1
GN_CHUNKS = 8
PAD = 24

ROW_W0, ROW_A0, ROW_KK, ROW_KA, ROW_RK, ROW_LNW, ROW_LNB, ROW_PSCALE = range(8)

VMEM_LIMIT_BYTES = 56 * 1024 * 1024


def _split_hi_lo(x):
    hi = x.astype(BF16)
    lo = (x - hi.astype(F32)).astype(BF16)
    return hi, lo


def _dot(a, b):
    return jnp.dot(a, b, preferred_element_type=F32)


def _sigmoid(x):
    return 0.5 * jnp.tanh(0.5 * x) + 0.5


ADA_ROWS = 256
LANES = 128


def _ada_kernel(cb_ref, w_ref, b_ref, o_ref):
    @pl.when(pl.program_id(0) == 0)
    def _():
        o_ref[...] = jnp.broadcast_to(b_ref[...], o_ref.shape)

    w = w_ref[...]
    rows = []
    for b in range(cb_ref.shape[0]):
        cb = cb_ref[b]
        cols = [jnp.sum(w[:, j:j + LANES] * cb, axis=0, keepdims=True)
                for j in range(0, w.shape[1], LANES)]
        rows.append(jnp.concatenate(cols, axis=1))
    o_ref[...] += jnp.concatenate(rows, axis=0)


def _ada_mod(c, w_ada, b_ada, l):
    nb, n = c.shape[0], w_ada.shape[2]
    cb = jnp.broadcast_to(c[:, :, None], (nb, D_MODEL, LANES))
    return pl.pallas_call(
        _ada_kernel,
        grid=(D_MODEL // ADA_ROWS,),
        in_specs=[
            pl.BlockSpec((nb, ADA_ROWS, LANES), lambda j: (0, j, 0)),
            pl.BlockSpec((None, ADA_ROWS, n), lambda j: (l, j, 0)),
            pl.BlockSpec((None, 1, n), lambda j: (l, 0, 0)),
        ],
        out_specs=pl.BlockSpec((nb, n), lambda j: (0, 0)),
        out_shape=jax.ShapeDtypeStruct((nb, n), F32),
        name="adaln_mod",
    )(cb, w_ada, b_ada.reshape(b_ada.shape[0], 1, n))


_S0 = 2 * D_POOL
W_IN_SPANS = ((0, _S0), (_S0, _S0 + 512), (_S0 + 576, _S0 + 1088), (_S0 + 1088, _S0 + 1600),
              (_S0 + 1664, _S0 + 2176), (_S0 + 512, _S0 + 576), (_S0 + 1600, _S0 + 1664))
WPREP_ROWS = 256


def _win_prep_kernel(w_ref, o_ref):
    w = w_ref[...]
    o_ref[...] = jnp.concatenate([w[:, a:b] for a, b in W_IN_SPANS], axis=1).astype(BF16)


def _win_prep(w_in, l):
    d, n = w_in.shape[1], w_in.shape[2]
    return pl.pallas_call(
        _win_prep_kernel,
        grid=(d // WPREP_ROWS,),
        in_specs=[pl.BlockSpec((None, WPREP_ROWS, n), lambda i: (l, i, 0))],
        out_specs=pl.BlockSpec((WPREP_ROWS, n), lambda i: (i, 0)),
        out_shape=jax.ShapeDtypeStruct((d, n), BF16),
        name="w_in_prep",
    )(w_in)


def _head_blocks(x, bdmask):
    xb = x.astype(BF16)
    return [xb * bdmask[h * HEAD:(h + 1) * HEAD, :] for h in range(HEADS_PER_GROUP)]


def _block_diag(x, bdmask):
    return jnp.concatenate(_head_blocks(x, bdmask), axis=0)


T_HEAD_ORDER = (0, 2, 1, 3)


def _heads_to_rows_t(x):
    xt = jnp.concatenate([x[:, 0:2 * HEAD], x[:, 2 * HEAD:4 * HEAD]], axis=0).T
    return jnp.concatenate([xt[0:HEAD, :], xt[HEAD:2 * HEAD, :]], axis=1)


def _fused_kernel(x_ref, mod_ref, ng_ref, win_ref, mu_ref, wup_ref, aup_ref,
                  w0_ref, a0_ref, kk_ref, ka_ref, rk_ref, lnw_ref, lnb_ref, pscale_ref,
                  pw_ref, wout_ref, fg_ref, bdmask_ref, trimask_ref, eye_ref, tmask_ref, cmat_ref,
                  segones_ref, o_ref,
                  p_ref, sa_ref, sb_ref, at_ref, rt_ref, bt_ref, kt_ref, bdc_ref, kdc_ref,
                  v_ref, bon_ref, mix_ref, hb_ref, h_ref):
    t = pl.program_id(1)

    @pl.when(t == 0)
    def _():
        p_ref[0:PAD, :] = jnp.zeros((PAD, N_IN), F32)
        sa_ref[0:8, :] = jnp.zeros((8, D_POOL), F32)
        sb_ref[0:8, :] = jnp.zeros((8, D_POOL), F32)
        h_ref[...] = jnp.zeros(h_ref.shape, F32)

    shift = mod_ref[0:1, :]
    scale = mod_ref[1:2, :]
    gate = mod_ref[2:3, :]

    xb = x_ref[...]
    ms = jnp.mean(xb * xb, axis=-1, keepdims=True)
    hmod = xb * lax.rsqrt(ms + NORM_EPS) * (ng_ref[...] * (1.0 + scale)) + shift
    hb_ref[...] = hmod.astype(BF16)

    def project(col, width):
        p_ref[PAD:PAD + TB, col:col + width] = _dot(hb_ref[...], win_ref[:, col:col + width])

    project(COL_LO, 2 * LORA)
    project(COL_R, D_RWKV)

    vec_refs = (w0_ref, a0_ref, kk_ref, ka_ref, rk_ref, lnw_ref, lnb_ref, pscale_ref)

    def vec(row):
        return vec_refs[row][...]

    def diag2(m0, m1):
        z = jnp.zeros_like(m0)
        return jnp.concatenate([jnp.concatenate([m0, z], axis=1),
                                jnp.concatenate([z, m1], axis=1)], axis=0)

    def lerp(col, width, row0=0, nrows=TB):
        ext = p_ref[PAD + row0 - 8:PAD + row0 + nrows, col:col + width]
        cur = ext[8:, :]
        prev = pltpu.roll(ext, 1, axis=0)[8:, :]
        return cur + (prev - cur) * mu_ref[0:1, col - SEG0:col - SEG0 + width]

    lo = lerp(COL_LO, 2 * LORA)
    lane = lax.broadcasted_iota(jnp.int32, lo.shape, 1)
    lo = jnp.where(lane < LORA, jnp.tanh(lo), lo)
    lo_hi, lo_lo = _split_hi_lo(lo)
    lora = diag2(wup_ref[...].astype(BF16), aup_ref[...].astype(BF16))
    lin = _dot(jnp.concatenate([lo_hi, lo_lo], axis=1), jnp.concatenate([lora, lora], axis=0))
    logw = -float(np.exp(-0.5) * np.log2(np.e)) * _sigmoid(vec(ROW_W0) + lin[:, 0:D_RWKV])
    a = _sigmoid(vec(ROW_A0) + lin[:, D_RWKV:2 * D_RWKV])

    def segsum(xv):
        parts = []
        for half in range(N_GROUPS):
            parts.append(_dot(xv[:, GROUP * half:GROUP * (half + 1)].astype(BF16), segones_ref[...]))
        return jnp.concatenate(parts, axis=1)

    lw_hi, lw_lo = _split_hi_lo(logw)
    cum = jnp.concatenate(
        [_dot(cmat_ref[...], lw_hi[rb:rb + CUM_ROWS, :]) + _dot(cmat_ref[...], lw_lo[rb:rb + CUM_ROWS, :])
         for rb in range(0, TB, CUM_ROWS)], axis=0)
    n_chunks = TB // CHUNK
    cum_end = [cum[(c + 1) * CHUNK - 1:(c + 1) * CHUNK, :] for c in range(n_chunks)]
    p_chunk = [jnp.exp2(ce) for ce in cum_end]
    p_tot = jnp.concatenate([jnp.broadcast_to(pc, (CHUNK, D_RWKV)) for pc in p_chunk], axis=0)

    project(COL_K, D_RWKV)
    r = lerp(COL_R, D_RWKV)
    rt_ref[...] = r * jnp.exp2(cum)
    bon_ref[...] = r * vec(ROW_RK)

    project(COL_V, D_RWKV)
    k = lerp(COL_K, D_RWKV)
    kkr = k * vec(ROW_KK)
    ssq = segsum(kkr * kkr)
    kk = kkr * lax.rsqrt(jnp.maximum(ssq, L2_EPS * L2_EPS))
    k2 = k * (a * vec(ROW_KA) + (1.0 - vec(ROW_KA)))
    p_inv = jnp.exp2(-cum)
    bt = kk * a * p_inv
    kt = k2 * p_inv
    bt_ref[...] = bt
    kt_ref[...] = kt
    bdc_ref[...] = bt * p_tot
    kdc_ref[...] = kt * p_tot
    at_ref[...] = -kk * jnp.exp2(cum - logw)
    bonus_rk = segsum(bon_ref[...] * k2)

    v = lerp(COL_V, D_RWKV)
    v_ref[...] = v
    bon_ref[...] = bonus_rk * v

    diffs = []

    def pool_windows():
        n_ext = TB + PAD
        sa_ref[8:n_ext, :] = p_ref[8:n_ext, 0:512] + p_ref[7:n_ext - 1, 0:512]
        sb_ref[8:n_ext, 128:512] = sa_ref[8:n_ext, 128:512] + sa_ref[6:n_ext - 2, 128:512]
        sa_ref[8:n_ext, 256:512] = sb_ref[8:n_ext, 256:512] + sb_ref[4:n_ext - 4, 256:512]
        sb_ref[8:n_ext, 384:512] = sa_ref[8:n_ext, 384:512] + sa_ref[0:n_ext - 8, 384:512]

    def pool_diffs():
        pos = t * TB + lax.broadcasted_iota(jnp.int32, (TB, 1), 0) + 1
        wsum_refs = (sa_ref, sb_ref, sa_ref, sb_ref)
        for g, win in enumerate(POOL_WINDOWS):
            lanes = slice(g * POOL_GROUP, (g + 1) * POOL_GROUP)
            cnt = jnp.minimum(pos, win).astype(F32)
            mean = wsum_refs[g][PAD:PAD + TB, lanes] / cnt
            diffs.append(mean - p_ref[PAD:PAD + TB, lanes])

    def pool_out(half):
        lanes = slice(256 * half, 256 * half + 256)
        d2 = jnp.concatenate(diffs[2 * half:2 * half + 2], axis=1).astype(BF16)
        yp = _dot(d2, diag2(pw_ref[2 * half].astype(BF16), pw_ref[2 * half + 1].astype(BF16)))
        zz = p_ref[PAD:PAD + TB, COL_PZ + 256 * half:COL_PZ + 256 * half + 256]
        yp = yp * pscale_ref[:, lanes] * (zz * _sigmoid(zz))
        mix_ref[:, lanes] = yp.astype(BF16)

    half_w = D_POOL // 2
    fillers = [
        lambda: project(COL_U, half_w),
        lambda: project(COL_U + half_w, half_w),
        lambda: (project(COL_PZ, half_w), pool_windows()),
        lambda: (project(COL_PZ + half_w, half_w), pool_diffs()),
        lambda: (project(COL_Z, half_w), pool_out(0)),
        lambda: (project(COL_Z + half_w, half_w), pool_out(1)),
    ]
    N_Z_FILLERS = len(fillers)

    bdmask = bdmask_ref[...]
    eye = eye_ref[...]

    def blocks(xv):
        return _head_blocks(xv, bdmask)

    def bd(xv):
        return _block_diag(xv, bdmask)

    def bdt(blks):
        return jnp.concatenate([blks[h] for h in T_HEAD_ORDER], axis=0)

    def rows_of(c):
        return slice(c * CHUNK, (c + 1) * CHUNK)

    def lanes_of(g):
        return slice(GROUP * g, GROUP * (g + 1))

    lab, lak, mrb, mrk = {}, {}, {}, {}
    tinv, lpow, lakv, mrkv, kv, xm = {}, {}, {}, {}, {}, {}
    w, u0, q, y0, a_mat, g_mat = {}, {}, {}, {}, {}, {}

    def st_scores(i):
        rows, lanes = rows_of(i[0]), lanes_of(i[1])
        ar = jnp.concatenate([at_ref[rows, lanes], rt_ref[rows, lanes]], axis=0).astype(BF16)
        rhs1 = jnp.concatenate([bdt(blocks(_heads_to_rows_t(bt_ref[rows, lanes]))),
                                bdt(blocks(_heads_to_rows_t(kt_ref[rows, lanes])))], axis=1)
        s = _dot(ar, rhs1) * trimask_ref[...]
        lab[i] = s[0:CHUNK, 0:GROUP]
        lak[i] = s[0:CHUNK, GROUP:2 * GROUP]
        mrb[i] = s[CHUNK:2 * CHUNK, 0:GROUP]
        mrk[i] = s[CHUNK:2 * CHUNK, GROUP:2 * GROUP]

    def st_inv_start(i):
        rows, lanes = rows_of(i[0]), lanes_of(i[1])
        ldiag = lab[i] * tmask_ref[0]
        tinv[i] = eye + ldiag
        lpow[i] = _dot(ldiag.astype(BF16), bd(ldiag))
        kd_t = _heads_to_rows_t(kdc_ref[rows, lanes])
        xv = _dot(jnp.concatenate([lak[i], mrk[i], kd_t], axis=0).astype(BF16),
                  bdt(blocks(v_ref[rows, lanes])))
        lakv[i] = xv[0:CHUNK, :]
        mrkv[i] = xv[CHUNK:2 * CHUNK, :]
        kv[i] = xv[2 * CHUNK:3 * CHUNK, :]

    def st_neumann(i):
        res = _dot(jnp.concatenate([lpow[i], tinv[i]], axis=0).astype(BF16), bd(lpow[i]))
        lpow[i] = res[0:CHUNK, :]
        tinv[i] = tinv[i] + res[CHUNK:2 * CHUNK, :]

    def st_neumann_last(i):
        tinv[i] = tinv[i] + _dot(tinv[i].astype(BF16), bd(lpow[i]))

    def st_merge_x(level):
        def stage(i):
            xm[i] = _dot((lab[i] * tmask_ref[level]).astype(BF16), bd(tinv[i]))
        return stage

    def st_merge_t(i):
        tinv[i] = tinv[i] + _dot(tinv[i].astype(BF16), bd(xm[i]))

    def st_wu(i):
        rows, lanes = rows_of(i[0]), lanes_of(i[1])
        wu = _dot(tinv[i].astype(BF16),
                  jnp.concatenate([bdt(blocks(at_ref[rows, lanes])), bdt(blocks(lakv[i]))], axis=1))
        w[i] = wu[:, 0:GROUP]
        u0[i] = wu[:, GROUP:2 * GROUP]

    def st_transition(i):
        rows, lanes = rows_of(i[0]), lanes_of(i[1])
        bd_t = _heads_to_rows_t(bdc_ref[rows, lanes])
        res = _dot(jnp.concatenate([mrb[i], bd_t], axis=0).astype(BF16),
                   jnp.concatenate([bdt(blocks(w[i])), bdt(blocks(u0[i]))], axis=1))
        q[i] = rt_ref[rows, lanes] + res[0:CHUNK, 0:GROUP]
        y0[i] = res[0:CHUNK, GROUP:2 * GROUP] + mrkv[i]
        a_mat[i] = eye * p_chunk[i[0]][:, lanes] + res[CHUNK:2 * CHUNK, 0:GROUP]
        g_mat[i] = res[CHUNK:2 * CHUNK, GROUP:2 * GROUP] + kv[i]

    stages = [st_scores, st_inv_start, st_neumann, st_neumann, st_neumann_last,
              st_merge_x(1), st_merge_t, st_merge_x(2), st_merge_t, st_wu, st_transition]

    hs = [h_ref[g] for g in range(N_GROUPS)]

    y_chunks = {}

    def chain_step(c):
        ys = []
        for g in range(N_GROUPS):
            i = (c, g)
            res = _dot(jnp.concatenate([a_mat[i], q[i]], axis=0).astype(BF16), bd(hs[g]))
            hs[g] = res[0:CHUNK, :] + g_mat[i]
            ys.append(res[CHUNK:2 * CHUNK, :] + y0[i])
        y_chunks[c] = jnp.concatenate(ys, axis=1)

    half = n_chunks // STAGE_BATCHES
    pending = []
    for first in range(0, n_chunks, half):
        insts = [(c, g) for c in range(first, first + half) for g in range(N_GROUPS)]
        for si, stage in enumerate(stages):
            for i in insts:
                stage(i)
            if pending and si % 2 == 1:
                chain_step(pending.pop(0))
        pending += list(range(first, first + half))
    def group_norm(c0, c1):
        row0, nrows = c0 * CHUNK, (c1 - c0) * CHUNK
        yh = jnp.concatenate([y_chunks[c] for c in range(c0, c1)], axis=0)
        mu = segsum(yh) * (1.0 / HEAD)
        dlt = yh - mu
        var = segsum(dlt * dlt) * (1.0 / HEAD)
        yn = dlt * lax.rsqrt(var + GN_EPS) * vec(ROW_LNW) + vec(ROW_LNB)
        z = lerp(COL_Z, D_RWKV, row0, nrows)
        y_rwkv = (yn + bon_ref[row0:row0 + nrows, :]) * (z * _sigmoid(z))
        mix_ref[row0:row0 + nrows, D_POOL:D_POOL + D_RWKV] = y_rwkv.astype(BF16)

    n_fillers, n_done = len(fillers), 0
    for c in pending:
        chain_step(c)
        if fillers:
            fillers.pop(0)()
        z_ready = n_fillers - len(fillers) >= N_Z_FILLERS
        if z_ready and c + 1 - n_done >= GN_CHUNKS:
            group_norm(n_done, c + 1)
            n_done = c + 1
    while fillers:
        fillers.pop(0)()
    if n_done < n_chunks:
        group_norm(n_done, n_chunks)
    for g in range(N_GROUPS):
        h_ref[g] = hs[g]

    out = _dot(mix_ref[...], wout_ref[...])
    xo = x_ref[...] + gate * out
    ms2 = jnp.mean(xo * xo, axis=-1, keepdims=True)
    o_ref[...] = xo * lax.rsqrt(ms2 + NORM_EPS) * fg_ref[...]

    p_ref[8:PAD, :] = p_ref[TB + 8:TB + PAD, :]


def _constants():
    hb = np.arange(GROUP) // HEAD
    bdmask = (hb[:, None] == hb[None, :]).astype(np.float32)
    ti = np.arange(CHUNK)[:, None]
    si = (np.arange(2 * GROUP) % HEAD)[None, :]
    trimask = np.concatenate([(ti > si), (ti >= si)], axis=0).astype(np.float32)
    sg = (np.arange(GROUP) % HEAD)[None, :]
    eye = (ti == sg).astype(np.float32)
    m16 = (ti // 16) == (sg // 16)
    m32 = (ti // 32) == (sg // 32)
    tmask = np.stack([m16, m32 & ~m16, ~m32], axis=0).astype(np.float32)
    rr = np.arange(CUM_ROWS)
    same = (rr[:, None] // CHUNK) == (rr[None, :] // CHUNK)
    tril = same & (rr[None, :] <= rr[:, None])
    cmat = tril.astype(np.float32)
    return bdmask, trimask, eye, tmask, cmat


def kernel(x, c, w_ada, b_ada, norm_g, w_in, pool_w, pool_scale, mu_shift, w0, w_up, a0, a_up,
           k_k, k_a, r_k, ln_w, ln_b, w_out, final_g):
    B, T, _ = x.shape
    assert w_ada.shape[0] == 1 and T % TB == 0
    l = 0

    mod = _ada_mod(c, w_ada, b_ada, l).reshape(B, 3, D_MODEL)

    w_in_p = _win_prep(w_in, l)
    mu_p = jnp.concatenate([mu_shift[l, a - _S0:b - _S0] for a, b in W_IN_SPANS[1:]], axis=0)[None, :]

    bdmask, trimask, eye, tmask, cmat = _constants()
    bdmask_b = jnp.asarray(bdmask, BF16)
    segones = bdmask_b
    trimask = jnp.asarray(trimask)
    eye = jnp.asarray(eye)
    tmask = jnp.asarray(tmask)
    cmat = jnp.asarray(cmat, BF16)

    def full(a):
        nd = a.ndim
        return pl.BlockSpec(a.shape, lambda b, t, _nd=nd: (0,) * _nd)

    def layer(a):
        a = a.reshape(a.shape[0], 1, a.shape[1]) if a.ndim == 2 else a
        nd = a.ndim
        return a, pl.BlockSpec((None,) + a.shape[1:], lambda b, t, _nd=nd: (l,) + (0,) * (_nd - 1))

    operands = [
        (x, pl.BlockSpec((None, TB, D_MODEL), lambda b, t: (b, t, 0))),
        (mod, pl.BlockSpec((None, 3, D_MODEL), lambda b, t: (b, 0, 0))),
        layer(norm_g), (w_in_p, full(w_in_p)), (mu_p, full(mu_p)), layer(w_up), layer(a_up),
        layer(w0), layer(a0), layer(k_k), layer(k_a), layer(r_k), layer(ln_w), layer(ln_b),
        layer(pool_scale), layer(pool_w), layer(w_out.astype(BF16)),
        (final_g[None, :], full(final_g[None, :])),
    ] + [(a, full(a)) for a in (bdmask_b, trimask, eye, tmask, cmat, segones)]
    in_specs = [spec for _, spec in operands]

    blk = lambda: pltpu.VMEM((TB, D_RWKV), F32)
    scratch = [
        pltpu.VMEM((TB + PAD, N_IN), F32),
        pltpu.VMEM((TB + PAD, D_POOL), F32),
        pltpu.VMEM((TB + PAD, D_POOL), F32),
        blk(), blk(), blk(), blk(), blk(), blk(),
        blk(), blk(),
        pltpu.VMEM((TB, D_MODEL), BF16),
        pltpu.VMEM((TB, D_MODEL), BF16),
        pltpu.VMEM((N_GROUPS, HEAD, GROUP), F32),
    ]
    return pl.pallas_call(
        _fused_kernel,
        grid=(B, T // TB),
        in_specs=in_specs,
        out_specs=pl.BlockSpec((None, TB, D_MODEL), lambda b, t: (b, t, 0)),
        out_shape=jax.ShapeDtypeStruct((B, T, D_MODEL), x.dtype),
        scratch_shapes=scratch,
        compiler_params=pltpu.CompilerParams(
            dimension_semantics=("arbitrary", "arbitrary"),
            vmem_limit_bytes=VMEM_LIMIT_BYTES),
        name="hybrid_block",
    )(*[a for a, _ in operands])
```

```python
import functools

import numpy as np
import jax
import jax.numpy as jnp
from jax import lax
from jax.experimental import pallas as pl
from jax.experimental.pallas import tpu as pltpu

F32 = jnp.float32
BF16 = jnp.bfloat16

D_MODEL = 1024
D_POOL = 512
D_RWKV = 512
HEAD = 64
LORA = 64
POOL_WINDOWS = (2, 4, 8, 16)
POOL_GROUP = 128
NORM_EPS = 1e-6
GN_EPS = 64e-5
L2_EPS = 1e-12

COL_U = 0
COL_PZ = 512
COL_R = 1024
COL_K = 1536
COL_V = 2048
COL_Z = 2560
COL_LO = 3072
N_IN = 3200
SEG0 = 1024

CHUNK = 64
GROUP = 256
HEADS_PER_GROUP = GROUP // HEAD
N_GROUPS = D_RWKV // GROUP
TB = 512
CUM_ROWS = 256
STAGE_BATCHES = 1
GN_CHUNKS = 8
PAD = 24

ROW_W0, ROW_A0, ROW_KK, ROW_KA, ROW_RK, ROW_LNW, ROW_LNB, ROW_PSCALE = range(8)

VMEM_LIMIT_BYTES = 56 * 1024 * 1024


def _split_hi_lo(x):
    hi = x.astype(BF16)
    lo = (x - hi.astype(F32)).astype(BF16)
    return hi, lo


def _dot(a, b):
    return jnp.dot(a, b, preferred_element_type=F32)


def _sigmoid(x):
    return 0.5 * jnp.tanh(0.5 * x) + 0.5


ADA_ROWS = 256
LANES = 128


def _ada_kernel(cb_ref, w_ref, b_ref, o_ref):
    @pl.when(pl.program_id(0) == 0)
    def _():
        o_ref[...] = jnp.broadcast_to(b_ref[...], o_ref.shape)

    w = w_ref[...]
    rows = []
    for b in range(cb_ref.shape[0]):
        cb = cb_ref[b]
        cols = [jnp.sum(w[:, j:j + LANES] * cb, axis=0, keepdims=True)
                for j in range(0, w.shape[1], LANES)]
        rows.append(jnp.concatenate(cols, axis=1))
    o_ref[...] += jnp.concatenate(rows, axis=0)


def _ada_mod(c, w_ada, b_ada, l):
    nb, n = c.shape[0], w_ada.shape[2]
    cb = jnp.broadcast_to(c[:, :, None], (nb, D_MODEL, LANES))
    return pl.pallas_call(
        _ada_kernel,
        grid=(D_MODEL // ADA_ROWS,),
        in_specs=[
            pl.BlockSpec((nb, ADA_ROWS, LANES), lambda j: (0, j, 0)),
            pl.BlockSpec((None, ADA_ROWS, n), lambda j: (l, j, 0)),
            pl.BlockSpec((None, 1, n), lambda j: (l, 0, 0)),
        ],
        out_specs=pl.BlockSpec((nb, n), lambda j: (0, 0)),
        out_shape=jax.ShapeDtypeStruct((nb, n), F32),
        name="adaln_mod",
    )(cb, w_ada, b_ada.reshape(b_ada.shape[0], 1, n))


_S0 = 2 * D_POOL
W_IN_SPANS = ((0, _S0), (_S0, _S0 + 512), (_S0 + 576, _S0 + 1088), (_S0 + 1088, _S0 + 1600),
              (_S0 + 1664, _S0 + 2176), (_S0 + 512, _S0 + 576), (_S0 + 1600, _S0 + 1664))
WPREP_ROWS = 256


def _win_prep_kernel(w_ref, o_ref):
    w = w_ref[...]
    o_ref[...] = jnp.concatenate([w[:, a:b] for a, b in W_IN_SPANS], axis=1).astype(BF16)


def _win_prep(w_in, l):
    d, n = w_in.shape[1], w_in.shape[2]
    return pl.pallas_call(
        _win_prep_kernel,
        grid=(d // WPREP_ROWS,),
        in_specs=[pl.BlockSpec((None, WPREP_ROWS, n), lambda i: (l, i, 0))],
        out_specs=pl.BlockSpec((WPREP_ROWS, n), lambda i: (i, 0)),
        out_shape=jax.ShapeDtypeStruct((d, n), BF16),
        name="w_in_prep",
    )(w_in)


def _head_blocks(x, bdmask):
    xb = x.astype(BF16)
    return [xb * bdmask[h * HEAD:(h + 1) * HEAD, :] for h in range(HEADS_PER_GROUP)]


def _block_diag(x, bdmask):
    return jnp.concatenate(_head_blocks(x, bdmask), axis=0)


T_HEAD_ORDER = (0, 2, 1, 3)


def _heads_to_rows_t(x):
    xt = jnp.concatenate([x[:, 0:2 * HEAD], x[:, 2 * HEAD:4 * HEAD]], axis=0).T
    return jnp.concatenate([xt[0:HEAD, :], xt[HEAD:2 * HEAD, :]], axis=1)


def _fused_kernel(x_ref, mod_ref, ng_ref, win_ref, mu_ref, wup_ref, aup_ref,
                  w0_ref, a0_ref, kk_ref, ka_ref, rk_ref, lnw_ref, lnb_ref, pscale_ref,
                  pw_ref, wout_ref, fg_ref, bdmask_ref, trimask_ref, eye_ref, tmask_ref, cmat_ref,
                  segones_ref, o_ref,
                  p_ref, sa_ref, sb_ref, at_ref, rt_ref, bt_ref, kt_ref, bdc_ref, kdc_ref,
                  v_ref, bon_ref, mix_ref, hb_ref, h_ref):
    t = pl.program_id(1)

    @pl.when(t == 0)
    def _():
        p_ref[0:PAD, :] = jnp.zeros((PAD, N_IN), F32)
        sa_ref[0:8, :] = jnp.zeros((8, D_POOL), F32)
        sb_ref[0:8, :] = jnp.zeros((8, D_POOL), F32)
        h_ref[...] = jnp.zeros(h_ref.shape, F32)

    shift = mod_ref[0:1, :]
    scale = mod_ref[1:2, :]
    gate = mod_ref[2:3, :]

    xb = x_ref[...]
    ms = jnp.mean(xb * xb, axis=-1, keepdims=True)
    hmod = xb * lax.rsqrt(ms + NORM_EPS) * (ng_ref[...] * (1.0 + scale)) + shift
    hb_ref[...] = hmod.astype(BF16)

    def project(col, width):
        p_ref[PAD:PAD + TB, col:col + width] = _dot(hb_ref[...], win_ref[:, col:col + width])

    project(COL_LO, 2 * LORA)
    project(COL_R, D_RWKV)

    vec_refs = (w0_ref, a0_ref, kk_ref, ka_ref, rk_ref, lnw_ref, lnb_ref, pscale_ref)

    def vec(row):
        return vec_refs[row][...]

    def diag2(m0, m1):
        z = jnp.zeros_like(m0)
        return jnp.concatenate([jnp.concatenate([m0, z], axis=1),
                                jnp.concatenate([z, m1], axis=1)], axis=0)

    def lerp(col, width, row0=0, nrows=TB):
        ext = p_ref[PAD + row0 - 8:PAD + row0 + nrows, col:col + width]
        cur = ext[8:, :]
        prev = pltpu.roll(ext, 1, axis=0)[8:, :]
        return cur + (prev - cur) * mu_ref[0:1, col - SEG0:col - SEG0 + width]

    lo = lerp(COL_LO, 2 * LORA)
    lane = lax.broadcasted_iota(jnp.int32, lo.shape, 1)
    lo = jnp.where(lane < LORA, jnp.tanh(lo), lo)
    lo_hi, lo_lo = _split_hi_lo(lo)
    lora = diag2(wup_ref[...].astype(BF16), aup_ref[...].astype(BF16))
    lin = _dot(jnp.concatenate([lo_hi, lo_lo], axis=1), jnp.concatenate([lora, lora], axis=0))
    logw = -float(np.exp(-0.5) * np.log2(np.e)) * _sigmoid(vec(ROW_W0) + lin[:, 0:D_RWKV])
    a = _sigmoid(vec(ROW_A0) + lin[:, D_RWKV:2 * D_RWKV])

    def segsum(xv):
        parts = []
        for half in range(N_GROUPS):
            parts.append(_dot(xv[:, GROUP * half:GROUP * (half + 1)].astype(BF16), segones_ref[...]))
        return jnp.concatenate(parts, axis=1)

    lw_hi, lw_lo = _split_hi_lo(logw)
    cum = jnp.concatenate(
        [_dot(cmat_ref[...], lw_hi[rb:rb + CUM_ROWS, :]) + _dot(cmat_ref[...], lw_lo[rb:rb + CUM_ROWS, :])
         for rb in range(0, TB, CUM_ROWS)], axis=0)
    n_chunks = TB // CHUNK
    cum_end = [cum[(c + 1) * CHUNK - 1:(c + 1) * CHUNK, :] for c in range(n_chunks)]
    p_chunk = [jnp.exp2(ce) for ce in cum_end]
    p_tot = jnp.concatenate([jnp.broadcast_to(pc, (CHUNK, D_RWKV)) for pc in p_chunk], axis=0)

    project(COL_K, D_RWKV)
    r = lerp(COL_R, D_RWKV)
    rt_ref[...] = r * jnp.exp2(cum)
    bon_ref[...] = r * vec(ROW_RK)

    project(COL_V, D_RWKV)
    k = lerp(COL_K, D_RWKV)
    kkr = k * vec(ROW_KK)
    ssq = segsum(kkr * kkr)
    kk = kkr * lax.rsqrt(jnp.maximum(ssq, L2_EPS * L2_EPS))
    k2 = k * (a * vec(ROW_KA) + (1.0 - vec(ROW_KA)))
    p_inv = jnp.exp2(-cum)
    bt = kk * a * p_inv
    kt = k2 * p_inv
    bt_ref[...] = bt
    kt_ref[...] = kt
    bdc_ref[...] = bt * p_tot
    kdc_ref[...] = kt * p_tot
    at_ref[...] = -kk * jnp.exp2(cum - logw)
    bonus_rk = segsum(bon_ref[...] * k2)

    v = lerp(COL_V, D_RWKV)
    v_ref[...] = v
    bon_ref[...] = bonus_rk * v

    diffs = []

    def pool_windows():
        n_ext = TB + PAD
        sa_ref[8:n_ext, :] = p_ref[8:n_ext, 0:512] + p_ref[7:n_ext - 1, 0:512]
        sb_ref[8:n_ext, 128:512] = sa_ref[8:n_ext, 128:512] + sa_ref[6:n_ext - 2, 128:512]
        sa_ref[8:n_ext, 256:512] = sb_ref[8:n_ext, 256:512] + sb_ref[4:n_ext - 4, 256:512]
        sb_ref[8:n_ext, 384:512] = sa_ref[8:n_ext, 384:512] + sa_ref[0:n_ext - 8, 384:512]

    def pool_diffs():
        pos = t * TB + lax.broadcasted_iota(jnp.int32, (TB, 1), 0) + 1
        wsum_refs = (sa_ref, sb_ref, sa_ref, sb_ref)
        for g, win in enumerate(POOL_WINDOWS):
            lanes = slice(g * POOL_GROUP, (g + 1) * POOL_GROUP)
            cnt = jnp.minimum(pos, win).astype(F32)
            mean = wsum_refs[g][PAD:PAD + TB, lanes] / cnt
            diffs.append(mean - p_ref[PAD:PAD + TB, lanes])

    def pool_out(half):
        lanes = slice(256 * half, 256 * half + 256)
        d2 = jnp.concatenate(diffs[2 * half:2 * half + 2], axis=1).astype(BF16)
        yp = _dot(d2, diag2(pw_ref[2 * half].astype(BF16), pw_ref[2 * half + 1].astype(BF16)))
        zz = p_ref[PAD:PAD + TB, COL_PZ + 256 * half:COL_PZ + 256 * half + 256]
        yp = yp * pscale_ref[:, lanes] * (zz * _sigmoid(zz))
        mix_ref[:, lanes] = yp.astype(BF16)

    half_w = D_POOL // 2
    fillers = [
        lambda: project(COL_U, half_w),
        lambda: project(COL_U + half_w, half_w),
        lambda: (project(COL_PZ, half_w), pool_windows()),
        lambda: (project(COL_PZ + half_w, half_w), pool_diffs()),
        lambda: (project(COL_Z, half_w), pool_out(0)),
        lambda: (project(COL_Z + half_w, half_w), pool_out(1)),
    ]
    N_Z_FILLERS = len(fillers)

    bdmask = bdmask_ref[...]
    eye = eye_ref[...]

    def blocks(xv):
        return _head_blocks(xv, bdmask)

    def bd(xv):
        return _block_diag(xv, bdmask)

    def bdt(blks):
        return jnp.concatenate([blks[h] for h in T_HEAD_ORDER], axis=0)

    def rows_of(c):
        return slice(c * CHUNK, (c + 1) * CHUNK)

    def lanes_of(g):
        return slice(GROUP * g, GROUP * (g + 1))

    lab, lak, mrb, mrk = {}, {}, {}, {}
    tinv, lpow, lakv, mrkv, kv, xm = {}, {}, {}, {}, {}, {}
    zt, q, y0, a_mat, g_mat = {}, {}, {}, {}, {}

    def st_scores(i):
        rows, lanes = rows_of(i[0]), lanes_of(i[1])
        ar = jnp.concatenate([at_ref[rows, lanes], rt_ref[rows, lanes]], axis=0).astype(BF16)
        rhs1 = jnp.concatenate([bdt(blocks(_heads_to_rows_t(bt_ref[rows, lanes]))),
                                bdt(blocks(_heads_to_rows_t(kt_ref[rows, lanes])))], axis=1)
        s = _dot(ar, rhs1) * trimask_ref[...]
        lab[i] = s[0:CHUNK, 0:GROUP]
        lak[i] = s[0:CHUNK, GROUP:2 * GROUP]
        mrb[i] = s[CHUNK:2 * CHUNK, 0:GROUP]
        mrk[i] = s[CHUNK:2 * CHUNK, GROUP:2 * GROUP]

    def st_inv_start(i):
        rows, lanes = rows_of(i[0]), lanes_of(i[1])
        ldiag = lab[i] * tmask_ref[0]
        tinv[i] = eye + ldiag
        lpow[i] = _dot(ldiag.astype(BF16), bd(ldiag))
        kd_t = _heads_to_rows_t(kdc_ref[rows, lanes])
        xv = _dot(jnp.concatenate([lak[i], mrk[i], kd_t], axis=0).astype(BF16),
                  bdt(blocks(v_ref[rows, lanes])))
        lakv[i] = xv[0:CHUNK, :]
        mrkv[i] = xv[CHUNK:2 * CHUNK, :]
        kv[i] = xv[2 * CHUNK:3 * CHUNK, :]

    def st_neumann(i):
        res = _dot(jnp.concatenate([lpow[i], tinv[i]], axis=0).astype(BF16), bd(lpow[i]))
        lpow[i] = res[0:CHUNK, :]
        tinv[i] = tinv[i] + res[CHUNK:2 * CHUNK, :]

    def st_neumann_last(i):
        tinv[i] = tinv[i] + _dot(tinv[i].astype(BF16), bd(lpow[i]))

    def st_merge_x(level):
        def stage(i):
            xm[i] = _dot((lab[i] * tmask_ref[level]).astype(BF16), bd(tinv[i]))
        return stage

    def st_merge_t(i):
        tinv[i] = tinv[i] + _dot(tinv[i].astype(BF16), bd(xm[i]))

    def st_last_merge_x(i):
        rows, lanes = rows_of(i[0]), lanes_of(i[1])
        bd_t = _heads_to_rows_t(bdc_ref[rows, lanes])
        res = _dot(jnp.concatenate([lab[i] * tmask_ref[2], mrb[i], bd_t], axis=0).astype(BF16),
                   bd(tinv[i]))
        xm[i] = res[0:CHUNK, :]
        zt[i] = res[CHUNK:3 * CHUNK, :]

    def st_last_merge_t(i):
        zt[i] = zt[i] + _dot(zt[i].astype(BF16), bd(xm[i]))

    def st_transition(i):
        rows, lanes = rows_of(i[0]), lanes_of(i[1])
        res = _dot(zt[i].astype(BF16),
                   jnp.concatenate([bdt(blocks(at_ref[rows, lanes])), bdt(blocks(lakv[i]))], axis=1))
        q[i] = rt_ref[rows, lanes] + res[0:CHUNK, 0:GROUP]
        y0[i] = res[0:CHUNK, GROUP:2 * GROUP] + mrkv[i]
        a_mat[i] = eye * p_chunk[i[0]][:, lanes] + res[CHUNK:2 * CHUNK, 0:GROUP]
        g_mat[i] = res[CHUNK:2 * CHUNK, GROUP:2 * GROUP] + kv[i]

    stages = [st_scores, st_inv_start, st_neumann, st_neumann, st_neumann_last,
              st_merge_x(1), st_merge_t, st_last_merge_x, st_last_merge_t, st_transition]

    hs = [h_ref[g] for g in range(N_GROUPS)]

    y_chunks = {}

    def chain_step(c):
        ys = []
        for g in range(N_GROUPS):
            i = (c, g)
            res = _dot(jnp.concatenate([a_mat[i], q[i]], axis=0).astype(BF16), bd(hs[g]))
            hs[g] = res[0:CHUNK, :] + g_mat[i]
            ys.append(res[CHUNK:2 * CHUNK, :] + y0[i])
        y_chunks[c] = jnp.concatenate(ys, axis=1)

    half = n_chunks // STAGE_BATCHES
    pending = []
    for first in range(0, n_chunks, half):
        insts = [(c, g) for c in range(first, first + half) for g in range(N_GROUPS)]
        for si, stage in enumerate(stages):
            for i in insts:
                stage(i)
            if pending and si % 2 == 1:
                chain_step(pending.pop(0))
        pending += list(range(first, first + half))
    def group_norm(c0, c1):
        row0, nrows = c0 * CHUNK, (c1 - c0) * CHUNK
        yh = jnp.concatenate([y_chunks[c] for c in range(c0, c1)], axis=0)
        mu = segsum(yh) * (1.0 / HEAD)
        dlt = yh - mu
        var = segsum(dlt * dlt) * (1.0 / HEAD)
        yn = dlt * lax.rsqrt(var + GN_EPS) * vec(ROW_LNW) + vec(ROW_LNB)
        z = lerp(COL_Z, D_RWKV, row0, nrows)
        y_rwkv = (yn + bon_ref[row0:row0 + nrows, :]) * (z * _sigmoid(z))
        mix_ref[row0:row0 + nrows, D_POOL:D_POOL + D_RWKV] = y_rwkv.astype(BF16)

    n_fillers, n_done = len(fillers), 0
    for c in pending:
        chain_step(c)
        if fillers:
            fillers.pop(0)()
        z_ready = n_fillers - len(fillers) >= N_Z_FILLERS
        if z_ready and c + 1 - n_done >= GN_CHUNKS:
            group_norm(n_done, c + 1)
            n_done = c + 1
    while fillers:
        fillers.pop(0)()
    if n_done < n_chunks:
        group_norm(n_done, n_chunks)
    for g in range(N_GROUPS):
        h_ref[g] = hs[g]

    out = _dot(mix_ref[...], wout_ref[...])
    xo = x_ref[...] + gate * out
    ms2 = jnp.mean(xo * xo, axis=-1, keepdims=True)
    o_ref[...] = xo * lax.rsqrt(ms2 + NORM_EPS) * fg_ref[...]

    p_ref[8:PAD, :] = p_ref[TB + 8:TB + PAD, :]


def _constants():
    hb = np.arange(GROUP) // HEAD
    bdmask = (hb[:, None] == hb[None, :]).astype(np.float32)
    ti = np.arange(CHUNK)[:, None]
    si = (np.arange(2 * GROUP) % HEAD)[None, :]
    trimask = np.concatenate([(ti > si), (ti >= si)], axis=0).astype(np.float32)
    sg = (np.arange(GROUP) % HEAD)[None, :]
    eye = (ti == sg).astype(np.float32)
    m16 = (ti // 16) == (sg // 16)
    m32 = (ti // 32) == (sg // 32)
    tmask = np.stack([m16, m32 & ~m16, ~m32], axis=0).astype(np.float32)
    rr = np.arange(CUM_ROWS)
    same = (rr[:, None] // CHUNK) == (rr[None, :] // CHUNK)
    tril = same & (rr[None, :] <= rr[:, None])
    cmat = tril.astype(np.float32)
    return bdmask, trimask, eye, tmask, cmat


def kernel(x, c, w_ada, b_ada, norm_g, w_in, pool_w, pool_scale, mu_shift, w0, w_up, a0, a_up,
           k_k, k_a, r_k, ln_w, ln_b, w_out, final_g):
    B, T, _ = x.shape
    assert w_ada.shape[0] == 1 and T % TB == 0
    l = 0

    mod = _ada_mod(c, w_ada, b_ada, l).reshape(B, 3, D_MODEL)

    w_in_p = _win_prep(w_in, l)
    mu_p = jnp.concatenate([mu_shift[l, a - _S0:b - _S0] for a, b in W_IN_SPANS[1:]], axis=0)[None, :]

    bdmask, trimask, eye, tmask, cmat = _constants()
    bdmask_b = jnp.asarray(bdmask, BF16)
    segones = bdmask_b
    trimask = jnp.asarray(trimask)
    eye = jnp.asarray(eye)
    tmask = jnp.asarray(tmask)
    cmat = jnp.asarray(cmat, BF16)

    def full(a):
        nd = a.ndim
        return pl.BlockSpec(a.shape, lambda b, t, _nd=nd: (0,) * _nd)

    def layer(a):
        a = a.reshape(a.shape[0], 1, a.shape[1]) if a.ndim == 2 else a
        nd = a.ndim
        return a, pl.BlockSpec((None,) + a.shape[1:], lambda b, t, _nd=nd: (l,) + (0,) * (_nd - 1))

    operands = [
        (x, pl.BlockSpec((None, TB, D_MODEL), lambda b, t: (b, t, 0))),
        (mod, pl.BlockSpec((None, 3, D_MODEL), lambda b, t: (b, 0, 0))),
        layer(norm_g), (w_in_p, full(w_in_p)), (mu_p, full(mu_p)), layer(w_up), layer(a_up),
        layer(w0), layer(a0), layer(k_k), layer(k_a), layer(r_k), layer(ln_w), layer(ln_b),
        layer(pool_scale), layer(pool_w), layer(w_out.astype(BF16)),
        (final_g[None, :], full(final_g[None, :])),
    ] + [(a, full(a)) for a in (bdmask_b, trimask, eye, tmask, cmat, segones)]
    in_specs = [spec for _, spec in operands]

    blk = lambda: pltpu.VMEM((TB, D_RWKV), F32)
    scratch = [
        pltpu.VMEM((TB + PAD, N_IN), F32),
        pltpu.VMEM((TB + PAD, D_POOL), F32),
        pltpu.VMEM((TB + PAD, D_POOL), F32),
        blk(), blk(), blk(), blk(), blk(), blk(),
        blk(), blk(),
        pltpu.VMEM((TB, D_MODEL), BF16),
        pltpu.VMEM((TB, D_MODEL), BF16),
        pltpu.VMEM((N_GROUPS, HEAD, GROUP), F32),
    ]
    return pl.pallas_call(
        _fused_kernel,
        grid=(B, T // TB),
        in_specs=in_specs,
        out_specs=pl.BlockSpec((None, TB, D_MODEL), lambda b, t: (b, t, 0)),
        out_shape=jax.ShapeDtypeStruct((B, T, D_MODEL), x.dtype),
        scratch_shapes=scratch,
        compiler_params=pltpu.CompilerParams(
            dimension_semantics=("arbitrary", "arbitrary"),
            vmem_limit_bytes=VMEM_LIMIT_BYTES),
        name="hybrid_block",
    )(*[a for a, _ in operands])
```

```python
import functools

import numpy as np
import jax
import jax.numpy as jnp
from jax import lax
from jax.experimental import pallas as pl
from jax.experimental.pallas import tpu as pltpu

F32 = jnp.float32
BF16 = jnp.bfloat16

D_MODEL = 1024
D_POOL = 512
D_RWKV = 512
HEAD = 64
LORA = 64
POOL_WINDOWS = (2, 4, 8, 16)
POOL_GROUP = 128
NORM_EPS = 1e-6
GN_EPS = 64e-5
L2_EPS = 1e-12

COL_U = 0
COL_PZ = 512
COL_R = 1024
COL_K = 1536
COL_V = 2048
COL_Z = 2560
COL_LO = 3072
N_IN = 3200
SEG0 = 1024

CHUNK = 64
GROUP = 256
HEADS_PER_GROUP = GROUP // HEAD
N_GROUPS = D_RWKV // GROUP
TB = 512
CUM_ROWS = 256
STAGE_BATCHES = 1
GN_CHUNKS = 8
INV_BLOCK = 32
N_MERGES = (CHUNK // INV_BLOCK).bit_length() - 1
PAD = 24

ROW_W0, ROW_A0, ROW_KK, ROW_KA, ROW_RK, ROW_LNW, ROW_LNB, ROW_PSCALE = range(8)

VMEM_LIMIT_BYTES = 56 * 1024 * 1024


def _split_hi_lo(x):
    hi = x.astype(BF16)
    lo = (x - hi.astype(F32)).astype(BF16)
    return hi, lo


def _dot(a, b):
    return jnp.dot(a, b, preferred_element_type=F32)


def _sigmoid(x):
    return 0.5 * jnp.tanh(0.5 * x) + 0.5


ADA_ROWS = 256
LANES = 128


def _ada_kernel(cb_ref, w_ref, b_ref, o_ref):
    @pl.when(pl.program_id(0) == 0)
    def _():
        o_ref[...] = jnp.broadcast_to(b_ref[...], o_ref.shape)

    w = w_ref[...]
    rows = []
    for b in range(cb_ref.shape[0]):
        cb = cb_ref[b]
        cols = [jnp.sum(w[:, j:j + LANES] * cb, axis=0, keepdims=True)
                for j in range(0, w.shape[1], LANES)]
        rows.append(jnp.concatenate(cols, axis=1))
    o_ref[...] += jnp.concatenate(rows, axis=0)


def _ada_mod(c, w_ada, b_ada, l):
    nb, n = c.shape[0], w_ada.shape[2]
    cb = jnp.broadcast_to(c[:, :, None], (nb, D_MODEL, LANES))
    return pl.pallas_call(
        _ada_kernel,
        grid=(D_MODEL // ADA_ROWS,),
        in_specs=[
            pl.BlockSpec((nb, ADA_ROWS, LANES), lambda j: (0, j, 0)),
            pl.BlockSpec((None, ADA_ROWS, n), lambda j: (l, j, 0)),
            pl.BlockSpec((None, 1, n), lambda j: (l, 0, 0)),
        ],
        out_specs=pl.BlockSpec((nb, n), lambda j: (0, 0)),
        out_shape=jax.ShapeDtypeStruct((nb, n), F32),
        name="adaln_mod",
    )(cb, w_ada, b_ada.reshape(b_ada.shape[0], 1, n))


_S0 = 2 * D_POOL
W_IN_SPANS = ((0, _S0), (_S0, _S0 + 512), (_S0 + 576, _S0 + 1088), (_S0 + 1088, _S0 + 1600),
              (_S0 + 1664, _S0 + 2176), (_S0 + 512, _S0 + 576), (_S0 + 1600, _S0 + 1664))
WPREP_ROWS = 256


def _win_prep_kernel(w_ref, o_ref):
    w = w_ref[...]
    o_ref[...] = jnp.concatenate([w[:, a:b] for a, b in W_IN_SPANS], axis=1).astype(BF16)


def _win_prep(w_in, l):
    d, n = w_in.shape[1], w_in.shape[2]
    return pl.pallas_call(
        _win_prep_kernel,
        grid=(d // WPREP_ROWS,),
        in_specs=[pl.BlockSpec((None, WPREP_ROWS, n), lambda i: (l, i, 0))],
        out_specs=pl.BlockSpec((WPREP_ROWS, n), lambda i: (i, 0)),
        out_shape=jax.ShapeDtypeStruct((d, n), BF16),
        name="w_in_prep",
    )(w_in)


def _head_blocks(x, bdmask):
    xb = x.astype(BF16)
    return [xb * bdmask[h * HEAD:(h + 1) * HEAD, :] for h in range(HEADS_PER_GROUP)]


def _block_diag(x, bdmask):
    return jnp.concatenate(_head_blocks(x, bdmask), axis=0)


T_HEAD_ORDER = (0, 2, 1, 3)


def _heads_to_rows_t(x):
    xt = jnp.concatenate([x[:, 0:2 * HEAD], x[:, 2 * HEAD:4 * HEAD]], axis=0).T
    return jnp.concatenate([xt[0:HEAD, :], xt[HEAD:2 * HEAD, :]], axis=1)


def _fused_kernel(x_ref, mod_ref, ng_ref, win_ref, mu_ref, wup_ref, aup_ref,
                  w0_ref, a0_ref, kk_ref, ka_ref, rk_ref, lnw_ref, lnb_ref, pscale_ref,
                  pw_ref, wout_ref, fg_ref, bdmask_ref, trimask_ref, eye_ref, tmask_ref, cmat_ref,
                  segones_ref, o_ref,
                  p_ref, sa_ref, sb_ref, at_ref, rt_ref, bt_ref, kt_ref, bdc_ref, kdc_ref,
                  v_ref, bon_ref, mix_ref, hb_ref, h_ref):
    t = pl.program_id(1)

    @pl.when(t == 0)
    def _():
        p_ref[0:PAD, :] = jnp.zeros((PAD, N_IN), F32)
        sa_ref[0:8, :] = jnp.zeros((8, D_POOL), F32)
        sb_ref[0:8, :] = jnp.zeros((8, D_POOL), F32)
        h_ref[...] = jnp.zeros(h_ref.shape, F32)

    shift = mod_ref[0:1, :]
    scale = mod_ref[1:2, :]
    gate = mod_ref[2:3, :]

    xb = x_ref[...]
    ms = jnp.mean(xb * xb, axis=-1, keepdims=True)
    hmod = xb * lax.rsqrt(ms + NORM_EPS) * (ng_ref[...] * (1.0 + scale)) + shift
    hb_ref[...] = hmod.astype(BF16)

    def project(col, width):
        p_ref[PAD:PAD + TB, col:col + width] = _dot(hb_ref[...], win_ref[:, col:col + width])

    project(COL_LO, 2 * LORA)
    project(COL_R, D_RWKV)

    vec_refs = (w0_ref, a0_ref, kk_ref, ka_ref, rk_ref, lnw_ref, lnb_ref, pscale_ref)

    def vec(row):
        return vec_refs[row][...]

    def diag2(m0, m1):
        z = jnp.zeros_like(m0)
        return jnp.concatenate([jnp.concatenate([m0, z], axis=1),
                                jnp.concatenate([z, m1], axis=1)], axis=0)

    def lerp(col, width, row0=0, nrows=TB):
        ext = p_ref[PAD + row0 - 8:PAD + row0 + nrows, col:col + width]
        cur = ext[8:, :]
        prev = pltpu.roll(ext, 1, axis=0)[8:, :]
        return cur + (prev - cur) * mu_ref[0:1, col - SEG0:col - SEG0 + width]

    lo = lerp(COL_LO, 2 * LORA)
    lane = lax.broadcasted_iota(jnp.int32, lo.shape, 1)
    lo = jnp.where(lane < LORA, jnp.tanh(lo), lo)
    lo_hi, lo_lo = _split_hi_lo(lo)
    lora = diag2(wup_ref[...].astype(BF16), aup_ref[...].astype(BF16))
    lin = _dot(jnp.concatenate([lo_hi, lo_lo], axis=1), jnp.concatenate([lora, lora], axis=0))
    logw = -float(np.exp(-0.5) * np.log2(np.e)) * _sigmoid(vec(ROW_W0) + lin[:, 0:D_RWKV])
    a = _sigmoid(vec(ROW_A0) + lin[:, D_RWKV:2 * D_RWKV])

    def segsum(xv):
        parts = []
        for half in range(N_GROUPS):
            parts.append(_dot(xv[:, GROUP * half:GROUP * (half + 1)].astype(BF16), segones_ref[...]))
        return jnp.concatenate(parts, axis=1)

    lw_hi, lw_lo = _split_hi_lo(logw)
    cum = jnp.concatenate(
        [_dot(cmat_ref[...], lw_hi[rb:rb + CUM_ROWS, :]) + _dot(cmat_ref[...], lw_lo[rb:rb + CUM_ROWS, :])
         for rb in range(0, TB, CUM_ROWS)], axis=0)
    n_chunks = TB // CHUNK
    cum_end = [cum[(c + 1) * CHUNK - 1:(c + 1) * CHUNK, :] for c in range(n_chunks)]
    p_chunk = [jnp.exp2(ce) for ce in cum_end]
    p_tot = jnp.concatenate([jnp.broadcast_to(pc, (CHUNK, D_RWKV)) for pc in p_chunk], axis=0)

    project(COL_K, D_RWKV)
    r = lerp(COL_R, D_RWKV)
    rt_ref[...] = r * jnp.exp2(cum)
    bon_ref[...] = r * vec(ROW_RK)

    project(COL_V, D_RWKV)
    k = lerp(COL_K, D_RWKV)
    kkr = k * vec(ROW_KK)
    ssq = segsum(kkr * kkr)
    kk = kkr * lax.rsqrt(jnp.maximum(ssq, L2_EPS * L2_EPS))
    k2 = k * (a * vec(ROW_KA) + (1.0 - vec(ROW_KA)))
    p_inv = jnp.exp2(-cum)
    bt = kk * a * p_inv
    kt = k2 * p_inv
    bt_ref[...] = bt
    kt_ref[...] = kt
    bdc_ref[...] = bt * p_tot
    kdc_ref[...] = kt * p_tot
    at_ref[...] = -kk * jnp.exp2(cum - logw)
    bonus_rk = segsum(bon_ref[...] * k2)

    v = lerp(COL_V, D_RWKV)
    v_ref[...] = v
    bon_ref[...] = bonus_rk * v

    diffs = []

    def pool_windows():
        n_ext = TB + PAD
        sa_ref[8:n_ext, :] = p_ref[8:n_ext, 0:512] + p_ref[7:n_ext - 1, 0:512]
        sb_ref[8:n_ext, 128:512] = sa_ref[8:n_ext, 128:512] + sa_ref[6:n_ext - 2, 128:512]
        sa_ref[8:n_ext, 256:512] = sb_ref[8:n_ext, 256:512] + sb_ref[4:n_ext - 4, 256:512]
        sb_ref[8:n_ext, 384:512] = sa_ref[8:n_ext, 384:512] + sa_ref[0:n_ext - 8, 384:512]

    def pool_diffs():
        pos = t * TB + lax.broadcasted_iota(jnp.int32, (TB, 1), 0) + 1
        wsum_refs = (sa_ref, sb_ref, sa_ref, sb_ref)
        for g, win in enumerate(POOL_WINDOWS):
            lanes = slice(g * POOL_GROUP, (g + 1) * POOL_GROUP)
            cnt = jnp.minimum(pos, win).astype(F32)
            mean = wsum_refs[g][PAD:PAD + TB, lanes] / cnt
            diffs.append(mean - p_ref[PAD:PAD + TB, lanes])

    def pool_out(half):
        lanes = slice(256 * half, 256 * half + 256)
        d2 = jnp.concatenate(diffs[2 * half:2 * half + 2], axis=1).astype(BF16)
        yp = _dot(d2, diag2(pw_ref[2 * half].astype(BF16), pw_ref[2 * half + 1].astype(BF16)))
        zz = p_ref[PAD:PAD + TB, COL_PZ + 256 * half:COL_PZ + 256 * half + 256]
        yp = yp * pscale_ref[:, lanes] * (zz * _sigmoid(zz))
        mix_ref[:, lanes] = yp.astype(BF16)

    half_w = D_POOL // 2
    fillers = [
        lambda: project(COL_U, half_w),
        lambda: project(COL_U + half_w, half_w),
        lambda: (project(COL_PZ, half_w), pool_windows()),
        lambda: (project(COL_PZ + half_w, half_w), pool_diffs()),
        lambda: (project(COL_Z, half_w), pool_out(0)),
        lambda: (project(COL_Z + half_w, half_w), pool_out(1)),
    ]
    N_Z_FILLERS = len(fillers)

    bdmask = bdmask_ref[...]
    eye = eye_ref[...]

    def blocks(xv):
        return _head_blocks(xv, bdmask)

    def bd(xv):
        return _block_diag(xv, bdmask)

    def bdt(blks):
        return jnp.concatenate([blks[h] for h in T_HEAD_ORDER], axis=0)

    def rows_of(c):
        return slice(c * CHUNK, (c + 1) * CHUNK)

    def lanes_of(g):
        return slice(GROUP * g, GROUP * (g + 1))

    lab, lak, mrb, mrk = {}, {}, {}, {}
    tinv, lpow, lakv, mrkv, kv, xm = {}, {}, {}, {}, {}, {}
    zt, q, y0, a_mat, g_mat = {}, {}, {}, {}, {}

    def st_scores(i):
        rows, lanes = rows_of(i[0]), lanes_of(i[1])
        ar = jnp.concatenate([at_ref[rows, lanes], rt_ref[rows, lanes]], axis=0).astype(BF16)
        rhs1 = jnp.concatenate([bdt(blocks(_heads_to_rows_t(bt_ref[rows, lanes]))),
                                bdt(blocks(_heads_to_rows_t(kt_ref[rows, lanes])))], axis=1)
        s = _dot(ar, rhs1) * trimask_ref[...]
        lab[i] = s[0:CHUNK, 0:GROUP]
        lak[i] = s[0:CHUNK, GROUP:2 * GROUP]
        mrb[i] = s[CHUNK:2 * CHUNK, 0:GROUP]
        mrk[i] = s[CHUNK:2 * CHUNK, GROUP:2 * GROUP]

    def st_inv_start(i):
        rows, lanes = rows_of(i[0]), lanes_of(i[1])
        ldiag = lab[i] * tmask_ref[0]
        tinv[i] = eye + ldiag
        lpow[i] = _dot(ldiag.astype(BF16), bd(ldiag))
        kd_t = _heads_to_rows_t(kdc_ref[rows, lanes])
        xv = _dot(jnp.concatenate([lak[i], mrk[i], kd_t], axis=0).astype(BF16),
                  bdt(blocks(v_ref[rows, lanes])))
        lakv[i] = xv[0:CHUNK, :]
        mrkv[i] = xv[CHUNK:2 * CHUNK, :]
        kv[i] = xv[2 * CHUNK:3 * CHUNK, :]

    def st_neumann(i):
        res = _dot(jnp.concatenate([lpow[i], tinv[i]], axis=0).astype(BF16), bd(lpow[i]))
        lpow[i] = res[0:CHUNK, :]
        tinv[i] = tinv[i] + res[CHUNK:2 * CHUNK, :]

    def st_neumann_last(i):
        tinv[i] = tinv[i] + _dot(tinv[i].astype(BF16), bd(lpow[i]))

    def st_merge_x(level):
        def stage(i):
            xm[i] = _dot((lab[i] * tmask_ref[level]).astype(BF16), bd(tinv[i]))
        return stage

    def st_merge_t(i):
        tinv[i] = tinv[i] + _dot(tinv[i].astype(BF16), bd(xm[i]))

    def st_last_merge_x(i):
        rows, lanes = rows_of(i[0]), lanes_of(i[1])
        bd_t = _heads_to_rows_t(bdc_ref[rows, lanes])
        res = _dot(jnp.concatenate([lab[i] * tmask_ref[N_MERGES], mrb[i], bd_t], axis=0).astype(BF16),
                   bd(tinv[i]))
        xm[i] = res[0:CHUNK, :]
        zt[i] = res[CHUNK:3 * CHUNK, :]

    def st_last_merge_t(i):
        zt[i] = zt[i] + _dot(zt[i].astype(BF16), bd(xm[i]))

    def st_transition(i):
        rows, lanes = rows_of(i[0]), lanes_of(i[1])
        res = _dot(zt[i].astype(BF16),
                   jnp.concatenate([bdt(blocks(at_ref[rows, lanes])), bdt(blocks(lakv[i]))], axis=1))
        q[i] = rt_ref[rows, lanes] + res[0:CHUNK, 0:GROUP]
        y0[i] = res[0:CHUNK, GROUP:2 * GROUP] + mrkv[i]
        a_mat[i] = eye * p_chunk[i[0]][:, lanes] + res[CHUNK:2 * CHUNK, 0:GROUP]
        g_mat[i] = res[CHUNK:2 * CHUNK, GROUP:2 * GROUP] + kv[i]

    stages = [st_scores, st_inv_start] + [st_neumann] * (INV_BLOCK.bit_length() - 3) + [st_neumann_last]
    for level in range(1, N_MERGES):
        stages += [st_merge_x(level), st_merge_t]
    stages += [st_last_merge_x, st_last_merge_t, st_transition]

    hs = [h_ref[g] for g in range(N_GROUPS)]

    y_chunks = {}

    def chain_step(c):
        ys = []
        for g in range(N_GROUPS):
            i = (c, g)
            res = _dot(jnp.concatenate([a_mat[i], q[i]], axis=0).astype(BF16), bd(hs[g]))
            hs[g] = res[0:CHUNK, :] + g_mat[i]
            ys.append(res[CHUNK:2 * CHUNK, :] + y0[i])
        y_chunks[c] = jnp.concatenate(ys, axis=1)

    half = n_chunks // STAGE_BATCHES
    pending = []
    for first in range(0, n_chunks, half):
        insts = [(c, g) for c in range(first, first + half) for g in range(N_GROUPS)]
        for si, stage in enumerate(stages):
            for i in insts:
                stage(i)
            if pending and si % 2 == 1:
                chain_step(pending.pop(0))
        pending += list(range(first, first + half))
    def group_norm(c0, c1):
        row0, nrows = c0 * CHUNK, (c1 - c0) * CHUNK
        yh = jnp.concatenate([y_chunks[c] for c in range(c0, c1)], axis=0)
        mu = segsum(yh) * (1.0 / HEAD)
        dlt = yh - mu
        var = segsum(dlt * dlt) * (1.0 / HEAD)
        yn = dlt * lax.rsqrt(var + GN_EPS) * vec(ROW_LNW) + vec(ROW_LNB)
        z = lerp(COL_Z, D_RWKV, row0, nrows)
        y_rwkv = (yn + bon_ref[row0:row0 + nrows, :]) * (z * _sigmoid(z))
        mix_ref[row0:row0 + nrows, D_POOL:D_POOL + D_RWKV] = y_rwkv.astype(BF16)

    n_fillers, n_done = len(fillers), 0
    for c in pending:
        chain_step(c)
        if fillers:
            fillers.pop(0)()
        z_ready = n_fillers - len(fillers) >= N_Z_FILLERS
        if z_ready and c + 1 - n_done >= GN_CHUNKS:
            group_norm(n_done, c + 1)
            n_done = c + 1
    while fillers:
        fillers.pop(0)()
    if n_done < n_chunks:
        group_norm(n_done, n_chunks)
    for g in range(N_GROUPS):
        h_ref[g] = hs[g]

    out = _dot(mix_ref[...], wout_ref[...])
    xo = x_ref[...] + gate * out
    ms2 = jnp.mean(xo * xo, axis=-1, keepdims=True)
    o_ref[...] = xo * lax.rsqrt(ms2 + NORM_EPS) * fg_ref[...]

    p_ref[8:PAD, :] = p_ref[TB + 8:TB + PAD, :]


def _constants():
    hb = np.arange(GROUP) // HEAD
    bdmask = (hb[:, None] == hb[None, :]).astype(np.float32)
    ti = np.arange(CHUNK)[:, None]
    si = (np.arange(2 * GROUP) % HEAD)[None, :]
    trimask = np.concatenate([(ti > si), (ti >= si)], axis=0).astype(np.float32)
    sg = (np.arange(GROUP) % HEAD)[None, :]
    eye = (ti == sg).astype(np.float32)
    same = [(ti // (INV_BLOCK << m)) == (sg // (INV_BLOCK << m)) for m in range(N_MERGES + 1)]
    tmask = np.stack([same[0]] + [same[m] & ~same[m - 1] for m in range(1, N_MERGES + 1)],
                     axis=0).astype(np.float32)
    rr = np.arange(CUM_ROWS)
    same = (rr[:, None] // CHUNK) == (rr[None, :] // CHUNK)
    tril = same & (rr[None, :] <= rr[:, None])
    cmat = tril.astype(np.float32)
    return bdmask, trimask, eye, tmask, cmat


def kernel(x, c, w_ada, b_ada, norm_g, w_in, pool_w, pool_scale, mu_shift, w0, w_up, a0, a_up,
           k_k, k_a, r_k, ln_w, ln_b, w_out, final_g):
    B, T, _ = x.shape
    assert w_ada.shape[0] == 1 and T % TB == 0
    l = 0

    mod = _ada_mod(c, w_ada, b_ada, l).reshape(B, 3, D_MODEL)

    w_in_p = _win_prep(w_in, l)
    mu_p = jnp.concatenate([mu_shift[l, a - _S0:b - _S0] for a, b in W_IN_SPANS[1:]], axis=0)[None, :]

    bdmask, trimask, eye, tmask, cmat = _constants()
    bdmask_b = jnp.asarray(bdmask, BF16)
    segones = bdmask_b
    trimask = jnp.asarray(trimask)
    eye = jnp.asarray(eye)
    tmask = jnp.asarray(tmask)
    cmat = jnp.asarray(cmat, BF16)

    def full(a):
        nd = a.ndim
        return pl.BlockSpec(a.shape, lambda b, t, _nd=nd: (0,) * _nd)

    def layer(a):
        a = a.reshape(a.shape[0], 1, a.shape[1]) if a.ndim == 2 else a
        nd = a.ndim
        return a, pl.BlockSpec((None,) + a.shape[1:], lambda b, t, _nd=nd: (l,) + (0,) * (_nd - 1))

    operands = [
        (x, pl.BlockSpec((None, TB, D_MODEL), lambda b, t: (b, t, 0))),
        (mod, pl.BlockSpec((None, 3, D_MODEL), lambda b, t: (b, 0, 0))),
        layer(norm_g), (w_in_p, full(w_in_p)), (mu_p, full(mu_p)), layer(w_up), layer(a_up),
        layer(w0), layer(a0), layer(k_k), layer(k_a), layer(r_k), layer(ln_w), layer(ln_b),
        layer(pool_scale), layer(pool_w), layer(w_out.astype(BF16)),
        (final_g[None, :], full(final_g[None, :])),
    ] + [(a, full(a)) for a in (bdmask_b, trimask, eye, tmask, cmat, segones)]
    in_specs = [spec for _, spec in operands]

    blk = lambda: pltpu.VMEM((TB, D_RWKV), F32)
    scratch = [
        pltpu.VMEM((TB + PAD, N_IN), F32),
        pltpu.VMEM((TB + PAD, D_POOL), F32),
        pltpu.VMEM((TB + PAD, D_POOL), F32),
        blk(), blk(), blk(), blk(), blk(), blk(),
        blk(), blk(),
        pltpu.VMEM((TB, D_MODEL), BF16),
        pltpu.VMEM((TB, D_MODEL), BF16),
        pltpu.VMEM((N_GROUPS, HEAD, GROUP), F32),
    ]
    return pl.pallas_call(
        _fused_kernel,
        grid=(B, T // TB),
        in_specs=in_specs,
        out_specs=pl.BlockSpec((None, TB, D_MODEL), lambda b, t: (b, t, 0)),
        out_shape=jax.ShapeDtypeStruct((B, T, D_MODEL), x.dtype),
        scratch_shapes=scratch,
        compiler_params=pltpu.CompilerParams(
            dimension_semantics=("arbitrary", "arbitrary"),
            vmem_limit_bytes=VMEM_LIMIT_BYTES),
        name="hybrid_block",
    )(*[a for a, _ in operands])
```

```python
import numpy as np
import jax
import jax.numpy as jnp
from jax import lax
from jax.experimental import pallas as pl
from jax.experimental.pallas import tpu as pltpu

F32 = jnp.float32
BF16 = jnp.bfloat16

D_MODEL = 1024
D_POOL = 512
D_RWKV = 512
HEAD = 64
LORA = 64
POOL_WINDOWS = (2, 4, 8, 16)
POOL_GROUP = 128
NORM_EPS = 1e-6
GN_EPS = 64e-5
L2_EPS = 1e-12

COL_U = 0
COL_PZ = 512
COL_R = 1024
COL_K = 1536
COL_V = 2048
COL_Z = 2560
COL_LO = 3072
N_IN = 3200
SEG0 = 1024

CHUNK = 64
GROUP = 256
HEADS_PER_GROUP = GROUP // HEAD
N_GROUPS = D_RWKV // GROUP
N_STREAMS = 2
TB = 256
CUM_ROWS = 256
INV_BLOCK = 32
N_MERGES = (CHUNK // INV_BLOCK).bit_length() - 1
PAD = 24

VMEM_LIMIT_BYTES = 56 * 1024 * 1024


def _split_hi_lo(x):
    hi = x.astype(BF16)
    lo = (x - hi.astype(F32)).astype(BF16)
    return hi, lo


def _dot(a, b):
    return jnp.dot(a, b, preferred_element_type=F32)


def _sigmoid(x):
    return 0.5 * jnp.tanh(0.5 * x) + 0.5


ADA_ROWS = 256
LANES = 128


def _ada_kernel(cb_ref, w_ref, b_ref, o_ref):
    @pl.when(pl.program_id(0) == 0)
    def _():
        o_ref[...] = jnp.broadcast_to(b_ref[...], o_ref.shape)

    w = w_ref[...]
    rows = []
    for b in range(cb_ref.shape[0]):
        cb = cb_ref[b]
        cols = [jnp.sum(w[:, j:j + LANES] * cb, axis=0, keepdims=True)
                for j in range(0, w.shape[1], LANES)]
        rows.append(jnp.concatenate(cols, axis=1))
    o_ref[...] += jnp.concatenate(rows, axis=0)


def _ada_mod(c, w_ada, b_ada, l):
    nb, n = c.shape[0], w_ada.shape[2]
    cb = jnp.broadcast_to(c[:, :, None], (nb, D_MODEL, LANES))
    return pl.pallas_call(
        _ada_kernel,
        grid=(D_MODEL // ADA_ROWS,),
        in_specs=[
            pl.BlockSpec((nb, ADA_ROWS, LANES), lambda j: (0, j, 0)),
            pl.BlockSpec((None, ADA_ROWS, n), lambda j: (l, j, 0)),
            pl.BlockSpec((None, 1, n), lambda j: (l, 0, 0)),
        ],
        out_specs=pl.BlockSpec((nb, n), lambda j: (0, 0)),
        out_shape=jax.ShapeDtypeStruct((nb, n), F32),
        name="adaln_mod",
    )(cb, w_ada, b_ada.reshape(b_ada.shape[0], 1, n))


_S0 = 2 * D_POOL
W_IN_SPANS = ((0, _S0), (_S0, _S0 + 512), (_S0 + 576, _S0 + 1088), (_S0 + 1088, _S0 + 1600),
              (_S0 + 1664, _S0 + 2176), (_S0 + 512, _S0 + 576), (_S0 + 1600, _S0 + 1664))
WPREP_ROWS = 256


def _win_prep_kernel(w_ref, o_ref):
    w = w_ref[...]
    o_ref[...] = jnp.concatenate([w[:, a:b] for a, b in W_IN_SPANS], axis=1).astype(BF16)


def _win_prep(w_in, l):
    d, n = w_in.shape[1], w_in.shape[2]
    return pl.pallas_call(
        _win_prep_kernel,
        grid=(d // WPREP_ROWS,),
        in_specs=[pl.BlockSpec((None, WPREP_ROWS, n), lambda i: (l, i, 0))],
        out_specs=pl.BlockSpec((WPREP_ROWS, n), lambda i: (i, 0)),
        out_shape=jax.ShapeDtypeStruct((d, n), BF16),
        name="w_in_prep",
    )(w_in)


def _head_blocks(x, bdmask):
    xb = x.astype(BF16)
    return [xb * bdmask[h * HEAD:(h + 1) * HEAD, :] for h in range(HEADS_PER_GROUP)]


def _block_diag(x, bdmask):
    return jnp.concatenate(_head_blocks(x, bdmask), axis=0)


T_HEAD_ORDER = (0, 2, 1, 3)


def _heads_to_rows_t(x):
    xt = jnp.concatenate([x[:, 0:2 * HEAD], x[:, 2 * HEAD:4 * HEAD]], axis=0).T
    return jnp.concatenate([xt[0:HEAD, :], xt[HEAD:2 * HEAD, :]], axis=1)


def _diag2(m0, m1):
    z = jnp.zeros_like(m0)
    return jnp.concatenate([jnp.concatenate([m0, z], axis=1),
                            jnp.concatenate([z, m1], axis=1)], axis=0)


def _interleave(first, second):
    out, j = [], 0
    for i, f in enumerate(first):
        out.append(f)
        want = ((i + 1) * len(second)) // max(len(first), 1)
        while j < want:
            out.append(second[j])
            j += 1
    out.extend(second[j:])
    return out


def _fused_kernel(x_ref, mod_ref, ng_ref, win_ref, mu_ref, wup_ref, aup_ref,
                  w0_ref, a0_ref, kk_ref, ka_ref, rk_ref, lnw_ref, lnb_ref, pscale_ref,
                  pw_ref, wout_ref, fg_ref, bdmask_ref, trimask_ref, eye_ref, tmask_ref, cmat_ref,
                  segones_ref, o_ref,
                  p_all, sa_all, sb_all, at_all, rt_all, bt_all, kt_all, bdc_all, kdc_all,
                  v_all, bon_all, mix_all, hb_all, h_all):
    t = pl.program_id(0)

    @pl.when(t == 0)
    def _():
        p_all[:, 0:PAD, :] = jnp.zeros((N_STREAMS, PAD, N_IN), F32)
        sa_all[:, 0:8, :] = jnp.zeros((N_STREAMS, 8, D_POOL), F32)
        sb_all[:, 0:8, :] = jnp.zeros((N_STREAMS, 8, D_POOL), F32)
        h_all[...] = jnp.zeros(h_all.shape, F32)

    bdmask = bdmask_ref[...]
    eye = eye_ref[...]
    n_chunks = TB // CHUNK

    def blocks(xv):
        return _head_blocks(xv, bdmask)

    def bd(xv):
        return _block_diag(xv, bdmask)

    def bdt(blks):
        return jnp.concatenate([blks[h] for h in T_HEAD_ORDER], axis=0)

    def rows_of(c):
        return slice(c * CHUNK, (c + 1) * CHUNK)

    def lanes_of(g):
        return slice(GROUP * g, GROUP * (g + 1))

    def segsum(xv):
        parts = []
        for half in range(N_GROUPS):
            parts.append(_dot(xv[:, GROUP * half:GROUP * (half + 1)].astype(BF16), segones_ref[...]))
        return jnp.concatenate(parts, axis=1)

    def make_stream(s):
        p_ref, sa_ref, sb_ref = p_all.at[s], sa_all.at[s], sb_all.at[s]
        at_ref, rt_ref, bt_ref, kt_ref = at_all.at[s], rt_all.at[s], bt_all.at[s], kt_all.at[s]
        bdc_ref, kdc_ref, v_ref, bon_ref = bdc_all.at[s], kdc_all.at[s], v_all.at[s], bon_all.at[s]
        mix_ref, hb_ref, h_ref = mix_all.at[s], hb_all.at[s], h_all.at[s]
        xs_ref, os_ref = x_ref.at[s], o_ref.at[s]
        shift = mod_ref[s, 0:1, :]
        scale = mod_ref[s, 1:2, :]
        gate = mod_ref[s, 2:3, :]
        st = {}

        def project(col, width):
            p_ref[PAD:PAD + TB, col:col + width] = _dot(hb_ref[...], win_ref[:, col:col + width])

        def lerp(col, width, row0=0, nrows=TB):
            ext = p_ref[PAD + row0 - 8:PAD + row0 + nrows, col:col + width]
            cur = ext[8:, :]
            prev = pltpu.roll(ext, 1, axis=0)[8:, :]
            return cur + (prev - cur) * mu_ref[0:1, col - SEG0:col - SEG0 + width]

        def f_norm():
            xb = xs_ref[...]
            ms = jnp.mean(xb * xb, axis=-1, keepdims=True)
            hmod = xb * lax.rsqrt(ms + NORM_EPS) * (ng_ref[...] * (1.0 + scale)) + shift
            hb_ref[...] = hmod.astype(BF16)
            project(COL_LO, 2 * LORA)
            project(COL_R, D_RWKV)

        def f_decay():
            lo = lerp(COL_LO, 2 * LORA)
            lane = lax.broadcasted_iota(jnp.int32, lo.shape, 1)
            lo = jnp.where(lane < LORA, jnp.tanh(lo), lo)
            lo_hi, lo_lo = _split_hi_lo(lo)
            lora = _diag2(wup_ref[...].astype(BF16), aup_ref[...].astype(BF16))
            lin = _dot(jnp.concatenate([lo_hi, lo_lo], axis=1), jnp.concatenate([lora, lora], axis=0))
            logw = -float(np.exp(-0.5) * np.log2(np.e)) * _sigmoid(w0_ref[...] + lin[:, 0:D_RWKV])
            st["a"] = _sigmoid(a0_ref[...] + lin[:, D_RWKV:2 * D_RWKV])
            lw_hi, lw_lo = _split_hi_lo(logw)
            cum = jnp.concatenate(
                [_dot(cmat_ref[...], lw_hi[rb:rb + CUM_ROWS, :]) + _dot(cmat_ref[...], lw_lo[rb:rb + CUM_ROWS, :])
                 for rb in range(0, TB, CUM_ROWS)], axis=0)
            cum_end = [cum[(c + 1) * CHUNK - 1:(c + 1) * CHUNK, :] for c in range(n_chunks)]
            st["p_chunk"] = [jnp.exp2(ce) for ce in cum_end]
            st["p_tot"] = jnp.concatenate(
                [jnp.broadcast_to(pc, (CHUNK, D_RWKV)) for pc in st["p_chunk"]], axis=0)
            st["cum"], st["logw"] = cum, logw
            project(COL_K, D_RWKV)

        def f_receptance():
            r = lerp(COL_R, D_RWKV)
            rt_ref[...] = r * jnp.exp2(st["cum"])
            bon_ref[...] = r * rk_ref[...]
            project(COL_V, D_RWKV)

        def f_key():
            a, cum = st["a"], st["cum"]
            k = lerp(COL_K, D_RWKV)
            kkr = k * kk_ref[...]
            ssq = segsum(kkr * kkr)
            kk = kkr * lax.rsqrt(jnp.maximum(ssq, L2_EPS * L2_EPS))
            k2 = k * (a * ka_ref[...] + (1.0 - ka_ref[...]))
            p_inv = jnp.exp2(-cum)
            bt = kk * a * p_inv
            kt = k2 * p_inv
            bt_ref[...] = bt
            kt_ref[...] = kt
            bdc_ref[...] = bt * st["p_tot"]
            kdc_ref[...] = kt * st["p_tot"]
            at_ref[...] = -kk * jnp.exp2(cum - st["logw"])
            st["bonus_rk"] = segsum(bon_ref[...] * k2)

        def f_value():
            v = lerp(COL_V, D_RWKV)
            v_ref[...] = v
            bon_ref[...] = st["bonus_rk"] * v

        front = [f_norm, f_decay, f_receptance, f_key, f_value]

        inst = [(c, g) for c in range(n_chunks) for g in range(N_GROUPS)]
        lab, lak, mrb, mrk = {}, {}, {}, {}
        tinv, lpow, lakv, mrkv, kv, xm = {}, {}, {}, {}, {}, {}
        zt, q, y0, a_mat, g_mat = {}, {}, {}, {}, {}

        def st_scores(i):
            rows, lanes = rows_of(i[0]), lanes_of(i[1])
            ar = jnp.concatenate([at_ref[rows, lanes], rt_ref[rows, lanes]], axis=0).astype(BF16)
            rhs1 = jnp.concatenate([bdt(blocks(_heads_to_rows_t(bt_ref[rows, lanes]))),
                                    bdt(blocks(_heads_to_rows_t(kt_ref[rows, lanes])))], axis=1)
            sc = _dot(ar, rhs1) * trimask_ref[...]
            lab[i] = sc[0:CHUNK, 0:GROUP]
            lak[i] = sc[0:CHUNK, GROUP:2 * GROUP]
            mrb[i] = sc[CHUNK:2 * CHUNK, 0:GROUP]
            mrk[i] = sc[CHUNK:2 * CHUNK, GROUP:2 * GROUP]

        def st_inv_start(i):
            rows, lanes = rows_of(i[0]), lanes_of(i[1])
            ldiag = lab[i] * tmask_ref[0]
            tinv[i] = eye + ldiag
            lpow[i] = _dot(ldiag.astype(BF16), bd(ldiag))
            kd_t = _heads_to_rows_t(kdc_ref[rows, lanes])
            xv = _dot(jnp.concatenate([lak[i], mrk[i], kd_t], axis=0).astype(BF16),
                      bdt(blocks(v_ref[rows, lanes])))
            lakv[i] = xv[0:CHUNK, :]
            mrkv[i] = xv[CHUNK:2 * CHUNK, :]
            kv[i] = xv[2 * CHUNK:3 * CHUNK, :]

        def st_neumann(i):
            res = _dot(jnp.concatenate([lpow[i], tinv[i]], axis=0).astype(BF16), bd(lpow[i]))
            lpow[i] = res[0:CHUNK, :]
            tinv[i] = tinv[i] + res[CHUNK:2 * CHUNK, :]

        def st_neumann_last(i):
            tinv[i] = tinv[i] + _dot(tinv[i].astype(BF16), bd(lpow[i]))

        def st_merge_x(level):
            def stage(i):
                xm[i] = _dot((lab[i] * tmask_ref[level]).astype(BF16), bd(tinv[i]))
            return stage

        def st_merge_t(i):
            tinv[i] = tinv[i] + _dot(tinv[i].astype(BF16), bd(xm[i]))

        def st_last_merge_x(i):
            rows, lanes = rows_of(i[0]), lanes_of(i[1])
            bd_t = _heads_to_rows_t(bdc_ref[rows, lanes])
            res = _dot(jnp.concatenate([lab[i] * tmask_ref[N_MERGES], mrb[i], bd_t], axis=0).astype(BF16),
                       bd(tinv[i]))
            xm[i] = res[0:CHUNK, :]
            zt[i] = res[CHUNK:3 * CHUNK, :]

        def st_last_merge_t(i):
            zt[i] = zt[i] + _dot(zt[i].astype(BF16), bd(xm[i]))

        def st_transition(i):
            rows, lanes = rows_of(i[0]), lanes_of(i[1])
            res = _dot(zt[i].astype(BF16),
                       jnp.concatenate([bdt(blocks(at_ref[rows, lanes])), bdt(blocks(lakv[i]))], axis=1))
            q[i] = rt_ref[rows, lanes] + res[0:CHUNK, 0:GROUP]
            y0[i] = res[0:CHUNK, GROUP:2 * GROUP] + mrkv[i]
            a_mat[i] = eye * st["p_chunk"][i[0]][:, lanes] + res[CHUNK:2 * CHUNK, 0:GROUP]
            g_mat[i] = res[CHUNK:2 * CHUNK, GROUP:2 * GROUP] + kv[i]

        stage_fns = ([st_scores, st_inv_start] + [st_neumann] * (INV_BLOCK.bit_length() - 3)
                     + [st_neumann_last])
        for level in range(1, N_MERGES):
            stage_fns += [st_merge_x(level), st_merge_t]
        stage_fns += [st_last_merge_x, st_last_merge_t, st_transition]

        def run_stage(fn):
            def thunk():
                for i in inst:
                    fn(i)
            return thunk

        chunk_stages = [run_stage(fn) for fn in stage_fns]

        y_chunks = {}

        def chain_step(c):
            def thunk():
                if c == 0:
                    st["hs"] = [h_ref[g] for g in range(N_GROUPS)]
                hs, ys = st["hs"], []
                for g in range(N_GROUPS):
                    i = (c, g)
                    res = _dot(jnp.concatenate([a_mat[i], q[i]], axis=0).astype(BF16), bd(hs[g]))
                    hs[g] = res[0:CHUNK, :] + g_mat[i]
                    ys.append(res[CHUNK:2 * CHUNK, :] + y0[i])
                y_chunks[c] = jnp.concatenate(ys, axis=1)
                if c == n_chunks - 1:
                    for g in range(N_GROUPS):
                        h_ref[g] = hs[g]
            return thunk

        diffs = []

        def pool_windows():
            n_ext = TB + PAD
            sa_ref[8:n_ext, :] = p_ref[8:n_ext, 0:512] + p_ref[7:n_ext - 1, 0:512]
            sb_ref[8:n_ext, 128:512] = sa_ref[8:n_ext, 128:512] + sa_ref[6:n_ext - 2, 128:512]
            sa_ref[8:n_ext, 256:512] = sb_ref[8:n_ext, 256:512] + sb_ref[4:n_ext - 4, 256:512]
            sb_ref[8:n_ext, 384:512] = sa_ref[8:n_ext, 384:512] + sa_ref[0:n_ext - 8, 384:512]

        def pool_diffs():
            pos = t * TB + lax.broadcasted_iota(jnp.int32, (TB, 1), 0) + 1
            wsum_refs = (sa_ref, sb_ref, sa_ref, sb_ref)
            for g, win in enumerate(POOL_WINDOWS):
                lanes = slice(g * POOL_GROUP, (g + 1) * POOL_GROUP)
                cnt = jnp.minimum(pos, win).astype(F32)
                mean = wsum_refs[g][PAD:PAD + TB, lanes] / cnt
                diffs.append(mean - p_ref[PAD:PAD + TB, lanes])

        def pool_out(half):
            lanes = slice(256 * half, 256 * half + 256)
            d2 = jnp.concatenate(diffs[2 * half:2 * half + 2], axis=1).astype(BF16)
            yp = _dot(d2, _diag2(pw_ref[2 * half].astype(BF16), pw_ref[2 * half + 1].astype(BF16)))
            zz = p_ref[PAD:PAD + TB, COL_PZ + 256 * half:COL_PZ + 256 * half + 256]
            yp = yp * pscale_ref[:, lanes] * (zz * _sigmoid(zz))
            mix_ref[:, lanes] = yp.astype(BF16)

        half_w = D_POOL // 2
        fillers = [
            lambda: project(COL_U, half_w),
            lambda: project(COL_U + half_w, half_w),
            lambda: (project(COL_PZ, half_w), pool_windows()),
            lambda: (project(COL_PZ + half_w, half_w), pool_diffs()),
            lambda: (project(COL_Z, half_w), pool_out(0)),
            lambda: (project(COL_Z + half_w, half_w), pool_out(1)),
        ]

        def finish():
            yh = jnp.concatenate([y_chunks[c] for c in range(n_chunks)], axis=0)
            mu = segsum(yh) * (1.0 / HEAD)
            dlt = yh - mu
            var = segsum(dlt * dlt) * (1.0 / HEAD)
            yn = dlt * lax.rsqrt(var + GN_EPS) * lnw_ref[...] + lnb_ref[...]
            z = lerp(COL_Z, D_RWKV)
            y_rwkv = (yn + bon_ref[...]) * (z * _sigmoid(z))
            mix_ref[:, D_POOL:D_POOL + D_RWKV] = y_rwkv.astype(BF16)
            out = _dot(mix_ref[...], wout_ref[...])
            xo = xs_ref[...] + gate * out
            ms2 = jnp.mean(xo * xo, axis=-1, keepdims=True)
            os_ref[...] = xo * lax.rsqrt(ms2 + NORM_EPS) * fg_ref[...]
            p_ref[8:PAD, :] = p_ref[TB + 8:TB + PAD, :]

        tail = _interleave([chain_step(c) for c in range(n_chunks)], fillers) + [finish]
        return front, chunk_stages, tail

    front0, stages0, tail0 = make_stream(0)
    front1, stages1, tail1 = make_stream(1)
    schedule = front0 + _interleave(stages0, front1) + _interleave(tail0, stages1) + tail1
    for thunk in schedule:
        thunk()


def _constants():
    hb = np.arange(GROUP) // HEAD
    bdmask = (hb[:, None] == hb[None, :]).astype(np.float32)
    ti = np.arange(CHUNK)[:, None]
    si = (np.arange(2 * GROUP) % HEAD)[None, :]
    trimask = np.concatenate([(ti > si), (ti >= si)], axis=0).astype(np.float32)
    sg = (np.arange(GROUP) % HEAD)[None, :]
    eye = (ti == sg).astype(np.float32)
    same = [(ti // (INV_BLOCK << m)) == (sg // (INV_BLOCK << m)) for m in range(N_MERGES + 1)]
    tmask = np.stack([same[0]] + [same[m] & ~same[m - 1] for m in range(1, N_MERGES + 1)],
                     axis=0).astype(np.float32)
    rr = np.arange(CUM_ROWS)
    same = (rr[:, None] // CHUNK) == (rr[None, :] // CHUNK)
    tril = same & (rr[None, :] <= rr[:, None])
    cmat = tril.astype(np.float32)
    return bdmask, trimask, eye, tmask, cmat


def kernel(x, c, w_ada, b_ada, norm_g, w_in, pool_w, pool_scale, mu_shift, w0, w_up, a0, a_up,
           k_k, k_a, r_k, ln_w, ln_b, w_out, final_g):
    B, T, _ = x.shape
    assert w_ada.shape[0] == 1 and T % TB == 0 and B == N_STREAMS
    l = 0

    mod = _ada_mod(c, w_ada, b_ada, l).reshape(B, 3, D_MODEL)

    w_in_p = _win_prep(w_in, l)
    mu_p = jnp.concatenate([mu_shift[l, a - _S0:b - _S0] for a, b in W_IN_SPANS[1:]], axis=0)[None, :]

    bdmask, trimask, eye, tmask, cmat = _constants()
    bdmask_b = jnp.asarray(bdmask, BF16)
    segones = bdmask_b
    trimask = jnp.asarray(trimask)
    eye = jnp.asarray(eye)
    tmask = jnp.asarray(tmask)
    cmat = jnp.asarray(cmat, BF16)

    def full(a):
        nd = a.ndim
        return pl.BlockSpec(a.shape, lambda t, _nd=nd: (0,) * _nd)

    def layer(a):
        a = a.reshape(a.shape[0], 1, a.shape[1]) if a.ndim == 2 else a
        nd = a.ndim
        return a, pl.BlockSpec((None,) + a.shape[1:], lambda t, _nd=nd: (l,) + (0,) * (_nd - 1))

    operands = [
        (x, pl.BlockSpec((B, TB, D_MODEL), lambda t: (0, t, 0))),
        (mod, full(mod)),
        layer(norm_g), (w_in_p, full(w_in_p)), (mu_p, full(mu_p)), layer(w_up), layer(a_up),
        layer(w0), layer(a0), layer(k_k), layer(k_a), layer(r_k), layer(ln_w), layer(ln_b),
        layer(pool_scale), layer(pool_w), layer(w_out.astype(BF16)),
        (final_g[None, :], full(final_g[None, :])),
    ] + [(a, full(a)) for a in (bdmask_b, trimask, eye, tmask, cmat, segones)]
    in_specs = [spec for _, spec in operands]

    blk = lambda: pltpu.VMEM((N_STREAMS, TB, D_RWKV), F32)
    scratch = [
        pltpu.VMEM((N_STREAMS, TB + PAD, N_IN), F32),
        pltpu.VMEM((N_STREAMS, TB + PAD, D_POOL), F32),
        pltpu.VMEM((N_STREAMS, TB + PAD, D_POOL), F32),
        blk(), blk(), blk(), blk(), blk(), blk(),
        blk(), blk(),
        pltpu.VMEM((N_STREAMS, TB, D_MODEL), BF16),
        pltpu.VMEM((N_STREAMS, TB, D_MODEL), BF16),
        pltpu.VMEM((N_STREAMS, N_GROUPS, HEAD, GROUP), F32),
    ]
    return pl.pallas_call(
        _fused_kernel,
        grid=(T // TB,),
        in_specs=in_specs,
        out_specs=pl.BlockSpec((B, TB, D_MODEL), lambda t: (0, t, 0)),
        out_shape=jax.ShapeDtypeStruct((B, T, D_MODEL), x.dtype),
        scratch_shapes=scratch,
        compiler_params=pltpu.CompilerParams(
            dimension_semantics=("arbitrary",),
            vmem_limit_bytes=VMEM_LIMIT_BYTES),
        name="hybrid_block",
    )(*[a for a, _ in operands])
```

```python
import numpy as np
import jax
import jax.numpy as jnp
from jax import lax
from jax.experimental import pallas as pl
from jax.experimental.pallas import tpu as pltpu

F32 = jnp.float32
BF16 = jnp.bfloat16

D_MODEL = 1024
D_POOL = 512
D_RWKV = 512
HEAD = 64
LORA = 64
POOL_WINDOWS = (2, 4, 8, 16)
POOL_GROUP = 128
NORM_EPS = 1e-6
GN_EPS = 64e-5
L2_EPS = 1e-12

COL_U = 0
COL_PZ = 512
COL_R = 1024
COL_K = 1536
COL_V = 2048
COL_Z = 2560
COL_LO = 3072
N_IN = 3200
SEG0 = 1024

CHUNK = 64
GROUP = 256
HEADS_PER_GROUP = GROUP // HEAD
N_GROUPS = D_RWKV // GROUP
TB = 512
CUM_ROWS = 256
INV_BLOCK = 32
N_MERGES = (CHUNK // INV_BLOCK).bit_length() - 1
PAD = 24

VMEM_LIMIT_BYTES = 56 * 1024 * 1024


def _split_hi_lo(x):
    hi = x.astype(BF16)
    lo = (x - hi.astype(F32)).astype(BF16)
    return hi, lo


def _dot(a, b):
    return jnp.dot(a, b, preferred_element_type=F32)


def _sigmoid(x):
    return 0.5 * jnp.tanh(0.5 * x) + 0.5


ADA_ROWS = 256
LANES = 128


def _ada_kernel(cb_ref, w_ref, b_ref, o_ref):
    @pl.when(pl.program_id(0) == 0)
    def _():
        o_ref[...] = jnp.broadcast_to(b_ref[...], o_ref.shape)

    w = w_ref[...]
    rows = []
    for b in range(cb_ref.shape[0]):
        cb = cb_ref[b]
        cols = [jnp.sum(w[:, j:j + LANES] * cb, axis=0, keepdims=True)
                for j in range(0, w.shape[1], LANES)]
        rows.append(jnp.concatenate(cols, axis=1))
    o_ref[...] += jnp.concatenate(rows, axis=0)


def _ada_mod(c, w_ada, b_ada, l):
    nb, n = c.shape[0], w_ada.shape[2]
    cb = jnp.broadcast_to(c[:, :, None], (nb, D_MODEL, LANES))
    return pl.pallas_call(
        _ada_kernel,
        grid=(D_MODEL // ADA_ROWS,),
        in_specs=[
            pl.BlockSpec((nb, ADA_ROWS, LANES), lambda j: (0, j, 0)),
            pl.BlockSpec((None, ADA_ROWS, n), lambda j: (l, j, 0)),
            pl.BlockSpec((None, 1, n), lambda j: (l, 0, 0)),
        ],
        out_specs=pl.BlockSpec((nb, n), lambda j: (0, 0)),
        out_shape=jax.ShapeDtypeStruct((nb, n), F32),
        name="adaln_mod",
    )(cb, w_ada, b_ada.reshape(b_ada.shape[0], 1, n))


_S0 = 2 * D_POOL
W_IN_SPANS = ((0, _S0), (_S0, _S0 + 512), (_S0 + 576, _S0 + 1088), (_S0 + 1088, _S0 + 1600),
              (_S0 + 1664, _S0 + 2176), (_S0 + 512, _S0 + 576), (_S0 + 1600, _S0 + 1664))
WPREP_ROWS = 256


def _win_prep_kernel(w_ref, o_ref):
    w = w_ref[...]
    o_ref[...] = jnp.concatenate([w[:, a:b] for a, b in W_IN_SPANS], axis=1).astype(BF16)


def _win_prep(w_in, l):
    d, n = w_in.shape[1], w_in.shape[2]
    return pl.pallas_call(
        _win_prep_kernel,
        grid=(d // WPREP_ROWS,),
        in_specs=[pl.BlockSpec((None, WPREP_ROWS, n), lambda i: (l, i, 0))],
        out_specs=pl.BlockSpec((WPREP_ROWS, n), lambda i: (i, 0)),
        out_shape=jax.ShapeDtypeStruct((d, n), BF16),
        name="w_in_prep",
    )(w_in)


def _head_blocks(x, bdmask):
    xb = x.astype(BF16)
    heads_per_tile = LANES // HEAD
    zero_tile = jnp.zeros((x.shape[0], LANES), BF16)
    out = []
    for h in range(HEADS_PER_GROUP):
        tiles = [xb[:, j * LANES:(j + 1) * LANES] * bdmask[h * HEAD:(h + 1) * HEAD, j * LANES:(j + 1) * LANES]
                 if j == h // heads_per_tile else zero_tile
                 for j in range(GROUP // LANES)]
        out.append(jnp.concatenate(tiles, axis=1))
    return out


def _block_diag(x, bdmask):
    return jnp.concatenate(_head_blocks(x, bdmask), axis=0)


T_HEAD_ORDER = (0, 2, 1, 3)


def _heads_to_rows_t(x):
    xt = jnp.concatenate([x[:, 0:2 * HEAD], x[:, 2 * HEAD:4 * HEAD]], axis=0).T
    return jnp.concatenate([xt[0:HEAD, :], xt[HEAD:2 * HEAD, :]], axis=1)


def _diag2(m0, m1):
    z = jnp.zeros_like(m0)
    return jnp.concatenate([jnp.concatenate([m0, z], axis=1),
                            jnp.concatenate([z, m1], axis=1)], axis=0)


def _fused_kernel(x_ref, mod_ref, ng_ref, win_ref, mu_ref, wup_ref, aup_ref,
                  w0_ref, a0_ref, kk_ref, ka_ref, rk_ref, lnw_ref, lnb_ref, pscale_ref,
                  pw_ref, wout_ref, fg_ref, bdmask_ref, trimask_ref, eye_ref, tmask_ref, cmat_ref,
                  segones_ref, o_ref,
                  p_ref, sa_ref, sb_ref, at_ref, rt_ref, bt_ref, kt_ref, bdc_ref, kdc_ref,
                  v_ref, bon_ref, mix_ref, hb_ref, h_ref):
    t = pl.program_id(1)

    @pl.when(t == 0)
    def _():
        p_ref[0:PAD, :] = jnp.zeros((PAD, N_IN), F32)
        sa_ref[0:8, :] = jnp.zeros((8, D_POOL), F32)
        sb_ref[0:8, :] = jnp.zeros((8, D_POOL), F32)
        h_ref[...] = jnp.zeros(h_ref.shape, F32)

    bdmask = bdmask_ref[...]
    eye = eye_ref[...]
    n_chunks = TB // CHUNK

    def blocks(xv):
        return _head_blocks(xv, bdmask)

    def bd(xv):
        return _block_diag(xv, bdmask)

    def bdt(blks):
        return jnp.concatenate([blks[h] for h in T_HEAD_ORDER], axis=0)

    def rows_of(c):
        return slice(c * CHUNK, (c + 1) * CHUNK)

    def lanes_of(g):
        return slice(GROUP * g, GROUP * (g + 1))

    def segsum(xv):
        parts = []
        for half in range(N_GROUPS):
            parts.append(_dot(xv[:, GROUP * half:GROUP * (half + 1)].astype(BF16), segones_ref[...]))
        return jnp.concatenate(parts, axis=1)

    def build():
        xs_ref, os_ref = x_ref, o_ref
        shift = mod_ref[0:1, :]
        scale = mod_ref[1:2, :]
        gate = mod_ref[2:3, :]
        st = {}

        def project(col, width):
            p_ref[PAD:PAD + TB, col:col + width] = _dot(hb_ref[...], win_ref[:, col:col + width])

        def lerp(col, width, row0=0, nrows=TB):
            ext = p_ref[PAD + row0 - 8:PAD + row0 + nrows, col:col + width]
            cur = ext[8:, :]
            prev = pltpu.roll(ext, 1, axis=0)[8:, :]
            return cur + (prev - cur) * mu_ref[0:1, col - SEG0:col - SEG0 + width]

        def f_norm():
            xb = xs_ref[...]
            ms = jnp.mean(xb * xb, axis=-1, keepdims=True)
            hmod = xb * lax.rsqrt(ms + NORM_EPS) * (ng_ref[...] * (1.0 + scale)) + shift
            hb_ref[...] = hmod.astype(BF16)
            project(COL_LO, 2 * LORA)
            project(COL_R, D_RWKV)

        def f_decay():
            lo = lerp(COL_LO, 2 * LORA)
            lane = lax.broadcasted_iota(jnp.int32, lo.shape, 1)
            lo = jnp.where(lane < LORA, jnp.tanh(lo), lo)
            lo_hi, lo_lo = _split_hi_lo(lo)
            lora = _diag2(wup_ref[...].astype(BF16), aup_ref[...].astype(BF16))
            lin = _dot(jnp.concatenate([lo_hi, lo_lo], axis=1), jnp.concatenate([lora, lora], axis=0))
            logw = -float(np.exp(-0.5) * np.log2(np.e)) * _sigmoid(w0_ref[...] + lin[:, 0:D_RWKV])
            st["a"] = _sigmoid(a0_ref[...] + lin[:, D_RWKV:2 * D_RWKV])
            lw_hi, lw_lo = _split_hi_lo(logw)
            cum = jnp.concatenate(
                [_dot(cmat_ref[...], lw_hi[rb:rb + CUM_ROWS, :]) + _dot(cmat_ref[...], lw_lo[rb:rb + CUM_ROWS, :])
                 for rb in range(0, TB, CUM_ROWS)], axis=0)
            cum_end = [cum[(c + 1) * CHUNK - 1:(c + 1) * CHUNK, :] for c in range(n_chunks)]
            st["p_chunk"] = [jnp.exp2(ce) for ce in cum_end]
            st["p_tot"] = jnp.concatenate(
                [jnp.broadcast_to(pc, (CHUNK, D_RWKV)) for pc in st["p_chunk"]], axis=0)
            st["cum"], st["logw"] = cum, logw
            project(COL_K, D_RWKV)

        def f_receptance():
            r = lerp(COL_R, D_RWKV)
            rt_ref[...] = r * jnp.exp2(st["cum"])
            bon_ref[...] = r * rk_ref[...]
            project(COL_V, D_RWKV)

        def f_key():
            a, cum = st["a"], st["cum"]
            k = lerp(COL_K, D_RWKV)
            kkr = k * kk_ref[...]
            ssq = segsum(kkr * kkr)
            kk = kkr * lax.rsqrt(jnp.maximum(ssq, L2_EPS * L2_EPS))
            k2 = k * (a * ka_ref[...] + (1.0 - ka_ref[...]))
            p_inv = jnp.exp2(-cum)
            bt = kk * a * p_inv
            kt = k2 * p_inv
            bt_ref[...] = bt
            kt_ref[...] = kt
            bdc_ref[...] = bt * st["p_tot"]
            kdc_ref[...] = kt * st["p_tot"]
            at_ref[...] = -kk * jnp.exp2(cum - st["logw"])
            bon_ref[...] = segsum(bon_ref[...] * k2) * v_ref[...]

        def f_value():
            v_ref[...] = lerp(COL_V, D_RWKV)

        front = [f_norm, f_decay, f_receptance, f_value, f_key]

        inst = [(c, g) for c in range(n_chunks) for g in range(N_GROUPS)]
        lab, lak, mrb, mrk = {}, {}, {}, {}
        tinv, lpow, lakv, mrkv, kv, xm = {}, {}, {}, {}, {}, {}
        zt, q, y0, a_mat, g_mat = {}, {}, {}, {}, {}

        def st_scores(i):
            rows, lanes = rows_of(i[0]), lanes_of(i[1])
            ar = jnp.concatenate([at_ref[rows, lanes], rt_ref[rows, lanes]], axis=0).astype(BF16)
            rhs1 = jnp.concatenate([bdt(blocks(_heads_to_rows_t(bt_ref[rows, lanes]))),
                                    bdt(blocks(_heads_to_rows_t(kt_ref[rows, lanes])))], axis=1)
            sc = _dot(ar, rhs1) * trimask_ref[...]
            lab[i] = sc[0:CHUNK, 0:GROUP]
            lak[i] = sc[0:CHUNK, GROUP:2 * GROUP]
            mrb[i] = sc[CHUNK:2 * CHUNK, 0:GROUP]
            mrk[i] = sc[CHUNK:2 * CHUNK, GROUP:2 * GROUP]

        def st_inv_start(i):
            rows, lanes = rows_of(i[0]), lanes_of(i[1])
            ldiag = lab[i] * tmask_ref[0]
            tinv[i] = eye + ldiag
            lpow[i] = _dot(ldiag.astype(BF16), bd(ldiag))
            kd_t = _heads_to_rows_t(kdc_ref[rows, lanes])
            xv = _dot(jnp.concatenate([lak[i], mrk[i], kd_t], axis=0).astype(BF16),
                      bdt(blocks(v_ref[rows, lanes])))
            lakv[i] = xv[0:CHUNK, :]
            mrkv[i] = xv[CHUNK:2 * CHUNK, :]
            kv[i] = xv[2 * CHUNK:3 * CHUNK, :]

        def st_neumann(i):
            res = _dot(jnp.concatenate([lpow[i], tinv[i]], axis=0).astype(BF16), bd(lpow[i]))
            lpow[i] = res[0:CHUNK, :]
            tinv[i] = tinv[i] + res[CHUNK:2 * CHUNK, :]

        def st_neumann_last(i):
            tinv[i] = tinv[i] + _dot(tinv[i].astype(BF16), bd(lpow[i]))

        def st_merge_x(level):
            def stage(i):
                xm[i] = _dot((lab[i] * tmask_ref[level]).astype(BF16), bd(tinv[i]))
            return stage

        def st_merge_t(i):
            tinv[i] = tinv[i] + _dot(tinv[i].astype(BF16), bd(xm[i]))

        def st_last_merge_x(i):
            rows, lanes = rows_of(i[0]), lanes_of(i[1])
            bd_t = _heads_to_rows_t(bdc_ref[rows, lanes])
            res = _dot(jnp.concatenate([lab[i] * tmask_ref[N_MERGES], mrb[i], bd_t], axis=0).astype(BF16),
                       bd(tinv[i]))
            xm[i] = res[0:CHUNK, :]
            zt[i] = res[CHUNK:3 * CHUNK, :]

        def st_last_merge_t(i):
            zt[i] = zt[i] + _dot(zt[i].astype(BF16), bd(xm[i]))

        def st_transition(i):
            rows, lanes = rows_of(i[0]), lanes_of(i[1])
            res = _dot(zt[i].astype(BF16),
                       jnp.concatenate([bdt(blocks(at_ref[rows, lanes])), bdt(blocks(lakv[i]))], axis=1))
            q[i] = rt_ref[rows, lanes] + res[0:CHUNK, 0:GROUP]
            y0[i] = res[0:CHUNK, GROUP:2 * GROUP] + mrkv[i]
            a_mat[i] = eye * st["p_chunk"][i[0]][:, lanes] + res[CHUNK:2 * CHUNK, 0:GROUP]
            g_mat[i] = res[CHUNK:2 * CHUNK, GROUP:2 * GROUP] + kv[i]

        stage_fns = ([st_scores, st_inv_start] + [st_neumann] * (INV_BLOCK.bit_length() - 3)
                     + [st_neumann_last])
        for level in range(1, N_MERGES):
            stage_fns += [st_merge_x(level), st_merge_t]
        stage_fns += [st_last_merge_x, st_last_merge_t, st_transition]

        def run_stage(fn):
            def thunk():
                for i in inst:
                    fn(i)
            return thunk

        chunk_stages = [run_stage(fn) for fn in stage_fns]

        y_chunks = {}

        def chain_step(c):
            def thunk():
                if c == 0:
                    st["hs"] = [h_ref[g] for g in range(N_GROUPS)]
                hs, ys = st["hs"], []
                for g in range(N_GROUPS):
                    i = (c, g)
                    res = _dot(jnp.concatenate([a_mat[i], q[i]], axis=0).astype(BF16), bd(hs[g]))
                    hs[g] = res[0:CHUNK, :] + g_mat[i]
                    ys.append(res[CHUNK:2 * CHUNK, :] + y0[i])
                y_chunks[c] = jnp.concatenate(ys, axis=1)
                if c == n_chunks - 1:
                    for g in range(N_GROUPS):
                        h_ref[g] = hs[g]
            return thunk

        diffs = []

        def pool_windows():
            n_ext = TB + PAD
            sa_ref[8:n_ext, :] = p_ref[8:n_ext, 0:512] + p_ref[7:n_ext - 1, 0:512]
            sb_ref[8:n_ext, 128:512] = sa_ref[8:n_ext, 128:512] + sa_ref[6:n_ext - 2, 128:512]
            sa_ref[8:n_ext, 256:512] = sb_ref[8:n_ext, 256:512] + sb_ref[4:n_ext - 4, 256:512]
            sb_ref[8:n_ext, 384:512] = sa_ref[8:n_ext, 384:512] + sa_ref[0:n_ext - 8, 384:512]

        def pool_diffs():
            pos = t * TB + lax.broadcasted_iota(jnp.int32, (TB, 1), 0) + 1
            wsum_refs = (sa_ref, sb_ref, sa_ref, sb_ref)
            for g, win in enumerate(POOL_WINDOWS):
                lanes = slice(g * POOL_GROUP, (g + 1) * POOL_GROUP)
                cnt = jnp.minimum(pos, win).astype(F32)
                mean = wsum_refs[g][PAD:PAD + TB, lanes] / cnt
                diffs.append(mean - p_ref[PAD:PAD + TB, lanes])

        def pool_out(half):
            lanes = slice(256 * half, 256 * half + 256)
            d2 = jnp.concatenate(diffs[2 * half:2 * half + 2], axis=1).astype(BF16)
            yp = _dot(d2, _diag2(pw_ref[2 * half].astype(BF16), pw_ref[2 * half + 1].astype(BF16)))
            zz = p_ref[PAD:PAD + TB, COL_PZ + 256 * half:COL_PZ + 256 * half + 256]
            yp = yp * pscale_ref[:, lanes] * (zz * _sigmoid(zz))
            mix_ref[:, lanes] = yp.astype(BF16)

        half_w = D_POOL // 2
        fillers = [
            lambda: project(COL_U, half_w),
            lambda: project(COL_U + half_w, half_w),
            lambda: (project(COL_PZ, half_w), pool_windows()),
            lambda: (project(COL_PZ + half_w, half_w), pool_diffs()),
            lambda: (project(COL_Z, half_w), pool_out(0)),
            lambda: (project(COL_Z + half_w, half_w), pool_out(1)),
        ]

        def finish():
            yh = jnp.concatenate([y_chunks[c] for c in range(n_chunks)], axis=0)
            mu = segsum(yh) * (1.0 / HEAD)
            dlt = yh - mu
            var = segsum(dlt * dlt) * (1.0 / HEAD)
            yn = dlt * lax.rsqrt(var + GN_EPS) * lnw_ref[...] + lnb_ref[...]
            z = lerp(COL_Z, D_RWKV)
            y_rwkv = (yn + bon_ref[...]) * (z * _sigmoid(z))
            mix_ref[:, D_POOL:D_POOL + D_RWKV] = y_rwkv.astype(BF16)
            out = _dot(mix_ref[...], wout_ref[...])
            xo = xs_ref[...] + gate * out
            ms2 = jnp.mean(xo * xo, axis=-1, keepdims=True)
            os_ref[...] = xo * lax.rsqrt(ms2 + NORM_EPS) * fg_ref[...]
            p_ref[8:PAD, :] = p_ref[TB + 8:TB + PAD, :]

        tail = []
        for c in range(n_chunks):
            tail.append(chain_step(c))
            tail += fillers[c:c + 1]
        tail += fillers[n_chunks:] + [finish]
        return front, chunk_stages, tail

    front, chunk_stages, tail = build()
    for thunk in front + chunk_stages + tail:
        thunk()


def _constants():
    hb = np.arange(GROUP) // HEAD
    bdmask = (hb[:, None] == hb[None, :]).astype(np.float32)
    ti = np.arange(CHUNK)[:, None]
    si = (np.arange(2 * GROUP) % HEAD)[None, :]
    trimask = np.concatenate([(ti > si), (ti >= si)], axis=0).astype(np.float32)
    sg = (np.arange(GROUP) % HEAD)[None, :]
    eye = (ti == sg).astype(np.float32)
    same = [(ti // (INV_BLOCK << m)) == (sg // (INV_BLOCK << m)) for m in range(N_MERGES + 1)]
    tmask = np.stack([same[0]] + [same[m] & ~same[m - 1] for m in range(1, N_MERGES + 1)],
                     axis=0).astype(np.float32)
    rr = np.arange(CUM_ROWS)
    same = (rr[:, None] // CHUNK) == (rr[None, :] // CHUNK)
    tril = same & (rr[None, :] <= rr[:, None])
    cmat = tril.astype(np.float32)
    return bdmask, trimask, eye, tmask, cmat


def kernel(x, c, w_ada, b_ada, norm_g, w_in, pool_w, pool_scale, mu_shift, w0, w_up, a0, a_up,
           k_k, k_a, r_k, ln_w, ln_b, w_out, final_g):
    B, T, _ = x.shape
    assert w_ada.shape[0] == 1 and T % TB == 0
    l = 0

    mod = _ada_mod(c, w_ada, b_ada, l).reshape(B, 3, D_MODEL)

    w_in_p = _win_prep(w_in, l)
    mu_p = jnp.concatenate([mu_shift[l, a - _S0:b - _S0] for a, b in W_IN_SPANS[1:]], axis=0)[None, :]

    bdmask, trimask, eye, tmask, cmat = _constants()
    bdmask_b = jnp.asarray(bdmask, BF16)
    segones = bdmask_b
    trimask = jnp.asarray(trimask)
    eye = jnp.asarray(eye)
    tmask = jnp.asarray(tmask)
    cmat = jnp.asarray(cmat, BF16)

    def full(a):
        nd = a.ndim
        return pl.BlockSpec(a.shape, lambda b, t, _nd=nd: (0,) * _nd)

    def layer(a):
        a = a.reshape(a.shape[0], 1, a.shape[1]) if a.ndim == 2 else a
        nd = a.ndim
        return a, pl.BlockSpec((None,) + a.shape[1:], lambda b, t, _nd=nd: (l,) + (0,) * (_nd - 1))

    operands = [
        (x, pl.BlockSpec((None, TB, D_MODEL), lambda b, t: (b, t, 0))),
        (mod, pl.BlockSpec((None, 3, D_MODEL), lambda b, t: (b, 0, 0))),
        layer(norm_g), (w_in_p, full(w_in_p)), (mu_p, full(mu_p)), layer(w_up), layer(a_up),
        layer(w0), layer(a0), layer(k_k), layer(k_a), layer(r_k), layer(ln_w), layer(ln_b),
        layer(pool_scale), layer(pool_w), layer(w_out.astype(BF16)),
        (final_g[None, :], full(final_g[None, :])),
    ] + [(a, full(a)) for a in (bdmask_b, trimask, eye, tmask, cmat, segones)]
    in_specs = [spec for _, spec in operands]

    blk = lambda: pltpu.VMEM((TB, D_RWKV), F32)
    scratch = [
        pltpu.VMEM((TB + PAD, N_IN), F32),
        pltpu.VMEM((TB + PAD, D_POOL), F32),
        pltpu.VMEM((TB + PAD, D_POOL), F32),
        blk(), blk(), blk(), blk(), blk(), blk(),
        blk(), blk(),
        pltpu.VMEM((TB, D_MODEL), BF16),
        pltpu.VMEM((TB, D_MODEL), BF16),
        pltpu.VMEM((N_GROUPS, HEAD, GROUP), F32),
    ]
    return pl.pallas_call(
        _fused_kernel,
        grid=(B, T // TB),
        in_specs=in_specs,
        out_specs=pl.BlockSpec((None, TB, D_MODEL), lambda b, t: (b, t, 0)),
        out_shape=jax.ShapeDtypeStruct((B, T, D_MODEL), x.dtype),
        scratch_shapes=scratch,
        compiler_params=pltpu.CompilerParams(
            dimension_semantics=("arbitrary", "arbitrary"),
            vmem_limit_bytes=VMEM_LIMIT_BYTES),
        name="hybrid_block",
    )(*[a for a, _ in operands])
```

```python
import numpy as np
import jax
import jax.numpy as jnp
from jax import lax
from jax.experimental import pallas as pl
from jax.experimental.pallas import tpu as pltpu

F32 = jnp.float32
BF16 = jnp.bfloat16

D_MODEL = 1024
D_POOL = 512
D_RWKV = 512
HEAD = 64
LORA = 64
POOL_WINDOWS = (2, 4, 8, 16)
POOL_GROUP = 128
NORM_EPS = 1e-6
GN_EPS = 64e-5
L2_EPS = 1e-12

COL_U = 0
COL_PZ = 512
COL_R = 1024
COL_K = 1536
COL_V = 2048
COL_Z = 2560
COL_LO = 3072
N_IN = 3200
SEG0 = 1024

CHUNK = 64
GROUP = 256
HEADS_PER_GROUP = GROUP // HEAD
N_GROUPS = D_RWKV // GROUP
TB = 512
CUM_ROWS = 256
INV_BLOCK = 32
N_MERGES = (CHUNK // INV_BLOCK).bit_length() - 1
PAD = 24

VMEM_LIMIT_BYTES = 56 * 1024 * 1024


def _split_hi_lo(x):
    hi = x.astype(BF16)
    lo = (x - hi.astype(F32)).astype(BF16)
    return hi, lo


def _dot(a, b):
    return jnp.dot(a, b, preferred_element_type=F32)


def _sigmoid(x):
    return 0.5 * jnp.tanh(0.5 * x) + 0.5


ADA_ROWS = 256
LANES = 128


def _ada_kernel(cb_ref, w_ref, b_ref, o_ref):
    @pl.when(pl.program_id(0) == 0)
    def _():
        o_ref[...] = jnp.broadcast_to(b_ref[...], o_ref.shape)

    w = w_ref[...]
    rows = []
    for b in range(cb_ref.shape[0]):
        cb = cb_ref[b]
        cols = [jnp.sum(w[:, j:j + LANES] * cb, axis=0, keepdims=True)
                for j in range(0, w.shape[1], LANES)]
        rows.append(jnp.concatenate(cols, axis=1))
    o_ref[...] += jnp.concatenate(rows, axis=0)


def _ada_mod(c, w_ada, b_ada, l):
    nb, n = c.shape[0], w_ada.shape[2]
    cb = jnp.broadcast_to(c[:, :, None], (nb, D_MODEL, LANES))
    return pl.pallas_call(
        _ada_kernel,
        grid=(D_MODEL // ADA_ROWS,),
        in_specs=[
            pl.BlockSpec((nb, ADA_ROWS, LANES), lambda j: (0, j, 0)),
            pl.BlockSpec((None, ADA_ROWS, n), lambda j: (l, j, 0)),
            pl.BlockSpec((None, 1, n), lambda j: (l, 0, 0)),
        ],
        out_specs=pl.BlockSpec((nb, n), lambda j: (0, 0)),
        out_shape=jax.ShapeDtypeStruct((nb, n), F32),
        name="adaln_mod",
    )(cb, w_ada, b_ada.reshape(b_ada.shape[0], 1, n))


_S0 = 2 * D_POOL
W_IN_SPANS = ((0, _S0), (_S0, _S0 + 512), (_S0 + 576, _S0 + 1088), (_S0 + 1088, _S0 + 1600),
              (_S0 + 1664, _S0 + 2176), (_S0 + 512, _S0 + 576), (_S0 + 1600, _S0 + 1664))
WPREP_ROWS = 128


def _win_prep_kernel(w_ref, wo_ref, o_ref, oo_ref):
    w = w_ref[...]
    o_ref[...] = jnp.concatenate([w[:, a:b] for a, b in W_IN_SPANS], axis=1).astype(BF16)
    oo_ref[...] = wo_ref[...].astype(BF16)


def _win_prep(w_in, w_out, l):
    d, n = w_in.shape[1], w_in.shape[2]
    do, no = w_out.shape[1], w_out.shape[2]
    steps = d // WPREP_ROWS
    return pl.pallas_call(
        _win_prep_kernel,
        grid=(steps,),
        in_specs=[pl.BlockSpec((None, WPREP_ROWS, n), lambda i: (l, i, 0)),
                  pl.BlockSpec((None, do // steps, no), lambda i: (l, i, 0))],
        out_specs=[pl.BlockSpec((WPREP_ROWS, n), lambda i: (i, 0)),
                   pl.BlockSpec((do // steps, no), lambda i: (i, 0))],
        out_shape=[jax.ShapeDtypeStruct((d, n), BF16), jax.ShapeDtypeStruct((do, no), BF16)],
        name="w_in_prep",
    )(w_in, w_out)


def _head_blocks(x, bdmask):
    xb = x.astype(BF16)
    heads_per_tile = LANES // HEAD
    zero_tile = jnp.zeros((x.shape[0], LANES), BF16)
    out = []
    for h in range(HEADS_PER_GROUP):
        tiles = [xb[:, j * LANES:(j + 1) * LANES] * bdmask[h * HEAD:(h + 1) * HEAD, j * LANES:(j + 1) * LANES]
                 if j == h // heads_per_tile else zero_tile
                 for j in range(GROUP // LANES)]
        out.append(jnp.concatenate(tiles, axis=1))
    return out


def _block_diag(x, bdmask):
    return jnp.concatenate(_head_blocks(x, bdmask), axis=0)


T_HEAD_ORDER = (0, 2, 1, 3)


def _heads_to_rows_t(x):
    xt = jnp.concatenate([x[:, 0:2 * HEAD], x[:, 2 * HEAD:4 * HEAD]], axis=0).T
    return jnp.concatenate([xt[0:HEAD, :], xt[HEAD:2 * HEAD, :]], axis=1)


def _diag2(m0, m1):
    z = jnp.zeros_like(m0)
    return jnp.concatenate([jnp.concatenate([m0, z], axis=1),
                            jnp.concatenate([z, m1], axis=1)], axis=0)


def _fused_kernel(x_ref, mod_ref, ng_ref, win_ref, mu_ref, wup_ref, aup_ref,
                  w0_ref, a0_ref, kk_ref, ka_ref, rk_ref, lnw_ref, lnb_ref, pscale_ref,
                  pw_ref, wout_ref, fg_ref, bdmask_ref, trimask_ref, eye_ref, tmask_ref, cmat_ref,
                  segones_ref, o_ref,
                  p_ref, sa_ref, sb_ref, at_ref, rt_ref, bt_ref, kt_ref, bdc_ref, kdc_ref,
                  v_ref, bon_ref, mix_ref, hb_ref, h_ref):
    t = pl.program_id(1)

    def normalise_into_hb(xin_ref):
        xb = xin_ref[...]
        ms = jnp.mean(xb * xb, axis=-1, keepdims=True)
        gs = ng_ref[...] * (1.0 + mod_ref[1:2, :])
        hb_ref[...] = (xb * lax.rsqrt(ms + NORM_EPS) * gs + mod_ref[0:1, :]).astype(BF16)

    @pl.when(t == 0)
    def _():
        p_ref[0:PAD, :] = jnp.zeros((PAD, N_IN), F32)
        sa_ref[0:8, :] = jnp.zeros((8, D_POOL), F32)
        sb_ref[0:8, :] = jnp.zeros((8, D_POOL), F32)
        h_ref[...] = jnp.zeros(h_ref.shape, F32)

    bdmask = bdmask_ref[...]
    eye = eye_ref[...]
    n_chunks = TB // CHUNK

    def blocks(xv):
        return _head_blocks(xv, bdmask)

    def bd(xv):
        return _block_diag(xv, bdmask)

    def bdt(blks):
        return jnp.concatenate([blks[h] for h in T_HEAD_ORDER], axis=0)

    def rows_of(c):
        return slice(c * CHUNK, (c + 1) * CHUNK)

    def lanes_of(g):
        return slice(GROUP * g, GROUP * (g + 1))

    def segsum(xb):
        parts = []
        for half in range(N_GROUPS):
            parts.append(_dot(xb[:, GROUP * half:GROUP * (half + 1)], segones_ref[...]))
        return jnp.concatenate(parts, axis=1)

    def build():
        xs_ref, os_ref = x_ref, o_ref
        gate = mod_ref[2:3, :]
        st = {}

        def project(col, width):
            p_ref[PAD:PAD + TB, col:col + width] = _dot(hb_ref[...], win_ref[:, col:col + width])

        def lerp(col, width, row0=0, nrows=TB):
            ext = p_ref[PAD + row0 - 8:PAD + row0 + nrows, col:col + width]
            cur = ext[8:, :]
            prev = pltpu.roll(ext, 1, axis=0)[8:, :]
            return cur + (prev - cur) * mu_ref[0:1, col - SEG0:col - SEG0 + width]

        def f_norm():
            normalise_into_hb(x_ref)
            project(COL_LO, 2 * LORA)
            project(COL_R, D_RWKV)

        def f_decay():
            lo = lerp(COL_LO, 2 * LORA)
            lane = lax.broadcasted_iota(jnp.int32, lo.shape, 1)
            lo = jnp.where(lane < LORA, jnp.tanh(lo), lo)
            lo_hi, lo_lo = _split_hi_lo(lo)
            lora = _diag2(wup_ref[...].astype(BF16), aup_ref[...].astype(BF16))
            lin = _dot(jnp.concatenate([lo_hi, lo_lo], axis=1), jnp.concatenate([lora, lora], axis=0))
            logw = -float(np.exp(-0.5) * np.log2(np.e)) * _sigmoid(w0_ref[...] + lin[:, 0:D_RWKV])
            st["a"] = _sigmoid(a0_ref[...] + lin[:, D_RWKV:2 * D_RWKV])
            lw_b = logw.astype(BF16)
            logw = lw_b.astype(F32)
            cum = jnp.concatenate(
                [_dot(cmat_ref[...], lw_b[rb:rb + CUM_ROWS, :]) for rb in range(0, TB, CUM_ROWS)], axis=0)
            cum_end = [cum[(c + 1) * CHUNK - 1:(c + 1) * CHUNK, :] for c in range(n_chunks)]
            st["p_chunk"] = [jnp.exp2(ce) for ce in cum_end]
            st["p_tot"] = jnp.concatenate(
                [jnp.broadcast_to(pc, (CHUNK, D_RWKV)) for pc in st["p_chunk"]], axis=0)
            st["cum"], st["logw"] = cum, logw
            project(COL_K, D_RWKV)

        def f_receptance():
            r = lerp(COL_R, D_RWKV)
            rt_ref[...] = r * jnp.exp2(st["cum"])
            bon_ref[...] = r * rk_ref[...]
            project(COL_V, D_RWKV)

        def store_transposed(ref, val):
            for c in range(n_chunks):
                for g in range(N_GROUPS):
                    ref[rows_of(c), lanes_of(g)] = _heads_to_rows_t(val[rows_of(c), lanes_of(g)])

        def f_key():
            a, cum = st["a"], st["cum"]
            k = lerp(COL_K, D_RWKV)
            kkr = k * kk_ref[...]
            ssq = segsum((kkr * kkr).astype(BF16))
            kk = kkr * lax.rsqrt(jnp.maximum(ssq, L2_EPS * L2_EPS))
            k2 = k * (a * ka_ref[...] + (1.0 - ka_ref[...]))
            p_inv = jnp.exp2(-cum)
            bt = kk * a * p_inv
            kt = k2 * p_inv
            store_transposed(bt_ref, bt)
            store_transposed(kt_ref, kt)
            store_transposed(bdc_ref, bt * st["p_tot"])
            store_transposed(kdc_ref, kt * st["p_tot"])
            at_ref[...] = -kk * jnp.exp2(cum - st["logw"])
            bon_ref[...] = segsum((bon_ref[...] * k2).astype(BF16)) * v_ref[...]

        def f_value():
            v_ref[...] = lerp(COL_V, D_RWKV)

        front = [f_norm, f_decay, f_receptance, f_value, f_key]

        inst = [(c, g) for c in range(n_chunks) for g in range(N_GROUPS)]
        lab, lak, mrb, mrk = {}, {}, {}, {}
        tinv, lpow, lakv, mrkv, kv, xm = {}, {}, {}, {}, {}, {}
        zt, q, y0, a_mat, g_mat = {}, {}, {}, {}, {}

        def st_scores(i):
            rows, lanes = rows_of(i[0]), lanes_of(i[1])
            ar = jnp.concatenate([at_ref[rows, lanes], rt_ref[rows, lanes]], axis=0).astype(BF16)
            rhs1 = jnp.concatenate([bdt(blocks(bt_ref[rows, lanes])),
                                    bdt(blocks(kt_ref[rows, lanes]))], axis=1)
            sc = _dot(ar, rhs1) * trimask_ref[...]
            lab[i] = sc[0:CHUNK, 0:GROUP]
            lak[i] = sc[0:CHUNK, GROUP:2 * GROUP]
            mrb[i] = sc[CHUNK:2 * CHUNK, 0:GROUP]
            mrk[i] = sc[CHUNK:2 * CHUNK, GROUP:2 * GROUP]

        def st_inv_start(i):
            rows, lanes = rows_of(i[0]), lanes_of(i[1])
            ldiag = lab[i] * tmask_ref[0]
            tinv[i] = eye + ldiag
            lpow[i] = _dot(ldiag.astype(BF16), bd(ldiag))
            xv = _dot(jnp.concatenate([lak[i], mrk[i], kdc_ref[rows, lanes]], axis=0).astype(BF16),
                      bdt(blocks(v_ref[rows, lanes])))
            lakv[i] = xv[0:CHUNK, :]
            mrkv[i] = xv[CHUNK:2 * CHUNK, :]
            kv[i] = xv[2 * CHUNK:3 * CHUNK, :]

        def st_neumann(i):
            res = _dot(jnp.concatenate([lpow[i], tinv[i]], axis=0).astype(BF16), bd(lpow[i]))
            lpow[i] = res[0:CHUNK, :]
            tinv[i] = tinv[i] + res[CHUNK:2 * CHUNK, :]

        def st_neumann_last(i):
            tinv[i] = tinv[i] + _dot(tinv[i].astype(BF16), bd(lpow[i]))

        def st_merge_x(level):
            def stage(i):
                xm[i] = _dot((lab[i] * tmask_ref[level]).astype(BF16), bd(tinv[i]))
            return stage

        def st_merge_t(i):
            tinv[i] = tinv[i] + _dot(tinv[i].astype(BF16), bd(xm[i]))

        def st_last_merge_x(i):
            rows, lanes = rows_of(i[0]), lanes_of(i[1])
            res = _dot(jnp.concatenate([lab[i] * tmask_ref[N_MERGES], mrb[i], bdc_ref[rows, lanes]],
                                       axis=0).astype(BF16),
                       bd(tinv[i]))
            xm[i] = res[0:CHUNK, :]
            zt[i] = res[CHUNK:3 * CHUNK, :]

        def st_last_merge_t(i):
            zt[i] = zt[i] + _dot(zt[i].astype(BF16), bd(xm[i]))

        def st_transition(i):
            rows, lanes = rows_of(i[0]), lanes_of(i[1])
            res = _dot(zt[i].astype(BF16),
                       jnp.concatenate([bdt(blocks(at_ref[rows, lanes])), bdt(blocks(lakv[i]))], axis=1))
            q[i] = rt_ref[rows, lanes] + res[0:CHUNK, 0:GROUP]
            y0[i] = res[0:CHUNK, GROUP:2 * GROUP] + mrkv[i]
            a_mat[i] = eye * st["p_chunk"][i[0]][:, lanes] + res[CHUNK:2 * CHUNK, 0:GROUP]
            g_mat[i] = res[CHUNK:2 * CHUNK, GROUP:2 * GROUP] + kv[i]

        stage_fns = ([st_scores, st_inv_start] + [st_neumann] * (INV_BLOCK.bit_length() - 3)
                     + [st_neumann_last])
        for level in range(1, N_MERGES):
            stage_fns += [st_merge_x(level), st_merge_t]
        stage_fns += [st_last_merge_x, st_last_merge_t, st_transition]

        def run_stage(fn):
            def thunk():
                for i in inst:
                    fn(i)
            return thunk

        chunk_stages = [run_stage(fn) for fn in stage_fns]

        y_chunks = {}

        def chain_step(c):
            def thunk():
                if c == 0:
                    st["hs"] = [h_ref[g] for g in range(N_GROUPS)]
                hs, ys = st["hs"], []
                for g in range(N_GROUPS):
                    i = (c, g)
                    res = _dot(jnp.concatenate([a_mat[i], q[i]], axis=0).astype(BF16), bd(hs[g]))
                    hs[g] = res[0:CHUNK, :] + g_mat[i]
                    ys.append(res[CHUNK:2 * CHUNK, :] + y0[i])
                y_chunks[c] = jnp.concatenate(ys, axis=1)
                if c == n_chunks - 1:
                    for g in range(N_GROUPS):
                        h_ref[g] = hs[g]
            return thunk

        diffs = []

        def pool_windows():
            n_ext = TB + PAD
            sa_ref[8:n_ext, :] = p_ref[8:n_ext, 0:512] + p_ref[7:n_ext - 1, 0:512]
            sb_ref[8:n_ext, 128:512] = sa_ref[8:n_ext, 128:512] + sa_ref[6:n_ext - 2, 128:512]
            sa_ref[8:n_ext, 256:512] = sb_ref[8:n_ext, 256:512] + sb_ref[4:n_ext - 4, 256:512]
            sb_ref[8:n_ext, 384:512] = sa_ref[8:n_ext, 384:512] + sa_ref[0:n_ext - 8, 384:512]

        def pool_diffs():
            pos = t * TB + lax.broadcasted_iota(jnp.int32, (TB, 1), 0) + 1
            wsum_refs = (sa_ref, sb_ref, sa_ref, sb_ref)
            for g, win in enumerate(POOL_WINDOWS):
                lanes = slice(g * POOL_GROUP, (g + 1) * POOL_GROUP)
                cnt = jnp.minimum(pos, win).astype(F32)
                mean = wsum_refs[g][PAD:PAD + TB, lanes] / cnt
                diffs.append(mean - p_ref[PAD:PAD + TB, lanes])

        def pool_out(half):
            lanes = slice(256 * half, 256 * half + 256)
            d2 = jnp.concatenate(diffs[2 * half:2 * half + 2], axis=1).astype(BF16)
            yp = _dot(d2, _diag2(pw_ref[2 * half].astype(BF16), pw_ref[2 * half + 1].astype(BF16)))
            zz = p_ref[PAD:PAD + TB, COL_PZ + 256 * half:COL_PZ + 256 * half + 256]
            yp = yp * pscale_ref[:, lanes] * (zz * _sigmoid(zz))
            mix_ref[:, lanes] = yp.astype(BF16)

        half_w = D_POOL // 2
        fillers = [
            lambda: project(COL_U, half_w),
            lambda: project(COL_U + half_w, half_w),
            lambda: (project(COL_PZ, half_w), pool_windows()),
            lambda: (project(COL_PZ + half_w, half_w), pool_diffs()),
            lambda: (project(COL_Z, half_w), pool_out(0)),
            lambda: (project(COL_Z + half_w, half_w), pool_out(1)),
        ]

        def finish():
            yh = jnp.concatenate([y_chunks[c] for c in range(n_chunks)], axis=0)
            mu = segsum(yh.astype(BF16)) * (1.0 / HEAD)
            dlt = yh - mu
            var = segsum((dlt * dlt).astype(BF16)) * (1.0 / HEAD)
            yn = dlt * lax.rsqrt(var + GN_EPS) * lnw_ref[...] + lnb_ref[...]
            z = lerp(COL_Z, D_RWKV)
            y_rwkv = (yn + bon_ref[...]) * (z * _sigmoid(z))
            mix_ref[:, D_POOL:D_POOL + D_RWKV] = y_rwkv.astype(BF16)
            out = _dot(mix_ref[...], wout_ref[...])
            xo = xs_ref[...] + gate * out
            ms2 = jnp.mean(xo * xo, axis=-1, keepdims=True)
            os_ref[...] = xo * lax.rsqrt(ms2 + NORM_EPS) * fg_ref[...]
            p_ref[8:PAD, :] = p_ref[TB + 8:TB + PAD, :]

        tail = []
        for c in range(n_chunks):
            tail.append(chain_step(c))
            tail += fillers[c:c + 1]
        tail += fillers[n_chunks:] + [finish]
        return front, chunk_stages, tail

    front, chunk_stages, tail = build()
    for thunk in front + chunk_stages + tail:
        thunk()


def _constants():
    hb = np.arange(GROUP) // HEAD
    bdmask = (hb[:, None] == hb[None, :]).astype(np.float32)
    ti = np.arange(CHUNK)[:, None]
    si = (np.arange(2 * GROUP) % HEAD)[None, :]
    trimask = np.concatenate([(ti > si), (ti >= si)], axis=0).astype(np.float32)
    sg = (np.arange(GROUP) % HEAD)[None, :]
    eye = (ti == sg).astype(np.float32)
    same = [(ti // (INV_BLOCK << m)) == (sg // (INV_BLOCK << m)) for m in range(N_MERGES + 1)]
    tmask = np.stack([same[0]] + [same[m] & ~same[m - 1] for m in range(1, N_MERGES + 1)],
                     axis=0).astype(np.float32)
    rr = np.arange(CUM_ROWS)
    same = (rr[:, None] // CHUNK) == (rr[None, :] // CHUNK)
    tril = same & (rr[None, :] <= rr[:, None])
    cmat = tril.astype(np.float32)
    return bdmask, trimask, eye, tmask, cmat


def kernel(x, c, w_ada, b_ada, norm_g, w_in, pool_w, pool_scale, mu_shift, w0, w_up, a0, a_up,
           k_k, k_a, r_k, ln_w, ln_b, w_out, final_g):
    B, T, _ = x.shape
    assert w_ada.shape[0] == 1 and T % TB == 0
    l = 0

    mod = _ada_mod(c, w_ada, b_ada, l).reshape(B, 3, D_MODEL)

    w_in_p, w_out_b = _win_prep(w_in, w_out, l)
    mu_p = jnp.concatenate([mu_shift[l, a - _S0:b - _S0] for a, b in W_IN_SPANS[1:]], axis=0)[None, :]

    bdmask, trimask, eye, tmask, cmat = _constants()
    bdmask_b = jnp.asarray(bdmask, BF16)
    segones = bdmask_b
    trimask = jnp.asarray(trimask)
    eye = jnp.asarray(eye)
    tmask = jnp.asarray(tmask)
    cmat = jnp.asarray(cmat, BF16)

    def full(a):
        nd = a.ndim
        return pl.BlockSpec(a.shape, lambda b, t, _nd=nd: (0,) * _nd)

    def layer(a):
        a = a.reshape(a.shape[0], 1, a.shape[1]) if a.ndim == 2 else a
        nd = a.ndim
        return a, pl.BlockSpec((None,) + a.shape[1:], lambda b, t, _nd=nd: (l,) + (0,) * (_nd - 1))

    operands = [
        (x, pl.BlockSpec((None, TB, D_MODEL), lambda b, t: (b, t, 0))),
        (mod, pl.BlockSpec((None, 3, D_MODEL), lambda b, t: (b, 0, 0))),
        layer(norm_g), (w_in_p, full(w_in_p)), (mu_p, full(mu_p)), layer(w_up), layer(a_up),
        layer(w0), layer(a0), layer(k_k), layer(k_a), layer(r_k), layer(ln_w), layer(ln_b),
        layer(pool_scale), layer(pool_w), (w_out_b, full(w_out_b)),
        (final_g[None, :], full(final_g[None, :])),
    ] + [(a, full(a)) for a in (bdmask_b, trimask, eye, tmask, cmat, segones)]
    in_specs = [spec for _, spec in operands]

    blk = lambda: pltpu.VMEM((TB, D_RWKV), F32)
    scratch = [
        pltpu.VMEM((TB + PAD, N_IN), F32),
        pltpu.VMEM((TB + PAD, D_POOL), F32),
        pltpu.VMEM((TB + PAD, D_POOL), F32),
        blk(), blk(), blk(), blk(), blk(), blk(),
        blk(), blk(),
        pltpu.VMEM((TB, D_MODEL), BF16),
        pltpu.VMEM((TB, D_MODEL), BF16),
        pltpu.VMEM((N_GROUPS, HEAD, GROUP), F32),
    ]
    return pl.pallas_call(
        _fused_kernel,
        grid=(B, T // TB),
        in_specs=in_specs,
        out_specs=pl.BlockSpec((None, TB, D_MODEL), lambda b, t: (b, t, 0)),
        out_shape=jax.ShapeDtypeStruct((B, T, D_MODEL), x.dtype),
        scratch_shapes=scratch,
        compiler_params=pltpu.CompilerParams(
            dimension_semantics=("arbitrary", "arbitrary"),
            vmem_limit_bytes=VMEM_LIMIT_BYTES),
        name="hybrid_block",
    )(*[a for a, _ in operands])
```

```python
import numpy as np
import jax
import jax.numpy as jnp
from jax import lax
from jax.experimental import pallas as pl
from jax.experimental.pallas import tpu as pltpu

F32 = jnp.float32
BF16 = jnp.bfloat16

D_MODEL = 1024
D_POOL = 512
D_RWKV = 512
HEAD = 64
LORA = 64
POOL_WINDOWS = (2, 4, 8, 16)
POOL_GROUP = 128
NORM_EPS = 1e-6
GN_EPS = 64e-5
L2_EPS = 1e-12

COL_U = 0
COL_PZ = 512
COL_R = 1024
COL_K = 1536
COL_V = 2048
COL_Z = 2560
COL_LO = 3072
N_IN = 3200
SEG0 = 1024

CHUNK = 64
GROUP = 256
HEADS_PER_GROUP = GROUP // HEAD
N_GROUPS = D_RWKV // GROUP
TB = 512
CUM_ROWS = 256
INV_BLOCK = 32
N_MERGES = (CHUNK // INV_BLOCK).bit_length() - 1
PAD = 24

VMEM_LIMIT_BYTES = 56 * 1024 * 1024


def _split_hi_lo(x):
    hi = x.astype(BF16)
    lo = (x - hi.astype(F32)).astype(BF16)
    return hi, lo


def _dot(a, b):
    return jnp.dot(a, b, preferred_element_type=F32)


def _sigmoid(x):
    return 0.5 * jnp.tanh(0.5 * x) + 0.5


ADA_ROWS = 256
LANES = 128


def _ada_kernel(cb_ref, w_ref, b_ref, o_ref):
    nb, n_mod, d = o_ref.shape

    @pl.when(pl.program_id(0) == 0)
    def _():
        for m in range(n_mod):
            o_ref[:, m, :] = jnp.broadcast_to(b_ref[:, m * d:(m + 1) * d], (nb, d))

    w = w_ref[...]
    rows = []
    for b in range(nb):
        cb = cb_ref[b]
        cols = [jnp.sum(w[:, j:j + LANES] * cb, axis=0, keepdims=True)
                for j in range(0, w.shape[1], LANES)]
        rows.append(jnp.concatenate(cols, axis=1))
    acc = jnp.concatenate(rows, axis=0)
    for m in range(n_mod):
        o_ref[:, m, :] += acc[:, m * d:(m + 1) * d]


def _ada_mod(c, w_ada, b_ada, l):
    nb, n = c.shape[0], w_ada.shape[2]
    cb = jnp.broadcast_to(c[:, :, None], (nb, D_MODEL, LANES))
    return pl.pallas_call(
        _ada_kernel,
        grid=(D_MODEL // ADA_ROWS,),
        in_specs=[
            pl.BlockSpec((nb, ADA_ROWS, LANES), lambda j: (0, j, 0)),
            pl.BlockSpec((None, ADA_ROWS, n), lambda j: (l, j, 0)),
            pl.BlockSpec((None, 1, n), lambda j: (l, 0, 0)),
        ],
        out_specs=pl.BlockSpec((nb, n // D_MODEL, D_MODEL), lambda j: (0, 0, 0)),
        out_shape=jax.ShapeDtypeStruct((nb, n // D_MODEL, D_MODEL), F32),
        name="adaln_mod",
    )(cb, w_ada, b_ada.reshape(b_ada.shape[0], 1, n))


_S0 = 2 * D_POOL
W_IN_SPANS = ((0, _S0), (_S0, _S0 + 512), (_S0 + 576, _S0 + 1088), (_S0 + 1088, _S0 + 1600),
              (_S0 + 1664, _S0 + 2176), (_S0 + 512, _S0 + 576), (_S0 + 1600, _S0 + 1664))
WPREP_ROWS = 256


def _win_prep_kernel(w_ref, wo_ref, o_ref, oo_ref):
    w = w_ref[...]
    o_ref[...] = jnp.concatenate([w[:, a:b] for a, b in W_IN_SPANS], axis=1).astype(BF16)
    oo_ref[...] = wo_ref[...].astype(BF16)


def _win_prep(w_in, w_out, l):
    d, n = w_in.shape[1], w_in.shape[2]
    do, no = w_out.shape[1], w_out.shape[2]
    steps = d // WPREP_ROWS
    return pl.pallas_call(
        _win_prep_kernel,
        grid=(steps,),
        in_specs=[pl.BlockSpec((None, WPREP_ROWS, n), lambda i: (l, i, 0)),
                  pl.BlockSpec((None, do // steps, no), lambda i: (l, i, 0))],
        out_specs=[pl.BlockSpec((WPREP_ROWS, n), lambda i: (i, 0)),
                   pl.BlockSpec((do // steps, no), lambda i: (i, 0))],
        out_shape=[jax.ShapeDtypeStruct((d, n), BF16), jax.ShapeDtypeStruct((do, no), BF16)],
        name="w_in_prep",
    )(w_in, w_out)


def _head_blocks(x, bdmask):
    xb = x.astype(BF16)
    heads_per_tile = LANES // HEAD
    zero_tile = jnp.zeros((x.shape[0], LANES), BF16)
    out = []
    for h in range(HEADS_PER_GROUP):
        tiles = [xb[:, j * LANES:(j + 1) * LANES] * bdmask[h * HEAD:(h + 1) * HEAD, j * LANES:(j + 1) * LANES]
                 if j == h // heads_per_tile else zero_tile
                 for j in range(GROUP // LANES)]
        out.append(jnp.concatenate(tiles, axis=1))
    return out


def _block_diag(x, bdmask):
    return jnp.concatenate(_head_blocks(x, bdmask), axis=0)


T_HEAD_ORDER = (0, 2, 1, 3)


def _heads_to_rows_t(x):
    xt = jnp.concatenate([x[:, 0:2 * HEAD], x[:, 2 * HEAD:4 * HEAD]], axis=0).T
    return jnp.concatenate([xt[0:HEAD, :], xt[HEAD:2 * HEAD, :]], axis=1)


def _diag2(m0, m1):
    z = jnp.zeros_like(m0)
    return jnp.concatenate([jnp.concatenate([m0, z], axis=1),
                            jnp.concatenate([z, m1], axis=1)], axis=0)


def _fused_kernel(x_ref, mod_ref, ng_ref, win_ref, mu_ref, wup_ref, aup_ref,
                  w0_ref, a0_ref, kk_ref, ka_ref, rk_ref, lnw_ref, lnb_ref, pscale_ref,
                  pw_ref, wout_ref, fg_ref, bdmask_ref, trimask_ref, eye_ref, tmask_ref, cmat_ref,
                  segones_ref, o_ref,
                  p_ref, sa_ref, sb_ref, at_ref, rt_ref, bt_ref, kt_ref, bdc_ref, kdc_ref,
                  v_ref, bon_ref, mix_ref, hb_ref, h_ref):
    t = pl.program_id(1)

    def normalise_into_hb(xin_ref):
        xb = xin_ref[...]
        ms = jnp.mean(xb * xb, axis=-1, keepdims=True)
        gs = ng_ref[...] * (1.0 + mod_ref[1:2, :])
        hb_ref[...] = (xb * lax.rsqrt(ms + NORM_EPS) * gs + mod_ref[0:1, :]).astype(BF16)

    @pl.when(t == 0)
    def _():
        p_ref[0:PAD, :] = jnp.zeros((PAD, N_IN), F32)
        sa_ref[0:8, :] = jnp.zeros((8, D_POOL), F32)
        sb_ref[0:8, :] = jnp.zeros((8, D_POOL), F32)
        h_ref[...] = jnp.zeros(h_ref.shape, F32)

    bdmask = bdmask_ref[...]
    eye = eye_ref[...]
    n_chunks = TB // CHUNK

    def blocks(xv):
        return _head_blocks(xv, bdmask)

    def bd(xv):
        return _block_diag(xv, bdmask)

    def bdt(blks):
        return jnp.concatenate([blks[h] for h in T_HEAD_ORDER], axis=0)

    def rows_of(c):
        return slice(c * CHUNK, (c + 1) * CHUNK)

    def lanes_of(g):
        return slice(GROUP * g, GROUP * (g + 1))

    def segsum(xb):
        parts = []
        for half in range(N_GROUPS):
            parts.append(_dot(xb[:, GROUP * half:GROUP * (half + 1)], segones_ref[...]))
        return jnp.concatenate(parts, axis=1)

    def build():
        xs_ref, os_ref = x_ref, o_ref
        gate = mod_ref[2:3, :]
        st = {}

        def project(col, width):
            p_ref[PAD:PAD + TB, col:col + width] = _dot(hb_ref[...], win_ref[:, col:col + width])

        def lerp(col, width, row0=0, nrows=TB):
            ext = p_ref[PAD + row0 - 8:PAD + row0 + nrows, col:col + width]
            cur = ext[8:, :]
            prev = pltpu.roll(ext, 1, axis=0)[8:, :]
            return cur + (prev - cur) * mu_ref[0:1, col - SEG0:col - SEG0 + width]

        def f_norm():
            normalise_into_hb(x_ref)
            project(COL_LO, 2 * LORA)
            project(COL_R, D_RWKV)

        def f_decay():
            lo = lerp(COL_LO, 2 * LORA)
            lane = lax.broadcasted_iota(jnp.int32, lo.shape, 1)
            lo = jnp.where(lane < LORA, jnp.tanh(lo), lo)
            lo_hi, lo_lo = _split_hi_lo(lo)
            lora = _diag2(wup_ref[...].astype(BF16), aup_ref[...].astype(BF16))
            lin = _dot(jnp.concatenate([lo_hi, lo_lo], axis=1), jnp.concatenate([lora, lora], axis=0))
            logw = -float(np.exp(-0.5) * np.log2(np.e)) * _sigmoid(w0_ref[...] + lin[:, 0:D_RWKV])
            st["a"] = _sigmoid(a0_ref[...] + lin[:, D_RWKV:2 * D_RWKV])
            lw_b = logw.astype(BF16)
            logw = lw_b.astype(F32)
            cum = jnp.concatenate(
                [_dot(cmat_ref[...], lw_b[rb:rb + CUM_ROWS, :]) for rb in range(0, TB, CUM_ROWS)], axis=0)
            cum_end = [cum[(c + 1) * CHUNK - 1:(c + 1) * CHUNK, :] for c in range(n_chunks)]
            st["p_chunk"] = [jnp.exp2(ce) for ce in cum_end]
            st["p_tot"] = jnp.concatenate(
                [jnp.broadcast_to(pc, (CHUNK, D_RWKV)) for pc in st["p_chunk"]], axis=0)
            st["cum"], st["logw"] = cum, logw
            project(COL_K, D_RWKV)

        def f_receptance():
            r = lerp(COL_R, D_RWKV)
            rt_ref[...] = r * jnp.exp2(st["cum"])
            bon_ref[...] = r * rk_ref[...]
            project(COL_V, D_RWKV)

        def store_transposed(ref, val):
            for c in range(n_chunks):
                for g in range(N_GROUPS):
                    ref[rows_of(c), lanes_of(g)] = _heads_to_rows_t(val[rows_of(c), lanes_of(g)])

        def f_key():
            a, cum = st["a"], st["cum"]
            k = lerp(COL_K, D_RWKV)
            kkr = k * kk_ref[...]
            ssq = segsum((kkr * kkr).astype(BF16))
            kk = kkr * lax.rsqrt(jnp.maximum(ssq, L2_EPS * L2_EPS))
            k2 = k * (a * ka_ref[...] + (1.0 - ka_ref[...]))
            p_inv = jnp.exp2(-cum)
            bt = kk * a * p_inv
            kt = k2 * p_inv
            store_transposed(bt_ref, bt)
            store_transposed(kt_ref, kt)
            store_transposed(bdc_ref, bt * st["p_tot"])
            store_transposed(kdc_ref, kt * st["p_tot"])
            at_ref[...] = -kk * jnp.exp2(cum - st["logw"])
            bon_ref[...] = segsum((bon_ref[...] * k2).astype(BF16)) * v_ref[...]

        def f_value():
            v_ref[...] = lerp(COL_V, D_RWKV)

        front = [f_norm, f_decay, f_receptance, f_value, f_key]

        inst = [(c, g) for c in range(n_chunks) for g in range(N_GROUPS)]
        lab, lak, mrb, mrk = {}, {}, {}, {}
        tinv, lpow, lakv, mrkv, kv, xm = {}, {}, {}, {}, {}, {}
        zt, q, y0, a_mat, g_mat = {}, {}, {}, {}, {}

        def st_scores(i):
            rows, lanes = rows_of(i[0]), lanes_of(i[1])
            ar = jnp.concatenate([at_ref[rows, lanes], rt_ref[rows, lanes]], axis=0).astype(BF16)
            rhs1 = jnp.concatenate([bdt(blocks(bt_ref[rows, lanes])),
                                    bdt(blocks(kt_ref[rows, lanes]))], axis=1)
            sc = _dot(ar, rhs1) * trimask_ref[...]
            lab[i] = sc[0:CHUNK, 0:GROUP]
            lak[i] = sc[0:CHUNK, GROUP:2 * GROUP]
            mrb[i] = sc[CHUNK:2 * CHUNK, 0:GROUP]
            mrk[i] = sc[CHUNK:2 * CHUNK, GROUP:2 * GROUP]

        def st_inv_start(i):
            rows, lanes = rows_of(i[0]), lanes_of(i[1])
            ldiag = lab[i] * tmask_ref[0]
            tinv[i] = eye + ldiag
            lpow[i] = _dot(ldiag.astype(BF16), bd(ldiag))
            xv = _dot(jnp.concatenate([lak[i], mrk[i], kdc_ref[rows, lanes]], axis=0).astype(BF16),
                      bdt(blocks(v_ref[rows, lanes])))
            lakv[i] = xv[0:CHUNK, :]
            mrkv[i] = xv[CHUNK:2 * CHUNK, :]
            kv[i] = xv[2 * CHUNK:3 * CHUNK, :]

        def st_neumann(i):
            res = _dot(jnp.concatenate([lpow[i], tinv[i]], axis=0).astype(BF16), bd(lpow[i]))
            lpow[i] = res[0:CHUNK, :]
            tinv[i] = tinv[i] + res[CHUNK:2 * CHUNK, :]

        def st_neumann_last(i):
            tinv[i] = tinv[i] + _dot(tinv[i].astype(BF16), bd(lpow[i]))

        def st_merge_x(level):
            def stage(i):
                xm[i] = _dot((lab[i] * tmask_ref[level]).astype(BF16), bd(tinv[i]))
            return stage

        def st_merge_t(i):
            tinv[i] = tinv[i] + _dot(tinv[i].astype(BF16), bd(xm[i]))

        def st_last_merge_x(i):
            rows, lanes = rows_of(i[0]), lanes_of(i[1])
            res = _dot(jnp.concatenate([lab[i] * tmask_ref[N_MERGES], mrb[i], bdc_ref[rows, lanes]],
                                       axis=0).astype(BF16),
                       bd(tinv[i]))
            xm[i] = res[0:CHUNK, :]
            zt[i] = res[CHUNK:3 * CHUNK, :]

        def st_last_merge_t(i):
            zt[i] = zt[i] + _dot(zt[i].astype(BF16), bd(xm[i]))

        def st_transition(i):
            rows, lanes = rows_of(i[0]), lanes_of(i[1])
            res = _dot(zt[i].astype(BF16),
                       jnp.concatenate([bdt(blocks(at_ref[rows, lanes])), bdt(blocks(lakv[i]))], axis=1))
            q[i] = rt_ref[rows, lanes] + res[0:CHUNK, 0:GROUP]
            y0[i] = res[0:CHUNK, GROUP:2 * GROUP] + mrkv[i]
            a_mat[i] = eye * st["p_chunk"][i[0]][:, lanes] + res[CHUNK:2 * CHUNK, 0:GROUP]
            g_mat[i] = res[CHUNK:2 * CHUNK, GROUP:2 * GROUP] + kv[i]

        stage_fns = ([st_scores, st_inv_start] + [st_neumann] * (INV_BLOCK.bit_length() - 3)
                     + [st_neumann_last])
        for level in range(1, N_MERGES):
            stage_fns += [st_merge_x(level), st_merge_t]
        stage_fns += [st_last_merge_x, st_last_merge_t, st_transition]

        def run_stage(fn):
            def thunk():
                for i in inst:
                    fn(i)
            return thunk

        chunk_stages = [run_stage(fn) for fn in stage_fns]

        y_chunks = {}

        def chain_step(c):
            def thunk():
                if c == 0:
                    st["hs"] = [h_ref[g] for g in range(N_GROUPS)]
                hs, ys = st["hs"], []
                for g in range(N_GROUPS):
                    i = (c, g)
                    res = _dot(jnp.concatenate([a_mat[i], q[i]], axis=0).astype(BF16), bd(hs[g]))
                    hs[g] = res[0:CHUNK, :] + g_mat[i]
                    ys.append(res[CHUNK:2 * CHUNK, :] + y0[i])
                y_chunks[c] = jnp.concatenate(ys, axis=1)
                if c == n_chunks - 1:
                    for g in range(N_GROUPS):
                        h_ref[g] = hs[g]
            return thunk

        diffs = []

        def pool_windows():
            n_ext = TB + PAD
            sa_ref[8:n_ext, :] = p_ref[8:n_ext, 0:512] + p_ref[7:n_ext - 1, 0:512]
            sb_ref[8:n_ext, 128:512] = sa_ref[8:n_ext, 128:512] + sa_ref[6:n_ext - 2, 128:512]
            sa_ref[8:n_ext, 256:512] = sb_ref[8:n_ext, 256:512] + sb_ref[4:n_ext - 4, 256:512]
            sb_ref[8:n_ext, 384:512] = sa_ref[8:n_ext, 384:512] + sa_ref[0:n_ext - 8, 384:512]

        def pool_diffs():
            pos = t * TB + lax.broadcasted_iota(jnp.int32, (TB, 1), 0) + 1
            wsum_refs = (sa_ref, sb_ref, sa_ref, sb_ref)
            for g, win in enumerate(POOL_WINDOWS):
                lanes = slice(g * POOL_GROUP, (g + 1) * POOL_GROUP)
                cnt = jnp.minimum(pos, win).astype(F32)
                mean = wsum_refs[g][PAD:PAD + TB, lanes] / cnt
                diffs.append(mean - p_ref[PAD:PAD + TB, lanes])

        def pool_out(half):
            lanes = slice(256 * half, 256 * half + 256)
            d2 = jnp.concatenate(diffs[2 * half:2 * half + 2], axis=1).astype(BF16)
            yp = _dot(d2, _diag2(pw_ref[2 * half].astype(BF16), pw_ref[2 * half + 1].astype(BF16)))
            zz = p_ref[PAD:PAD + TB, COL_PZ + 256 * half:COL_PZ + 256 * half + 256]
            yp = yp * pscale_ref[:, lanes] * (zz * _sigmoid(zz))
            mix_ref[:, lanes] = yp.astype(BF16)

        half_w = D_POOL // 2
        fillers = [
            lambda: project(COL_U, half_w),
            lambda: project(COL_U + half_w, half_w),
            lambda: (project(COL_PZ, half_w), pool_windows()),
            lambda: (project(COL_PZ + half_w, half_w), pool_diffs()),
            lambda: (project(COL_Z, half_w), pool_out(0)),
            lambda: (project(COL_Z + half_w, half_w), pool_out(1)),
        ]

        def finish():
            yh = jnp.concatenate([y_chunks[c] for c in range(n_chunks)], axis=0)
            mu = segsum(yh.astype(BF16)) * (1.0 / HEAD)
            dlt = yh - mu
            var = segsum((dlt * dlt).astype(BF16)) * (1.0 / HEAD)
            yn = dlt * lax.rsqrt(var + GN_EPS) * lnw_ref[...] + lnb_ref[...]
            z = lerp(COL_Z, D_RWKV)
            y_rwkv = (yn + bon_ref[...]) * (z * _sigmoid(z))
            mix_ref[:, D_POOL:D_POOL + D_RWKV] = y_rwkv.astype(BF16)
            out = _dot(mix_ref[...], wout_ref[...])
            xo = xs_ref[...] + gate * out
            ms2 = jnp.mean(xo * xo, axis=-1, keepdims=True)
            os_ref[...] = xo * lax.rsqrt(ms2 + NORM_EPS) * fg_ref[...]
            p_ref[8:PAD, :] = p_ref[TB + 8:TB + PAD, :]

        tail = []
        for c in range(n_chunks):
            tail += fillers[c:c + 1]
            tail.append(chain_step(c))
        tail += fillers[n_chunks:] + [finish]
        return front, chunk_stages, tail

    front, chunk_stages, tail = build()
    for thunk in front + chunk_stages + tail:
        thunk()


def _constants():
    hb = np.arange(GROUP) // HEAD
    bdmask = (hb[:, None] == hb[None, :]).astype(np.float32)
    ti = np.arange(CHUNK)[:, None]
    si = (np.arange(2 * GROUP) % HEAD)[None, :]
    trimask = np.concatenate([(ti > si), (ti >= si)], axis=0).astype(np.float32)
    sg = (np.arange(GROUP) % HEAD)[None, :]
    eye = (ti == sg).astype(np.float32)
    same = [(ti // (INV_BLOCK << m)) == (sg // (INV_BLOCK << m)) for m in range(N_MERGES + 1)]
    tmask = np.stack([same[0]] + [same[m] & ~same[m - 1] for m in range(1, N_MERGES + 1)],
                     axis=0).astype(np.float32)
    rr = np.arange(CUM_ROWS)
    same = (rr[:, None] // CHUNK) == (rr[None, :] // CHUNK)
    tril = same & (rr[None, :] <= rr[:, None])
    cmat = tril.astype(np.float32)
    return bdmask, trimask, eye, tmask, cmat


def kernel(x, c, w_ada, b_ada, norm_g, w_in, pool_w, pool_scale, mu_shift, w0, w_up, a0, a_up,
           k_k, k_a, r_k, ln_w, ln_b, w_out, final_g):
    B, T, _ = x.shape
    assert w_ada.shape[0] == 1 and T % TB == 0
    l = 0

    mod = _ada_mod(c, w_ada, b_ada, l)

    w_in_p, w_out_b = _win_prep(w_in, w_out, l)
    mu_p = jnp.concatenate([mu_shift[l, a - _S0:b - _S0] for a, b in W_IN_SPANS[1:]], axis=0)[None, :]

    bdmask, trimask, eye, tmask, cmat = _constants()
    bdmask_b = jnp.asarray(bdmask, BF16)
    segones = bdmask_b
    trimask = jnp.asarray(trimask)
    eye = jnp.asarray(eye)
    tmask = jnp.asarray(tmask)
    cmat = jnp.asarray(cmat, BF16)

    def full(a):
        nd = a.ndim
        return pl.BlockSpec(a.shape, lambda b, t, _nd=nd: (0,) * _nd)

    def layer(a):
        a = a.reshape(a.shape[0], 1, a.shape[1]) if a.ndim == 2 else a
        nd = a.ndim
        return a, pl.BlockSpec((None,) + a.shape[1:], lambda b, t, _nd=nd: (l,) + (0,) * (_nd - 1))

    operands = [
        (x, pl.BlockSpec((None, TB, D_MODEL), lambda b, t: (b, t, 0))),
        (mod, pl.BlockSpec((None, 3, D_MODEL), lambda b, t: (b, 0, 0))),
        layer(norm_g), (w_in_p, full(w_in_p)), (mu_p, full(mu_p)), layer(w_up), layer(a_up),
        layer(w0), layer(a0), layer(k_k), layer(k_a), layer(r_k), layer(ln_w), layer(ln_b),
        layer(pool_scale), layer(pool_w), (w_out_b, full(w_out_b)),
        (final_g[None, :], full(final_g[None, :])),
    ] + [(a, full(a)) for a in (bdmask_b, trimask, eye, tmask, cmat, segones)]
    in_specs = [spec for _, spec in operands]

    blk = lambda: pltpu.VMEM((TB, D_RWKV), F32)
    scratch = [
        pltpu.VMEM((TB + PAD, N_IN), F32),
        pltpu.VMEM((TB + PAD, D_POOL), F32),
        pltpu.VMEM((TB + PAD, D_POOL), F32),
        blk(), blk(), blk(), blk(), blk(), blk(),
        blk(), blk(),
        pltpu.VMEM((TB, D_MODEL), BF16),
        pltpu.VMEM((TB, D_MODEL), BF16),
        pltpu.VMEM((N_GROUPS, HEAD, GROUP), F32),
    ]
    return pl.pallas_call(
        _fused_kernel,
        grid=(B, T // TB),
        in_specs=in_specs,
        out_specs=pl.BlockSpec((None, TB, D_MODEL), lambda b, t: (b, t, 0)),
        out_shape=jax.ShapeDtypeStruct((B, T, D_MODEL), x.dtype),
        scratch_shapes=scratch,
        compiler_params=pltpu.CompilerParams(
            dimension_semantics=("arbitrary", "arbitrary"),
            vmem_limit_bytes=VMEM_LIMIT_BYTES),
        name="hybrid_block",
    )(*[a for a, _ in operands])
```

```python
import numpy as np
import jax
import jax.numpy as jnp
from jax import lax
from jax.experimental import pallas as pl
from jax.experimental.pallas import tpu as pltpu

F32 = jnp.float32
BF16 = jnp.bfloat16

D_MODEL = 1024
D_POOL = 512
D_RWKV = 512
HEAD = 64
LORA = 64
POOL_WINDOWS = (2, 4, 8, 16)
POOL_GROUP = 128
NORM_EPS = 1e-6
GN_EPS = 64e-5
L2_EPS = 1e-12

COL_U = 0
COL_PZ = 512
COL_R = 1024
COL_K = 1536
COL_V = 2048
COL_Z = 2560
COL_LO = 3072
N_IN = 3200
SEG0 = 1024

CHUNK = 64
GROUP = 256
HEADS_PER_GROUP = GROUP // HEAD
N_GROUPS = D_RWKV // GROUP
TB = 512
CUM_ROWS = 256
INV_BLOCK = 32
N_MERGES = (CHUNK // INV_BLOCK).bit_length() - 1
PAD = 24

VMEM_LIMIT_BYTES = 56 * 1024 * 1024


def _split_hi_lo(x):
    hi = x.astype(BF16)
    lo = (x - hi.astype(F32)).astype(BF16)
    return hi, lo


def _dot(a, b):
    return jnp.dot(a, b, preferred_element_type=F32)


def _sigmoid(x):
    return 0.5 * jnp.tanh(0.5 * x) + 0.5


ADA_ROWS = 256
LANES = 128


def _ada_accumulate(cb_ref, w_ref, b_ref, o_ref):
    nb, n_mod, d = o_ref.shape

    @pl.when(pl.program_id(0) == 0)
    def _():
        for m in range(n_mod):
            o_ref[:, m, :] = jnp.broadcast_to(b_ref[:, m * d:(m + 1) * d], (nb, d))

    w = w_ref[...]
    rows = []
    for b in range(nb):
        cb = cb_ref[b]
        cols = [jnp.sum(w[:, j:j + LANES] * cb, axis=0, keepdims=True)
                for j in range(0, w.shape[1], LANES)]
        rows.append(jnp.concatenate(cols, axis=1))
    acc = jnp.concatenate(rows, axis=0)
    for m in range(n_mod):
        o_ref[:, m, :] += acc[:, m * d:(m + 1) * d]


_S0 = 2 * D_POOL
W_IN_SPANS = ((0, _S0), (_S0, _S0 + 512), (_S0 + 576, _S0 + 1088), (_S0 + 1088, _S0 + 1600),
              (_S0 + 1664, _S0 + 2176), (_S0 + 512, _S0 + 576), (_S0 + 1600, _S0 + 1664))
PROLOGUE_VMEM_BYTES = 40 * 1024 * 1024


def _prologue_kernel(cb_ref, wa_ref, b_ref, wi_ref, wo_ref, mod_ref, wip_ref, wob_ref):
    _ada_accumulate(cb_ref, wa_ref, b_ref, mod_ref)
    w = wi_ref[...]
    wip_ref[...] = jnp.concatenate([w[:, a:b] for a, b in W_IN_SPANS], axis=1).astype(BF16)
    wob_ref[...] = wo_ref[...].astype(BF16)


def _prologue(c, w_ada, b_ada, w_in, w_out, l):
    nb, n = c.shape[0], w_ada.shape[2]
    d, n_in = w_in.shape[1], w_in.shape[2]
    n_out = w_out.shape[2]
    cb = jnp.broadcast_to(c[:, :, None], (nb, D_MODEL, LANES))
    return pl.pallas_call(
        _prologue_kernel,
        grid=(D_MODEL // ADA_ROWS,),
        in_specs=[
            pl.BlockSpec((nb, ADA_ROWS, LANES), lambda j: (0, j, 0)),
            pl.BlockSpec((None, ADA_ROWS, n), lambda j: (l, j, 0)),
            pl.BlockSpec((None, 1, n), lambda j: (l, 0, 0)),
            pl.BlockSpec((None, ADA_ROWS, n_in), lambda j: (l, j, 0)),
            pl.BlockSpec((None, ADA_ROWS, n_out), lambda j: (l, j, 0)),
        ],
        out_specs=[
            pl.BlockSpec((nb, n // D_MODEL, D_MODEL), lambda j: (0, 0, 0)),
            pl.BlockSpec((ADA_ROWS, n_in), lambda j: (j, 0)),
            pl.BlockSpec((ADA_ROWS, n_out), lambda j: (j, 0)),
        ],
        out_shape=[
            jax.ShapeDtypeStruct((nb, n // D_MODEL, D_MODEL), F32),
            jax.ShapeDtypeStruct((d, n_in), BF16),
            jax.ShapeDtypeStruct((d, n_out), BF16),
        ],
        compiler_params=pltpu.CompilerParams(
            dimension_semantics=("arbitrary",), vmem_limit_bytes=PROLOGUE_VMEM_BYTES),
        name="prologue",
    )(cb, w_ada, b_ada.reshape(b_ada.shape[0], 1, n), w_in, w_out)


def _head_blocks(x, bdmask):
    xb = x.astype(BF16)
    heads_per_tile = LANES // HEAD
    zero_tile = jnp.zeros((x.shape[0], LANES), BF16)
    out = []
    for h in range(HEADS_PER_GROUP):
        tiles = [xb[:, j * LANES:(j + 1) * LANES] * bdmask[h * HEAD:(h + 1) * HEAD, j * LANES:(j + 1) * LANES]
                 if j == h // heads_per_tile else zero_tile
                 for j in range(GROUP // LANES)]
        out.append(jnp.concatenate(tiles, axis=1))
    return out


def _block_diag(x, bdmask):
    return jnp.concatenate(_head_blocks(x, bdmask), axis=0)


T_HEAD_ORDER = (0, 2, 1, 3)


def _heads_to_rows_t(x):
    xt = jnp.concatenate([x[:, 0:2 * HEAD], x[:, 2 * HEAD:4 * HEAD]], axis=0).T
    return jnp.concatenate([xt[0:HEAD, :], xt[HEAD:2 * HEAD, :]], axis=1)


def _diag2(m0, m1):
    z = jnp.zeros_like(m0)
    return jnp.concatenate([jnp.concatenate([m0, z], axis=1),
                            jnp.concatenate([z, m1], axis=1)], axis=0)


def _fused_kernel(x_ref, mod_ref, ng_ref, win_ref, mu_ref, wup_ref, aup_ref,
                  w0_ref, a0_ref, kk_ref, ka_ref, rk_ref, lnw_ref, lnb_ref, pscale_ref,
                  pw_ref, wout_ref, fg_ref, bdmask_ref, trimask_ref, eye_ref, tmask_ref, cmat_ref,
                  segones_ref, o_ref,
                  p_ref, sa_ref, sb_ref, at_ref, rt_ref, bt_ref, kt_ref, bdc_ref, kdc_ref,
                  v_ref, bon_ref, mix_ref, hb_ref, h_ref):
    t = pl.program_id(1)

    def normalise_into_hb(xin_ref):
        xb = xin_ref[...]
        ms = jnp.mean(xb * xb, axis=-1, keepdims=True)
        gs = ng_ref[...] * (1.0 + mod_ref[1:2, :])
        hb_ref[...] = (xb * lax.rsqrt(ms + NORM_EPS) * gs + mod_ref[0:1, :]).astype(BF16)

    @pl.when(t == 0)
    def _():
        p_ref[0:PAD, :] = jnp.zeros((PAD, N_IN), F32)
        sa_ref[0:8, :] = jnp.zeros((8, D_POOL), F32)
        sb_ref[0:8, :] = jnp.zeros((8, D_POOL), F32)
        h_ref[...] = jnp.zeros(h_ref.shape, F32)

    bdmask = bdmask_ref[...]
    eye = eye_ref[...]
    n_chunks = TB // CHUNK

    def blocks(xv):
        return _head_blocks(xv, bdmask)

    def bd(xv):
        return _block_diag(xv, bdmask)

    def bdt(blks):
        return jnp.concatenate([blks[h] for h in T_HEAD_ORDER], axis=0)

    def rows_of(c):
        return slice(c * CHUNK, (c + 1) * CHUNK)

    def lanes_of(g):
        return slice(GROUP * g, GROUP * (g + 1))

    def segsum(xb):
        parts = []
        for half in range(N_GROUPS):
            parts.append(_dot(xb[:, GROUP * half:GROUP * (half + 1)], segones_ref[...]))
        return jnp.concatenate(parts, axis=1)

    def build():
        xs_ref, os_ref = x_ref, o_ref
        gate = mod_ref[2:3, :]
        st = {}

        def project(col, width):
            p_ref[PAD:PAD + TB, col:col + width] = _dot(hb_ref[...], win_ref[:, col:col + width])

        def lerp(col, width, row0=0, nrows=TB):
            ext = p_ref[PAD + row0 - 8:PAD + row0 + nrows, col:col + width]
            cur = ext[8:, :]
            prev = pltpu.roll(ext, 1, axis=0)[8:, :]
            return cur + (prev - cur) * mu_ref[0:1, col - SEG0:col - SEG0 + width]

        def f_norm():
            normalise_into_hb(x_ref)
            project(COL_LO, 2 * LORA)
            project(COL_R, D_RWKV)

        def f_decay():
            lo = lerp(COL_LO, 2 * LORA)
            lane = lax.broadcasted_iota(jnp.int32, lo.shape, 1)
            lo = jnp.where(lane < LORA, jnp.tanh(lo), lo)
            lo_hi, lo_lo = _split_hi_lo(lo)
            lora = _diag2(wup_ref[...].astype(BF16), aup_ref[...].astype(BF16))
            lin = _dot(jnp.concatenate([lo_hi, lo_lo], axis=1), jnp.concatenate([lora, lora], axis=0))
            logw = -float(np.exp(-0.5) * np.log2(np.e)) * _sigmoid(w0_ref[...] + lin[:, 0:D_RWKV])
            st["a"] = _sigmoid(a0_ref[...] + lin[:, D_RWKV:2 * D_RWKV])
            lw_b = logw.astype(BF16)
            logw = lw_b.astype(F32)
            cum = jnp.concatenate(
                [_dot(cmat_ref[...], lw_b[rb:rb + CUM_ROWS, :]) for rb in range(0, TB, CUM_ROWS)], axis=0)
            cum_end = [cum[(c + 1) * CHUNK - 1:(c + 1) * CHUNK, :] for c in range(n_chunks)]
            st["p_chunk"] = [jnp.exp2(ce) for ce in cum_end]
            st["p_tot"] = jnp.concatenate(
                [jnp.broadcast_to(pc, (CHUNK, D_RWKV)) for pc in st["p_chunk"]], axis=0)
            st["cum"], st["logw"] = cum, logw
            project(COL_K, D_RWKV)

        def f_receptance():
            r = lerp(COL_R, D_RWKV)
            rt_ref[...] = r * jnp.exp2(st["cum"])
            bon_ref[...] = r * rk_ref[...]
            project(COL_V, D_RWKV)

        def store_transposed(ref, val):
            for c in range(n_chunks):
                for g in range(N_GROUPS):
                    ref[rows_of(c), lanes_of(g)] = _heads_to_rows_t(val[rows_of(c), lanes_of(g)])

        def f_key():
            a, cum = st["a"], st["cum"]
            k = lerp(COL_K, D_RWKV)
            kkr = k * kk_ref[...]
            ssq = segsum((kkr * kkr).astype(BF16))
            kk = kkr * lax.rsqrt(jnp.maximum(ssq, L2_EPS * L2_EPS))
            k2 = k * (a * ka_ref[...] + (1.0 - ka_ref[...]))
            p_inv = jnp.exp2(-cum)
            bt = kk * a * p_inv
            kt = k2 * p_inv
            store_transposed(bt_ref, bt)
            store_transposed(kt_ref, kt)
            store_transposed(bdc_ref, bt * st["p_tot"])
            store_transposed(kdc_ref, kt * st["p_tot"])
            at_ref[...] = -kk * jnp.exp2(cum - st["logw"])
            bon_ref[...] = segsum((bon_ref[...] * k2).astype(BF16)) * v_ref[...]

        def f_value():
            v_ref[...] = lerp(COL_V, D_RWKV)

        front = [f_norm, f_decay, f_receptance, f_value, f_key]

        inst = [(c, g) for c in range(n_chunks) for g in range(N_GROUPS)]
        lab, lak, mrb, mrk = {}, {}, {}, {}
        tinv, lpow, lakv, mrkv, kv, xm = {}, {}, {}, {}, {}, {}
        zt, q, y0, a_mat, g_mat = {}, {}, {}, {}, {}

        def st_scores(i):
            rows, lanes = rows_of(i[0]), lanes_of(i[1])
            ar = jnp.concatenate([at_ref[rows, lanes], rt_ref[rows, lanes]], axis=0).astype(BF16)
            rhs1 = jnp.concatenate([bdt(blocks(bt_ref[rows, lanes])),
                                    bdt(blocks(kt_ref[rows, lanes]))], axis=1)
            sc = _dot(ar, rhs1) * trimask_ref[...]
            lab[i] = sc[0:CHUNK, 0:GROUP]
            lak[i] = sc[0:CHUNK, GROUP:2 * GROUP]
            mrb[i] = sc[CHUNK:2 * CHUNK, 0:GROUP]
            mrk[i] = sc[CHUNK:2 * CHUNK, GROUP:2 * GROUP]

        def st_inv_start(i):
            rows, lanes = rows_of(i[0]), lanes_of(i[1])
            ldiag = lab[i] * tmask_ref[0]
            tinv[i] = eye + ldiag
            lpow[i] = _dot(ldiag.astype(BF16), bd(ldiag))
            xv = _dot(jnp.concatenate([lak[i], mrk[i], kdc_ref[rows, lanes]], axis=0).astype(BF16),
                      bdt(blocks(v_ref[rows, lanes])))
            lakv[i] = xv[0:CHUNK, :]
            mrkv[i] = xv[CHUNK:2 * CHUNK, :]
            kv[i] = xv[2 * CHUNK:3 * CHUNK, :]

        def st_neumann(i):
            res = _dot(jnp.concatenate([lpow[i], tinv[i]], axis=0).astype(BF16), bd(lpow[i]))
            lpow[i] = res[0:CHUNK, :]
            tinv[i] = tinv[i] + res[CHUNK:2 * CHUNK, :]

        def st_neumann_last(i):
            tinv[i] = tinv[i] + _dot(tinv[i].astype(BF16), bd(lpow[i]))

        def st_merge_x(level):
            def stage(i):
                xm[i] = _dot((lab[i] * tmask_ref[level]).astype(BF16), bd(tinv[i]))
            return stage

        def st_merge_t(i):
            tinv[i] = tinv[i] + _dot(tinv[i].astype(BF16), bd(xm[i]))

        def st_last_merge_x(i):
            rows, lanes = rows_of(i[0]), lanes_of(i[1])
            res = _dot(jnp.concatenate([lab[i] * tmask_ref[N_MERGES], mrb[i], bdc_ref[rows, lanes]],
                                       axis=0).astype(BF16),
                       bd(tinv[i]))
            xm[i] = res[0:CHUNK, :]
            zt[i] = res[CHUNK:3 * CHUNK, :]

        def st_last_merge_t(i):
            zt[i] = zt[i] + _dot(zt[i].astype(BF16), bd(xm[i]))

        def st_transition(i):
            rows, lanes = rows_of(i[0]), lanes_of(i[1])
            res = _dot(zt[i].astype(BF16),
                       jnp.concatenate([bdt(blocks(at_ref[rows, lanes])), bdt(blocks(lakv[i]))], axis=1))
            q[i] = rt_ref[rows, lanes] + res[0:CHUNK, 0:GROUP]
            y0[i] = res[0:CHUNK, GROUP:2 * GROUP] + mrkv[i]
            a_mat[i] = eye * st["p_chunk"][i[0]][:, lanes] + res[CHUNK:2 * CHUNK, 0:GROUP]
            g_mat[i] = res[CHUNK:2 * CHUNK, GROUP:2 * GROUP] + kv[i]

        stage_fns = ([st_scores, st_inv_start] + [st_neumann] * (INV_BLOCK.bit_length() - 3)
                     + [st_neumann_last])
        for level in range(1, N_MERGES):
            stage_fns += [st_merge_x(level), st_merge_t]
        stage_fns += [st_last_merge_x, st_last_merge_t, st_transition]

        def run_stage(fn):
            def thunk():
                for i in inst:
                    fn(i)
            return thunk

        chunk_stages = [run_stage(fn) for fn in stage_fns]

        y_chunks = {}

        def chain_step(c):
            def thunk():
                if c == 0:
                    st["hs"] = [h_ref[g] for g in range(N_GROUPS)]
                hs, ys = st["hs"], []
                for g in range(N_GROUPS):
                    i = (c, g)
                    res = _dot(jnp.concatenate([a_mat[i], q[i]], axis=0).astype(BF16), bd(hs[g]))
                    hs[g] = res[0:CHUNK, :] + g_mat[i]
                    ys.append(res[CHUNK:2 * CHUNK, :] + y0[i])
                y_chunks[c] = jnp.concatenate(ys, axis=1)
                if c == n_chunks - 1:
                    for g in range(N_GROUPS):
                        h_ref[g] = hs[g]
            return thunk

        diffs = []

        def pool_windows():
            n_ext = TB + PAD
            sa_ref[8:n_ext, :] = p_ref[8:n_ext, 0:512] + p_ref[7:n_ext - 1, 0:512]
            sb_ref[8:n_ext, 128:512] = sa_ref[8:n_ext, 128:512] + sa_ref[6:n_ext - 2, 128:512]
            sa_ref[8:n_ext, 256:512] = sb_ref[8:n_ext, 256:512] + sb_ref[4:n_ext - 4, 256:512]
            sb_ref[8:n_ext, 384:512] = sa_ref[8:n_ext, 384:512] + sa_ref[0:n_ext - 8, 384:512]

        def pool_diffs():
            pos = t * TB + lax.broadcasted_iota(jnp.int32, (TB, 1), 0) + 1
            wsum_refs = (sa_ref, sb_ref, sa_ref, sb_ref)
            for g, win in enumerate(POOL_WINDOWS):
                lanes = slice(g * POOL_GROUP, (g + 1) * POOL_GROUP)
                cnt = jnp.minimum(pos, win).astype(F32)
                mean = wsum_refs[g][PAD:PAD + TB, lanes] / cnt
                diffs.append(mean - p_ref[PAD:PAD + TB, lanes])

        def pool_out(half):
            lanes = slice(256 * half, 256 * half + 256)
            d2 = jnp.concatenate(diffs[2 * half:2 * half + 2], axis=1).astype(BF16)
            yp = _dot(d2, _diag2(pw_ref[2 * half].astype(BF16), pw_ref[2 * half + 1].astype(BF16)))
            zz = p_ref[PAD:PAD + TB, COL_PZ + 256 * half:COL_PZ + 256 * half + 256]
            yp = yp * pscale_ref[:, lanes] * (zz * _sigmoid(zz))
            mix_ref[:, lanes] = yp.astype(BF16)

        half_w = D_POOL // 2
        fillers = [
            lambda: project(COL_U, half_w),
            lambda: project(COL_U + half_w, half_w),
            lambda: (project(COL_PZ, half_w), pool_windows()),
            lambda: (project(COL_PZ + half_w, half_w), pool_diffs()),
            lambda: (project(COL_Z, half_w), pool_out(0)),
            lambda: (project(COL_Z + half_w, half_w), pool_out(1)),
        ]

        def finish():
            yh = jnp.concatenate([y_chunks[c] for c in range(n_chunks)], axis=0)
            mu = segsum(yh.astype(BF16)) * (1.0 / HEAD)
            dlt = yh - mu
            var = segsum((dlt * dlt).astype(BF16)) * (1.0 / HEAD)
            yn = dlt * lax.rsqrt(var + GN_EPS) * lnw_ref[...] + lnb_ref[...]
            z = lerp(COL_Z, D_RWKV)
            y_rwkv = (yn + bon_ref[...]) * (z * _sigmoid(z))
            mix_ref[:, D_POOL:D_POOL + D_RWKV] = y_rwkv.astype(BF16)
            out = _dot(mix_ref[...], wout_ref[...])
            xo = xs_ref[...] + gate * out
            ms2 = jnp.mean(xo * xo, axis=-1, keepdims=True)
            os_ref[...] = xo * lax.rsqrt(ms2 + NORM_EPS) * fg_ref[...]
            p_ref[8:PAD, :] = p_ref[TB + 8:TB + PAD, :]

        tail = []
        for c in range(n_chunks):
            tail += fillers[c:c + 1]
            tail.append(chain_step(c))
        tail += fillers[n_chunks:] + [finish]
        return front, chunk_stages, tail

    front, chunk_stages, tail = build()
    for thunk in front + chunk_stages + tail:
        thunk()


def _constants():
    hb = np.arange(GROUP) // HEAD
    bdmask = (hb[:, None] == hb[None, :]).astype(np.float32)
    ti = np.arange(CHUNK)[:, None]
    si = (np.arange(2 * GROUP) % HEAD)[None, :]
    trimask = np.concatenate([(ti > si), (ti >= si)], axis=0).astype(np.float32)
    sg = (np.arange(GROUP) % HEAD)[None, :]
    eye = (ti == sg).astype(np.float32)
    same = [(ti // (INV_BLOCK << m)) == (sg // (INV_BLOCK << m)) for m in range(N_MERGES + 1)]
    tmask = np.stack([same[0]] + [same[m] & ~same[m - 1] for m in range(1, N_MERGES + 1)],
                     axis=0).astype(np.float32)
    rr = np.arange(CUM_ROWS)
    same = (rr[:, None] // CHUNK) == (rr[None, :] // CHUNK)
    tril = same & (rr[None, :] <= rr[:, None])
    cmat = tril.astype(np.float32)
    return bdmask, trimask, eye, tmask, cmat


def kernel(x, c, w_ada, b_ada, norm_g, w_in, pool_w, pool_scale, mu_shift, w0, w_up, a0, a_up,
           k_k, k_a, r_k, ln_w, ln_b, w_out, final_g):
    B, T, _ = x.shape
    assert w_ada.shape[0] == 1 and T % TB == 0
    l = 0

    mod, w_in_p, w_out_b = _prologue(c, w_ada, b_ada, w_in, w_out, l)
    mu_p =jnp.concatenate([mu_shift[l, a - _S0:b - _S0] for a, b in W_IN_SPANS[1:]], axis=0)[None, :]

    bdmask, trimask, eye, tmask, cmat = _constants()
    bdmask_b = jnp.asarray(bdmask, BF16)
    segones = bdmask_b
    trimask = jnp.asarray(trimask)
    eye = jnp.asarray(eye)
    tmask = jnp.asarray(tmask)
    cmat = jnp.asarray(cmat, BF16)

    def full(a):
        nd = a.ndim
        return pl.BlockSpec(a.shape, lambda b, t, _nd=nd: (0,) * _nd)

    def layer(a):
        a = a.reshape(a.shape[0], 1, a.shape[1]) if a.ndim == 2 else a
        nd = a.ndim
        return a, pl.BlockSpec((None,) + a.shape[1:], lambda b, t, _nd=nd: (l,) + (0,) * (_nd - 1))

    operands = [
        (x, pl.BlockSpec((None, TB, D_MODEL), lambda b, t: (b, t, 0))),
        (mod, pl.BlockSpec((None, 3, D_MODEL), lambda b, t: (b, 0, 0))),
        layer(norm_g), (w_in_p, full(w_in_p)), (mu_p, full(mu_p)), layer(w_up), layer(a_up),
        layer(w0), layer(a0), layer(k_k), layer(k_a), layer(r_k), layer(ln_w), layer(ln_b),
        layer(pool_scale), layer(pool_w), (w_out_b, full(w_out_b)),
        (final_g[None, :], full(final_g[None, :])),
    ] + [(a, full(a)) for a in (bdmask_b, trimask, eye, tmask, cmat, segones)]
    in_specs = [spec for _, spec in operands]

    blk = lambda: pltpu.VMEM((TB, D_RWKV), F32)
    scratch = [
        pltpu.VMEM((TB + PAD, N_IN), F32),
        pltpu.VMEM((TB + PAD, D_POOL), F32),
        pltpu.VMEM((TB + PAD, D_POOL), F32),
        blk(), blk(), blk(), blk(), blk(), blk(),
        blk(), blk(),
        pltpu.VMEM((TB, D_MODEL), BF16),
        pltpu.VMEM((TB, D_MODEL), BF16),
        pltpu.VMEM((N_GROUPS, HEAD, GROUP), F32),
    ]
    return pl.pallas_call(
        _fused_kernel,
        grid=(B, T // TB),
        in_specs=in_specs,
        out_specs=pl.BlockSpec((None, TB, D_MODEL), lambda b, t: (b, t, 0)),
        out_shape=jax.ShapeDtypeStruct((B, T, D_MODEL), x.dtype),
        scratch_shapes=scratch,
        compiler_params=pltpu.CompilerParams(
            dimension_semantics=("arbitrary", "arbitrary"),
            vmem_limit_bytes=VMEM_LIMIT_BYTES),
        name="hybrid_block",
    )(*[a for a, _ in operands])
```

```python
import numpy as np
import jax
import jax.numpy as jnp
from jax import lax
from jax.experimental import pallas as pl
from jax.experimental.pallas import tpu as pltpu

F32 = jnp.float32
BF16 = jnp.bfloat16

D_MODEL = 1024
D_POOL = 512
D_RWKV = 512
HEAD = 64
LORA = 64
POOL_WINDOWS = (2, 4, 8, 16)
POOL_GROUP = 128
NORM_EPS = 1e-6
GN_EPS = 64e-5
L2_EPS = 1e-12

COL_U = 0
COL_PZ = 512
COL_R = 1024
COL_K = 1536
COL_V = 2048
COL_Z = 2560
COL_LO = 3072
N_IN = 3200
SEG0 = 1024

CHUNK = 64
GROUP = 256
HEADS_PER_GROUP = GROUP // HEAD
N_GROUPS = D_RWKV // GROUP
TB = 512
CUM_ROWS = 256
INV_BLOCK = 32
N_MERGES = (CHUNK // INV_BLOCK).bit_length() - 1
PAD = 24

VMEM_LIMIT_BYTES = 56 * 1024 * 1024


def _split_hi_lo(x):
    hi = x.astype(BF16)
    lo = (x - hi.astype(F32)).astype(BF16)
    return hi, lo


def _dot(a, b):
    return jnp.dot(a, b, preferred_element_type=F32)


def _sigmoid(x):
    return 0.5 * jnp.tanh(0.5 * x) + 0.5


ADA_ROWS = 128
LANES = 128


def _ada_accumulate(cb_ref, w_ref, b_ref, o_ref):
    nb, n_mod, d = o_ref.shape

    @pl.when(pl.program_id(0) == 0)
    def _():
        for m in range(n_mod):
            o_ref[:, m, :] = jnp.broadcast_to(b_ref[:, m * d:(m + 1) * d], (nb, d))

    w = w_ref[...]
    rows = []
    for b in range(nb):
        cb = cb_ref[b]
        cols = [jnp.sum(w[:, j:j + LANES] * cb, axis=0, keepdims=True)
                for j in range(0, w.shape[1], LANES)]
        rows.append(jnp.concatenate(cols, axis=1))
    acc = jnp.concatenate(rows, axis=0)
    for m in range(n_mod):
        o_ref[:, m, :] += acc[:, m * d:(m + 1) * d]


_S0 = 2 * D_POOL
W_IN_SPANS = ((0, _S0), (_S0, _S0 + 512), (_S0 + 576, _S0 + 1088), (_S0 + 1088, _S0 + 1600),
              (_S0 + 1664, _S0 + 2176), (_S0 + 512, _S0 + 576), (_S0 + 1600, _S0 + 1664))


def _prologue_kernel(cb_ref, wa_ref, b_ref, wi_ref, wo_ref, mod_ref, wip_ref, wob_ref):
    _ada_accumulate(cb_ref, wa_ref, b_ref, mod_ref)
    w = wi_ref[...]
    wip_ref[...] = jnp.concatenate([w[:, a:b] for a, b in W_IN_SPANS], axis=1).astype(BF16)
    wob_ref[...] = wo_ref[...].astype(BF16)


def _prologue(c, w_ada, b_ada, w_in, w_out, l):
    nb, n = c.shape[0], w_ada.shape[2]
    d, n_in = w_in.shape[1], w_in.shape[2]
    n_out = w_out.shape[2]
    cb = jnp.broadcast_to(c[:, :, None], (nb, D_MODEL, LANES))
    return pl.pallas_call(
        _prologue_kernel,
        grid=(D_MODEL // ADA_ROWS,),
        in_specs=[
            pl.BlockSpec((nb, ADA_ROWS, LANES), lambda j: (0, j, 0)),
            pl.BlockSpec((None, ADA_ROWS, n), lambda j: (l, j, 0)),
            pl.BlockSpec((None, 1, n), lambda j: (l, 0, 0)),
            pl.BlockSpec((None, ADA_ROWS, n_in), lambda j: (l, j, 0)),
            pl.BlockSpec((None, ADA_ROWS, n_out), lambda j: (l, j, 0)),
        ],
        out_specs=[
            pl.BlockSpec((nb, n // D_MODEL, D_MODEL), lambda j: (0, 0, 0)),
            pl.BlockSpec((ADA_ROWS, n_in), lambda j: (j, 0)),
            pl.BlockSpec((ADA_ROWS, n_out), lambda j: (j, 0)),
        ],
        out_shape=[
            jax.ShapeDtypeStruct((nb, n // D_MODEL, D_MODEL), F32),
            jax.ShapeDtypeStruct((d, n_in), BF16),
            jax.ShapeDtypeStruct((d, n_out), BF16),
        ],
        compiler_params=pltpu.CompilerParams(dimension_semantics=("arbitrary",)),
        name="prologue",
    )(cb, w_ada, b_ada.reshape(b_ada.shape[0], 1, n), w_in, w_out)


def _head_blocks(x, bdmask):
    xb = x.astype(BF16)
    heads_per_tile = LANES // HEAD
    zero_tile = jnp.zeros((x.shape[0], LANES), BF16)
    out = []
    for h in range(HEADS_PER_GROUP):
        tiles = [xb[:, j * LANES:(j + 1) * LANES] * bdmask[h * HEAD:(h + 1) * HEAD, j * LANES:(j + 1) * LANES]
                 if j == h // heads_per_tile else zero_tile
                 for j in range(GROUP // LANES)]
        out.append(jnp.concatenate(tiles, axis=1))
    return out


def _block_diag(x, bdmask):
    return jnp.concatenate(_head_blocks(x, bdmask), axis=0)


T_HEAD_ORDER = (0, 2, 1, 3)


def _heads_to_rows_t(x):
    xt = jnp.concatenate([x[:, 0:2 * HEAD], x[:, 2 * HEAD:4 * HEAD]], axis=0).T
    return jnp.concatenate([xt[0:HEAD, :], xt[HEAD:2 * HEAD, :]], axis=1)


def _diag2(m0, m1):
    z = jnp.zeros_like(m0)
    return jnp.concatenate([jnp.concatenate([m0, z], axis=1),
                            jnp.concatenate([z, m1], axis=1)], axis=0)


def _fused_kernel(x_ref, mod_ref, ng_ref, win_ref, mu_ref, wup_ref, aup_ref,
                  w0_ref, a0_ref, kk_ref, ka_ref, rk_ref, lnw_ref, lnb_ref, pscale_ref,
                  pw_ref, wout_ref, fg_ref, bdmask_ref, trimask_ref, eye_ref, tmask_ref, cmat_ref,
                  segones_ref, o_ref,
                  p_ref, sa_ref, sb_ref, at_ref, rt_ref, bt_ref, kt_ref, bdc_ref, kdc_ref,
                  v_ref, bon_ref, mix_ref, hb_ref, h_ref):
    t = pl.program_id(1)

    def normalise_into_hb(xin_ref):
        xb = xin_ref[...]
        ms = jnp.mean(xb * xb, axis=-1, keepdims=True)
        gs = ng_ref[...] * (1.0 + mod_ref[1:2, :])
        hb_ref[...] = (xb * lax.rsqrt(ms + NORM_EPS) * gs + mod_ref[0:1, :]).astype(BF16)

    @pl.when(t == 0)
    def _():
        p_ref[0:PAD, :] = jnp.zeros((PAD, N_IN), F32)
        sa_ref[0:8, :] = jnp.zeros((8, D_POOL), F32)
        sb_ref[0:8, :] = jnp.zeros((8, D_POOL), F32)
        h_ref[...] = jnp.zeros(h_ref.shape, F32)

    bdmask = bdmask_ref[...]
    eye = eye_ref[...]
    n_chunks = TB // CHUNK

    def blocks(xv):
        return _head_blocks(xv, bdmask)

    def bd(xv):
        return _block_diag(xv, bdmask)

    def bdt(blks):
        return jnp.concatenate([blks[h] for h in T_HEAD_ORDER], axis=0)

    def rows_of(c):
        return slice(c * CHUNK, (c + 1) * CHUNK)

    def lanes_of(g):
        return slice(GROUP * g, GROUP * (g + 1))

    def segsum(xb):
        parts = []
        for half in range(N_GROUPS):
            parts.append(_dot(xb[:, GROUP * half:GROUP * (half + 1)], segones_ref[...]))
        return jnp.concatenate(parts, axis=1)

    def build():
        xs_ref, os_ref = x_ref, o_ref
        gate = mod_ref[2:3, :]
        st = {}

        def project(col, width):
            p_ref[PAD:PAD + TB, col:col + width] = _dot(hb_ref[...], win_ref[:, col:col + width])

        def lerp(col, width, row0=0, nrows=TB):
            ext = p_ref[PAD + row0 - 8:PAD + row0 + nrows, col:col + width]
            cur = ext[8:, :]
            prev = pltpu.roll(ext, 1, axis=0)[8:, :]
            return cur + (prev - cur) * mu_ref[0:1, col - SEG0:col - SEG0 + width]

        def f_norm():
            normalise_into_hb(x_ref)
            project(COL_LO, 2 * LORA)
            project(COL_R, D_RWKV)

        def f_decay():
            lo = lerp(COL_LO, 2 * LORA)
            lane = lax.broadcasted_iota(jnp.int32, lo.shape, 1)
            lo = jnp.where(lane < LORA, jnp.tanh(lo), lo)
            lo_hi, lo_lo = _split_hi_lo(lo)
            lora = _diag2(wup_ref[...].astype(BF16), aup_ref[...].astype(BF16))
            lin = _dot(jnp.concatenate([lo_hi, lo_lo], axis=1), jnp.concatenate([lora, lora], axis=0))
            logw = -float(np.exp(-0.5) * np.log2(np.e)) * _sigmoid(w0_ref[...] + lin[:, 0:D_RWKV])
            st["a"] = _sigmoid(a0_ref[...] + lin[:, D_RWKV:2 * D_RWKV])
            lw_b = logw.astype(BF16)
            logw = lw_b.astype(F32)
            cum = jnp.concatenate(
                [_dot(cmat_ref[...], lw_b[rb:rb + CUM_ROWS, :]) for rb in range(0, TB, CUM_ROWS)], axis=0)
            cum_end = [cum[(c + 1) * CHUNK - 1:(c + 1) * CHUNK, :] for c in range(n_chunks)]
            st["p_chunk"] = [jnp.exp2(ce) for ce in cum_end]
            st["p_tot"] = jnp.concatenate(
                [jnp.broadcast_to(pc, (CHUNK, D_RWKV)) for pc in st["p_chunk"]], axis=0)
            st["cum"], st["logw"] = cum, logw
            project(COL_K, D_RWKV)

        def f_receptance():
            r = lerp(COL_R, D_RWKV)
            rt_ref[...] = r * jnp.exp2(st["cum"])
            bon_ref[...] = r * rk_ref[...]
            project(COL_V, D_RWKV)

        def store_transposed(ref, val):
            for c in range(n_chunks):
                for g in range(N_GROUPS):
                    ref[rows_of(c), lanes_of(g)] = _heads_to_rows_t(val[rows_of(c), lanes_of(g)])

        def f_key():
            a, cum = st["a"], st["cum"]
            k = lerp(COL_K, D_RWKV)
            kkr = k * kk_ref[...]
            ssq = segsum((kkr * kkr).astype(BF16))
            kk = kkr * lax.rsqrt(jnp.maximum(ssq, L2_EPS * L2_EPS))
            k2 = k * (a * ka_ref[...] + (1.0 - ka_ref[...]))
            p_inv = jnp.exp2(-cum)
            bt = kk * a * p_inv
            kt = k2 * p_inv
            store_transposed(bt_ref, bt)
            store_transposed(kt_ref, kt)
            store_transposed(bdc_ref, bt * st["p_tot"])
            store_transposed(kdc_ref, kt * st["p_tot"])
            at_ref[...] = -kk * jnp.exp2(cum - st["logw"])
            bon_ref[...] = segsum((bon_ref[...] * k2).astype(BF16)) * v_ref[...]

        def f_value():
            v_ref[...] = lerp(COL_V, D_RWKV)

        front = [f_norm, f_decay, f_receptance, f_value, f_key]

        inst = [(c, g) for c in range(n_chunks) for g in range(N_GROUPS)]
        lab, lak, mrb, mrk = {}, {}, {}, {}
        tinv, lpow, lakv, mrkv, kv, xm = {}, {}, {}, {}, {}, {}
        zt, q, y0, a_mat, g_mat = {}, {}, {}, {}, {}

        def st_scores(i):
            rows, lanes = rows_of(i[0]), lanes_of(i[1])
            ar = jnp.concatenate([at_ref[rows, lanes], rt_ref[rows, lanes]], axis=0).astype(BF16)
            rhs1 = jnp.concatenate([bdt(blocks(bt_ref[rows, lanes])),
                                    bdt(blocks(kt_ref[rows, lanes]))], axis=1)
            sc = _dot(ar, rhs1) * trimask_ref[...]
            lab[i] = sc[0:CHUNK, 0:GROUP]
            lak[i] = sc[0:CHUNK, GROUP:2 * GROUP]
            mrb[i] = sc[CHUNK:2 * CHUNK, 0:GROUP]
            mrk[i] = sc[CHUNK:2 * CHUNK, GROUP:2 * GROUP]

        def st_inv_start(i):
            rows, lanes = rows_of(i[0]), lanes_of(i[1])
            ldiag = lab[i] * tmask_ref[0]
            tinv[i] = eye + ldiag
            lpow[i] = _dot(ldiag.astype(BF16), bd(ldiag))
            xv = _dot(jnp.concatenate([lak[i], mrk[i], kdc_ref[rows, lanes]], axis=0).astype(BF16),
                      bdt(blocks(v_ref[rows, lanes])))
            lakv[i] = xv[0:CHUNK, :]
            mrkv[i] = xv[CHUNK:2 * CHUNK, :]
            kv[i] = xv[2 * CHUNK:3 * CHUNK, :]

        def st_neumann(i):
            res = _dot(jnp.concatenate([lpow[i], tinv[i]], axis=0).astype(BF16), bd(lpow[i]))
            lpow[i] = res[0:CHUNK, :]
            tinv[i] = tinv[i] + res[CHUNK:2 * CHUNK, :]

        def st_neumann_last(i):
            tinv[i] = tinv[i] + _dot(tinv[i].astype(BF16), bd(lpow[i]))

        def st_merge_x(level):
            def stage(i):
                xm[i] = _dot((lab[i] * tmask_ref[level]).astype(BF16), bd(tinv[i]))
            return stage

        def st_merge_t(i):
            tinv[i] = tinv[i] + _dot(tinv[i].astype(BF16), bd(xm[i]))

        def st_last_merge_x(i):
            rows, lanes = rows_of(i[0]), lanes_of(i[1])
            res = _dot(jnp.concatenate([lab[i] * tmask_ref[N_MERGES], mrb[i], bdc_ref[rows, lanes]],
                                       axis=0).astype(BF16),
                       bd(tinv[i]))
            xm[i] = res[0:CHUNK, :]
            zt[i] = res[CHUNK:3 * CHUNK, :]

        def st_last_merge_t(i):
            zt[i] = zt[i] + _dot(zt[i].astype(BF16), bd(xm[i]))

        def st_transition(i):
            rows, lanes = rows_of(i[0]), lanes_of(i[1])
            res = _dot(zt[i].astype(BF16),
                       jnp.concatenate([bdt(blocks(at_ref[rows, lanes])), bdt(blocks(lakv[i]))], axis=1))
            q[i] = rt_ref[rows, lanes] + res[0:CHUNK, 0:GROUP]
            y0[i] = res[0:CHUNK, GROUP:2 * GROUP] + mrkv[i]
            a_mat[i] = eye * st["p_chunk"][i[0]][:, lanes] + res[CHUNK:2 * CHUNK, 0:GROUP]
            g_mat[i] = res[CHUNK:2 * CHUNK, GROUP:2 * GROUP] + kv[i]

        stage_fns = ([st_scores, st_inv_start] + [st_neumann] * (INV_BLOCK.bit_length() - 3)
                     + [st_neumann_last])
        for level in range(1, N_MERGES):
            stage_fns += [st_merge_x(level), st_merge_t]
        stage_fns += [st_last_merge_x, st_last_merge_t, st_transition]

        def run_stage(fn):
            def thunk():
                for i in inst:
                    fn(i)
            return thunk

        chunk_stages = [run_stage(fn) for fn in stage_fns]

        y_chunks = {}

        def chain_step(c):
            def thunk():
                if c == 0:
                    st["hs"] = [h_ref[g] for g in range(N_GROUPS)]
                hs, ys = st["hs"], []
                for g in range(N_GROUPS):
                    i = (c, g)
                    res = _dot(jnp.concatenate([a_mat[i], q[i]], axis=0).astype(BF16), bd(hs[g]))
                    hs[g] = res[0:CHUNK, :] + g_mat[i]
                    ys.append(res[CHUNK:2 * CHUNK, :] + y0[i])
                y_chunks[c] = jnp.concatenate(ys, axis=1)
                if c == n_chunks - 1:
                    for g in range(N_GROUPS):
                        h_ref[g] = hs[g]
            return thunk

        diffs = []

        def pool_windows():
            n_ext = TB + PAD
            sa_ref[8:n_ext, :] = p_ref[8:n_ext, 0:512] + p_ref[7:n_ext - 1, 0:512]
            sb_ref[8:n_ext, 128:512] = sa_ref[8:n_ext, 128:512] + sa_ref[6:n_ext - 2, 128:512]
            sa_ref[8:n_ext, 256:512] = sb_ref[8:n_ext, 256:512] + sb_ref[4:n_ext - 4, 256:512]
            sb_ref[8:n_ext, 384:512] = sa_ref[8:n_ext, 384:512] + sa_ref[0:n_ext - 8, 384:512]

        def pool_diffs():
            pos = t * TB + lax.broadcasted_iota(jnp.int32, (TB, 1), 0) + 1
            wsum_refs = (sa_ref, sb_ref, sa_ref, sb_ref)
            for g, win in enumerate(POOL_WINDOWS):
                lanes = slice(g * POOL_GROUP, (g + 1) * POOL_GROUP)
                cnt = jnp.minimum(pos, win).astype(F32)
                mean = wsum_refs[g][PAD:PAD + TB, lanes] / cnt
                diffs.append(mean - p_ref[PAD:PAD + TB, lanes])

        def pool_out(half):
            lanes = slice(256 * half, 256 * half + 256)
            d2 = jnp.concatenate(diffs[2 * half:2 * half + 2], axis=1).astype(BF16)
            yp = _dot(d2, _diag2(pw_ref[2 * half].astype(BF16), pw_ref[2 * half + 1].astype(BF16)))
            zz = p_ref[PAD:PAD + TB, COL_PZ + 256 * half:COL_PZ + 256 * half + 256]
            yp = yp * pscale_ref[:, lanes] * (zz * _sigmoid(zz))
            mix_ref[:, lanes] = yp.astype(BF16)

        half_w = D_POOL // 2
        fillers = [
            lambda: project(COL_U, half_w),
            lambda: project(COL_U + half_w, half_w),
            lambda: (project(COL_PZ, half_w), pool_windows()),
            lambda: (project(COL_PZ + half_w, half_w), pool_diffs()),
            lambda: (project(COL_Z, half_w), pool_out(0)),
            lambda: (project(COL_Z + half_w, half_w), pool_out(1)),
        ]

        def finish():
            yh = jnp.concatenate([y_chunks[c] for c in range(n_chunks)], axis=0)
            mu = segsum(yh.astype(BF16)) * (1.0 / HEAD)
            dlt = yh - mu
            var = segsum((dlt * dlt).astype(BF16)) * (1.0 / HEAD)
            yn = dlt * lax.rsqrt(var + GN_EPS) * lnw_ref[...] + lnb_ref[...]
            z = lerp(COL_Z, D_RWKV)
            y_rwkv = (yn + bon_ref[...]) * (z * _sigmoid(z))
            mix_ref[:, D_POOL:D_POOL + D_RWKV] = y_rwkv.astype(BF16)
            out = _dot(mix_ref[...], wout_ref[...])
            xo = xs_ref[...] + gate * out
            ms2 = jnp.mean(xo * xo, axis=-1, keepdims=True)
            os_ref[...] = xo * lax.rsqrt(ms2 + NORM_EPS) * fg_ref[...]
            p_ref[8:PAD, :] = p_ref[TB + 8:TB + PAD, :]

        tail = []
        for c in range(n_chunks):
            tail += fillers[c:c + 1]
            tail.append(chain_step(c))
        tail += fillers[n_chunks:] + [finish]
        return front, chunk_stages, tail

    front, chunk_stages, tail = build()
    for thunk in front + chunk_stages + tail:
        thunk()


def _constants():
    hb = np.arange(GROUP) // HEAD
    bdmask = (hb[:, None] == hb[None, :]).astype(np.float32)
    ti = np.arange(CHUNK)[:, None]
    si = (np.arange(2 * GROUP) % HEAD)[None, :]
    trimask = np.concatenate([(ti > si), (ti >= si)], axis=0).astype(np.float32)
    sg = (np.arange(GROUP) % HEAD)[None, :]
    eye = (ti == sg).astype(np.float32)
    same = [(ti // (INV_BLOCK << m)) == (sg // (INV_BLOCK << m)) for m in range(N_MERGES + 1)]
    tmask = np.stack([same[0]] + [same[m] & ~same[m - 1] for m in range(1, N_MERGES + 1)],
                     axis=0).astype(np.float32)
    rr = np.arange(CUM_ROWS)
    same = (rr[:, None] // CHUNK) == (rr[None, :] // CHUNK)
    tril = same & (rr[None, :] <= rr[:, None])
    cmat = tril.astype(np.float32)
    return bdmask, trimask, eye, tmask, cmat


def kernel(x, c, w_ada, b_ada, norm_g, w_in, pool_w, pool_scale, mu_shift, w0, w_up, a0, a_up,
           k_k, k_a, r_k, ln_w, ln_b, w_out, final_g):
    B, T, _ = x.shape
    assert w_ada.shape[0] == 1 and T % TB == 0
    l = 0

    mod, w_in_p, w_out_b = _prologue(c, w_ada, b_ada, w_in, w_out, l)
    mu_p =jnp.concatenate([mu_shift[l, a - _S0:b - _S0] for a, b in W_IN_SPANS[1:]], axis=0)[None, :]

    bdmask, trimask, eye, tmask, cmat = _constants()
    bdmask_b = jnp.asarray(bdmask, BF16)
    segones = bdmask_b
    trimask = jnp.asarray(trimask)
    eye = jnp.asarray(eye)
    tmask = jnp.asarray(tmask)
    cmat = jnp.asarray(cmat, BF16)

    def full(a):
        nd = a.ndim
        return pl.BlockSpec(a.shape, lambda b, t, _nd=nd: (0,) * _nd)

    def layer(a):
        a = a.reshape(a.shape[0], 1, a.shape[1]) if a.ndim == 2 else a
        nd = a.ndim
        return a, pl.BlockSpec((None,) + a.shape[1:], lambda b, t, _nd=nd: (l,) + (0,) * (_nd - 1))

    operands = [
        (x, pl.BlockSpec((None, TB, D_MODEL), lambda b, t: (b, t, 0))),
        (mod, pl.BlockSpec((None, 3, D_MODEL), lambda b, t: (b, 0, 0))),
        layer(norm_g), (w_in_p, full(w_in_p)), (mu_p, full(mu_p)), layer(w_up), layer(a_up),
        layer(w0), layer(a0), layer(k_k), layer(k_a), layer(r_k), layer(ln_w), layer(ln_b),
        layer(pool_scale), layer(pool_w), (w_out_b, full(w_out_b)),
        (final_g[None, :], full(final_g[None, :])),
    ] + [(a, full(a)) for a in (bdmask_b, trimask, eye, tmask, cmat, segones)]
    in_specs = [spec for _, spec in operands]

    blk = lambda: pltpu.VMEM((TB, D_RWKV), F32)
    scratch = [
        pltpu.VMEM((TB + PAD, N_IN), F32),
        pltpu.VMEM((TB + PAD, D_POOL), F32),
        pltpu.VMEM((TB + PAD, D_POOL), F32),
        blk(), blk(), blk(), blk(), blk(), blk(),
        blk(), blk(),
        pltpu.VMEM((TB, D_MODEL), BF16),
        pltpu.VMEM((TB, D_MODEL), BF16),
        pltpu.VMEM((N_GROUPS, HEAD, GROUP), F32),
    ]
    return pl.pallas_call(
        _fused_kernel,
        grid=(B, T // TB),
        in_specs=in_specs,
        out_specs=pl.BlockSpec((None, TB, D_MODEL), lambda b, t: (b, t, 0)),
        out_shape=jax.ShapeDtypeStruct((B, T, D_MODEL), x.dtype),
        scratch_shapes=scratch,
        compiler_params=pltpu.CompilerParams(
            dimension_semantics=("arbitrary", "arbitrary"),
            vmem_limit_bytes=VMEM_LIMIT_BYTES),
        name="hybrid_block",
    )(*[a for a, _ in operands])
```

```python
import numpy as np
import jax
import jax.numpy as jnp
from jax import lax
from jax.experimental import pallas as pl
from jax.experimental.pallas import tpu as pltpu

F32 = jnp.float32
BF16 = jnp.bfloat16

D_MODEL = 1024
D_POOL = 512
D_RWKV = 512
HEAD = 64
LORA = 64
POOL_WINDOWS = (2, 4, 8, 16)
POOL_GROUP = 128
NORM_EPS = 1e-6
GN_EPS = 64e-5
L2_EPS = 1e-12

COL_U = 0
COL_PZ = 512
COL_R = 1024
COL_K = 1536
COL_V = 2048
COL_Z = 2560
COL_LO = 3072
N_IN = 3200
SEG0 = 1024

CHUNK = 64
GROUP = 256
HEADS_PER_GROUP = GROUP // HEAD
N_GROUPS = D_RWKV // GROUP
TB = 512
CUM_ROWS = 256
INV_BLOCK = 32
N_MERGES = (CHUNK // INV_BLOCK).bit_length() - 1
PAD = 24

VMEM_LIMIT_BYTES = 56 * 1024 * 1024


def _split_hi_lo(x):
    hi = x.astype(BF16)
    lo = (x - hi.astype(F32)).astype(BF16)
    return hi, lo


def _dot(a, b):
    return jnp.dot(a, b, preferred_element_type=F32)


def _sigmoid(x):
    return 0.5 * jnp.tanh(0.5 * x) + 0.5


ADA_ROWS = 128
LANES = 128


def _ada_accumulate(cb_ref, w_ref, b_ref, o_ref):
    nb, n_mod, d = o_ref.shape

    @pl.when(pl.program_id(0) == 0)
    def _():
        for m in range(n_mod):
            o_ref[:, m, :] = jnp.broadcast_to(b_ref[:, m * d:(m + 1) * d], (nb, d))

    w = w_ref[...]
    rows = []
    for b in range(nb):
        cb = cb_ref[b]
        cols = [jnp.sum(w[:, j:j + LANES] * cb, axis=0, keepdims=True)
                for j in range(0, w.shape[1], LANES)]
        rows.append(jnp.concatenate(cols, axis=1))
    acc = jnp.concatenate(rows, axis=0)
    for m in range(n_mod):
        o_ref[:, m, :] += acc[:, m * d:(m + 1) * d]


_S0 = 2 * D_POOL
W_IN_SPANS = ((0, _S0), (_S0, _S0 + 512), (_S0 + 576, _S0 + 1088), (_S0 + 1088, _S0 + 1600),
              (_S0 + 1664, _S0 + 2176), (_S0 + 512, _S0 + 576), (_S0 + 1600, _S0 + 1664))


def _prologue_kernel(cb_ref, wa_ref, b_ref, wi_ref, wo_ref, mod_ref, wip_ref, wob_ref):
    _ada_accumulate(cb_ref, wa_ref, b_ref, mod_ref)
    w = wi_ref[...]
    wip_ref[...] = jnp.concatenate([w[:, a:b] for a, b in W_IN_SPANS], axis=1).astype(BF16)
    wob_ref[...] = wo_ref[...].astype(BF16)


def _prologue(c, w_ada, b_ada, w_in, w_out, l):
    nb, n = c.shape[0], w_ada.shape[2]
    d, n_in = w_in.shape[1], w_in.shape[2]
    n_out = w_out.shape[2]
    cb = jnp.broadcast_to(c[:, :, None], (nb, D_MODEL, LANES))
    return pl.pallas_call(
        _prologue_kernel,
        grid=(D_MODEL // ADA_ROWS,),
        in_specs=[
            pl.BlockSpec((nb, ADA_ROWS, LANES), lambda j: (0, j, 0)),
            pl.BlockSpec((None, ADA_ROWS, n), lambda j: (l, j, 0)),
            pl.BlockSpec((None, 1, n), lambda j: (l, 0, 0)),
            pl.BlockSpec((None, ADA_ROWS, n_in), lambda j: (l, j, 0)),
            pl.BlockSpec((None, ADA_ROWS, n_out), lambda j: (l, j, 0)),
        ],
        out_specs=[
            pl.BlockSpec((nb, n // D_MODEL, D_MODEL), lambda j: (0, 0, 0)),
            pl.BlockSpec((ADA_ROWS, n_in), lambda j: (j, 0)),
            pl.BlockSpec((ADA_ROWS, n_out), lambda j: (j, 0)),
        ],
        out_shape=[
            jax.ShapeDtypeStruct((nb, n // D_MODEL, D_MODEL), F32),
            jax.ShapeDtypeStruct((d, n_in), BF16),
            jax.ShapeDtypeStruct((d, n_out), BF16),
        ],
        compiler_params=pltpu.CompilerParams(dimension_semantics=("arbitrary",)),
        name="prologue",
    )(cb, w_ada, b_ada.reshape(b_ada.shape[0], 1, n), w_in, w_out)


def _head_blocks(x, bdmask):
    xb = x.astype(BF16)
    heads_per_tile = LANES // HEAD
    zero_tile = jnp.zeros((x.shape[0], LANES), BF16)
    out = []
    for h in range(HEADS_PER_GROUP):
        tiles = [xb[:, j * LANES:(j + 1) * LANES] * bdmask[h * HEAD:(h + 1) * HEAD, j * LANES:(j + 1) * LANES]
                 if j == h // heads_per_tile else zero_tile
                 for j in range(GROUP // LANES)]
        out.append(jnp.concatenate(tiles, axis=1))
    return out


def _block_diag(x, bdmask):
    return jnp.concatenate(_head_blocks(x, bdmask), axis=0)


T_HEAD_ORDER = (0, 2, 1, 3)


def _heads_to_rows_t(x):
    xt = jnp.concatenate([x[:, 0:2 * HEAD], x[:, 2 * HEAD:4 * HEAD]], axis=0).T
    return jnp.concatenate([xt[0:HEAD, :], xt[HEAD:2 * HEAD, :]], axis=1)


def _diag2(m0, m1):
    z = jnp.zeros_like(m0)
    return jnp.concatenate([jnp.concatenate([m0, z], axis=1),
                            jnp.concatenate([z, m1], axis=1)], axis=0)


def _fused_kernel(x_ref, mod_ref, ng_ref, win_ref, mu_ref, wup_ref, aup_ref,
                  w0_ref, a0_ref, kk_ref, ka_ref, rk_ref, lnw_ref, lnb_ref, pscale_ref,
                  pw_ref, wout_ref, fg_ref, bdmask_ref, trimask_ref, eye_ref, tmask_ref, cmat_ref,
                  segones_ref, o_ref,
                  p_ref, sa_ref, sb_ref, at_ref, rt_ref, bt_ref, kt_ref, bdc_ref, kdc_ref,
                  v_ref, bon_ref, mix_ref, hb_ref, h_ref):
    t = pl.program_id(1)

    def normalise_into_hb(xin_ref):
        xb = xin_ref[...]
        ms = jnp.mean(xb * xb, axis=-1, keepdims=True)
        gs = ng_ref[...] * (1.0 + mod_ref[1:2, :])
        hb_ref[...] = (xb * lax.rsqrt(ms + NORM_EPS) * gs + mod_ref[0:1, :]).astype(BF16)

    @pl.when(t == 0)
    def _():
        p_ref[0:PAD, :] = jnp.zeros((PAD, N_IN), F32)
        sa_ref[0:8, :] = jnp.zeros((8, D_POOL), F32)
        sb_ref[0:8, :] = jnp.zeros((8, D_POOL), F32)
        h_ref[...] = jnp.zeros(h_ref.shape, F32)

    bdmask = bdmask_ref[...]
    eye = eye_ref[...]
    n_chunks = TB // CHUNK

    def blocks(xv):
        return _head_blocks(xv, bdmask)

    def bd(xv):
        return _block_diag(xv, bdmask)

    def bdt(blks):
        return jnp.concatenate([blks[h] for h in T_HEAD_ORDER], axis=0)

    def rows_of(c):
        return slice(c * CHUNK, (c + 1) * CHUNK)

    def lanes_of(g):
        return slice(GROUP * g, GROUP * (g + 1))

    def segsum(xb):
        parts = []
        for half in range(N_GROUPS):
            parts.append(_dot(xb[:, GROUP * half:GROUP * (half + 1)], segones_ref[...]))
        return jnp.concatenate(parts, axis=1)

    def build():
        xs_ref, os_ref = x_ref, o_ref
        gate = mod_ref[2:3, :]
        st = {}

        def project(col, width):
            p_ref[PAD:PAD + TB, col:col + width] = _dot(hb_ref[...], win_ref[:, col:col + width])

        def lerp(col, width, row0=0, nrows=TB):
            ext = p_ref[PAD + row0 - 8:PAD + row0 + nrows, col:col + width]
            cur = ext[8:, :]
            prev = pltpu.roll(ext, 1, axis=0)[8:, :]
            return cur + (prev - cur) * mu_ref[0:1, col - SEG0:col - SEG0 + width]

        def f_norm():
            normalise_into_hb(x_ref)
            project(COL_LO, 2 * LORA)
            project(COL_R, D_RWKV)

        def f_decay():
            lo = lerp(COL_LO, 2 * LORA)
            lane = lax.broadcasted_iota(jnp.int32, lo.shape, 1)
            lo = jnp.where(lane < LORA, jnp.tanh(lo), lo)
            lo_hi, lo_lo = _split_hi_lo(lo)
            lora = _diag2(wup_ref[...].astype(BF16), aup_ref[...].astype(BF16))
            lin = _dot(jnp.concatenate([lo_hi, lo_lo], axis=1), jnp.concatenate([lora, lora], axis=0))
            logw = -float(np.exp(-0.5) * np.log2(np.e)) * _sigmoid(w0_ref[...] + lin[:, 0:D_RWKV])
            st["a"] = _sigmoid(a0_ref[...] + lin[:, D_RWKV:2 * D_RWKV])
            lw_b = logw.astype(BF16)
            logw = lw_b.astype(F32)
            cum = jnp.concatenate(
                [_dot(cmat_ref[...], lw_b[rb:rb + CUM_ROWS, :]) for rb in range(0, TB, CUM_ROWS)], axis=0)
            cum_end = [cum[(c + 1) * CHUNK - 1:(c + 1) * CHUNK, :] for c in range(n_chunks)]
            st["p_chunk"] = [jnp.exp2(ce) for ce in cum_end]
            st["p_tot"] = jnp.concatenate(
                [jnp.broadcast_to(pc, (CHUNK, D_RWKV)) for pc in st["p_chunk"]], axis=0)
            st["cum"], st["logw"] = cum, logw
            project(COL_K, D_RWKV)

        def f_receptance():
            r = lerp(COL_R, D_RWKV)
            rt_ref[...] = r * jnp.exp2(st["cum"])
            bon_ref[...] = r * rk_ref[...]
            project(COL_V, D_RWKV)

        def store_transposed(ref, val):
            for c in range(n_chunks):
                for g in range(N_GROUPS):
                    ref[rows_of(c), lanes_of(g)] = _heads_to_rows_t(val[rows_of(c), lanes_of(g)])

        def f_key():
            a, cum = st["a"], st["cum"]
            k = lerp(COL_K, D_RWKV)
            kkr = k * kk_ref[...]
            ssq = segsum((kkr * kkr).astype(BF16))
            kk = kkr * lax.rsqrt(jnp.maximum(ssq, L2_EPS * L2_EPS))
            k2 = k * (a * ka_ref[...] + (1.0 - ka_ref[...]))
            p_inv = jnp.exp2(-cum)
            bt = kk * a * p_inv
            kt = k2 * p_inv
            store_transposed(bt_ref, bt)
            store_transposed(kt_ref, kt)
            store_transposed(bdc_ref, bt * st["p_tot"])
            store_transposed(kdc_ref, kt * st["p_tot"])
            at_ref[...] = -kk * jnp.exp2(cum - st["logw"])
            bon_ref[...] = segsum((bon_ref[...] * k2).astype(BF16)) * v_ref[...]

        def f_value():
            v_ref[...] = lerp(COL_V, D_RWKV)

        front = [f_norm, f_decay, f_receptance, f_value, f_key]

        inst = [(c, g) for g in range(N_GROUPS) for c in range(n_chunks)]
        lab, lak, mrb, mrk = {}, {}, {}, {}
        tinv, lpow, lakv, mrkv, kv, xm = {}, {}, {}, {}, {}, {}
        zt, q, y0, a_mat, g_mat = {}, {}, {}, {}, {}

        def st_scores(i):
            rows, lanes = rows_of(i[0]), lanes_of(i[1])
            ar = jnp.concatenate([at_ref[rows, lanes], rt_ref[rows, lanes]], axis=0).astype(BF16)
            rhs1 = jnp.concatenate([bdt(blocks(bt_ref[rows, lanes])),
                                    bdt(blocks(kt_ref[rows, lanes]))], axis=1)
            sc = _dot(ar, rhs1) * trimask_ref[...]
            lab[i] = sc[0:CHUNK, 0:GROUP]
            lak[i] = sc[0:CHUNK, GROUP:2 * GROUP]
            mrb[i] = sc[CHUNK:2 * CHUNK, 0:GROUP]
            mrk[i] = sc[CHUNK:2 * CHUNK, GROUP:2 * GROUP]

        def st_inv_start(i):
            rows, lanes = rows_of(i[0]), lanes_of(i[1])
            ldiag = lab[i] * tmask_ref[0]
            tinv[i] = eye + ldiag
            lpow[i] = _dot(ldiag.astype(BF16), bd(ldiag))
            xv = _dot(jnp.concatenate([lak[i], mrk[i], kdc_ref[rows, lanes]], axis=0).astype(BF16),
                      bdt(blocks(v_ref[rows, lanes])))
            lakv[i] = xv[0:CHUNK, :]
            mrkv[i] = xv[CHUNK:2 * CHUNK, :]
            kv[i] = xv[2 * CHUNK:3 * CHUNK, :]

        def st_neumann(i):
            res = _dot(jnp.concatenate([lpow[i], tinv[i]], axis=0).astype(BF16), bd(lpow[i]))
            lpow[i] = res[0:CHUNK, :]
            tinv[i] = tinv[i] + res[CHUNK:2 * CHUNK, :]

        def st_neumann_last(i):
            tinv[i] = tinv[i] + _dot(tinv[i].astype(BF16), bd(lpow[i]))

        def st_merge_x(level):
            def stage(i):
                xm[i] = _dot((lab[i] * tmask_ref[level]).astype(BF16), bd(tinv[i]))
            return stage

        def st_merge_t(i):
            tinv[i] = tinv[i] + _dot(tinv[i].astype(BF16), bd(xm[i]))

        def st_last_merge_x(i):
            rows, lanes = rows_of(i[0]), lanes_of(i[1])
            res = _dot(jnp.concatenate([lab[i] * tmask_ref[N_MERGES], mrb[i], bdc_ref[rows, lanes]],
                                       axis=0).astype(BF16),
                       bd(tinv[i]))
            xm[i] = res[0:CHUNK, :]
            zt[i] = res[CHUNK:3 * CHUNK, :]

        def st_last_merge_t(i):
            zt[i] = zt[i] + _dot(zt[i].astype(BF16), bd(xm[i]))

        def st_transition(i):
            rows, lanes = rows_of(i[0]), lanes_of(i[1])
            res = _dot(zt[i].astype(BF16),
                       jnp.concatenate([bdt(blocks(at_ref[rows, lanes])), bdt(blocks(lakv[i]))], axis=1))
            q[i] = rt_ref[rows, lanes] + res[0:CHUNK, 0:GROUP]
            y0[i] = res[0:CHUNK, GROUP:2 * GROUP] + mrkv[i]
            a_mat[i] = eye * st["p_chunk"][i[0]][:, lanes] + res[CHUNK:2 * CHUNK, 0:GROUP]
            g_mat[i] = res[CHUNK:2 * CHUNK, GROUP:2 * GROUP] + kv[i]

        stage_fns = ([st_scores, st_inv_start] + [st_neumann] * (INV_BLOCK.bit_length() - 3)
                     + [st_neumann_last])
        for level in range(1, N_MERGES):
            stage_fns += [st_merge_x(level), st_merge_t]
        stage_fns += [st_last_merge_x, st_last_merge_t, st_transition]

        def run_stage(fn):
            def thunk():
                for i in inst:
                    fn(i)
            return thunk

        chunk_stages = [run_stage(fn) for fn in stage_fns]

        y_chunks = {}

        def chain_step(c):
            def thunk():
                if c == 0:
                    st["hs"] = [h_ref[g] for g in range(N_GROUPS)]
                hs, ys = st["hs"], []
                for g in range(N_GROUPS):
                    i = (c, g)
                    res = _dot(jnp.concatenate([a_mat[i], q[i]], axis=0).astype(BF16), bd(hs[g]))
                    hs[g] = res[0:CHUNK, :] + g_mat[i]
                    ys.append(res[CHUNK:2 * CHUNK, :] + y0[i])
                y_chunks[c] = jnp.concatenate(ys, axis=1)
                if c == n_chunks - 1:
                    for g in range(N_GROUPS):
                        h_ref[g] = hs[g]
            return thunk

        diffs = []

        def pool_windows():
            n_ext = TB + PAD
            sa_ref[8:n_ext, :] = p_ref[8:n_ext, 0:512] + p_ref[7:n_ext - 1, 0:512]
            sb_ref[8:n_ext, 128:512] = sa_ref[8:n_ext, 128:512] + sa_ref[6:n_ext - 2, 128:512]
            sa_ref[8:n_ext, 256:512] = sb_ref[8:n_ext, 256:512] + sb_ref[4:n_ext - 4, 256:512]
            sb_ref[8:n_ext, 384:512] = sa_ref[8:n_ext, 384:512] + sa_ref[0:n_ext - 8, 384:512]

        def pool_diffs():
            pos = t * TB + lax.broadcasted_iota(jnp.int32, (TB, 1), 0) + 1
            wsum_refs = (sa_ref, sb_ref, sa_ref, sb_ref)
            for g, win in enumerate(POOL_WINDOWS):
                lanes = slice(g * POOL_GROUP, (g + 1) * POOL_GROUP)
                cnt = jnp.minimum(pos, win).astype(F32)
                mean = wsum_refs[g][PAD:PAD + TB, lanes] / cnt
                diffs.append(mean - p_ref[PAD:PAD + TB, lanes])

        def pool_out(half):
            lanes = slice(256 * half, 256 * half + 256)
            d2 = jnp.concatenate(diffs[2 * half:2 * half + 2], axis=1).astype(BF16)
            yp = _dot(d2, _diag2(pw_ref[2 * half].astype(BF16), pw_ref[2 * half + 1].astype(BF16)))
            zz = p_ref[PAD:PAD + TB, COL_PZ + 256 * half:COL_PZ + 256 * half + 256]
            yp = yp * pscale_ref[:, lanes] * (zz * _sigmoid(zz))
            mix_ref[:, lanes] = yp.astype(BF16)

        half_w = D_POOL // 2
        fillers = [
            lambda: project(COL_U, half_w),
            lambda: project(COL_U + half_w, half_w),
            lambda: (project(COL_PZ, half_w), pool_windows()),
            lambda: (project(COL_PZ + half_w, half_w), pool_diffs()),
            lambda: (project(COL_Z, half_w), pool_out(0)),
            lambda: (project(COL_Z + half_w, half_w), pool_out(1)),
        ]

        def finish():
            yh = jnp.concatenate([y_chunks[c] for c in range(n_chunks)], axis=0)
            mu = segsum(yh.astype(BF16)) * (1.0 / HEAD)
            dlt = yh - mu
            var = segsum((dlt * dlt).astype(BF16)) * (1.0 / HEAD)
            yn = dlt * lax.rsqrt(var + GN_EPS) * lnw_ref[...] + lnb_ref[...]
            z = lerp(COL_Z, D_RWKV)
            y_rwkv = (yn + bon_ref[...]) * (z * _sigmoid(z))
            mix_ref[:, D_POOL:D_POOL + D_RWKV] = y_rwkv.astype(BF16)
            out = _dot(mix_ref[...], wout_ref[...])
            xo = xs_ref[...] + gate * out
            ms2 = jnp.mean(xo * xo, axis=-1, keepdims=True)
            os_ref[...] = xo * lax.rsqrt(ms2 + NORM_EPS) * fg_ref[...]
            p_ref[8:PAD, :] = p_ref[TB + 8:TB + PAD, :]

        tail = []
        for c in range(n_chunks):
            tail += fillers[c:c + 1]
            tail.append(chain_step(c))
        tail += fillers[n_chunks:] + [finish]
        return front, chunk_stages, tail

    front, chunk_stages, tail = build()
    for thunk in front + chunk_stages + tail:
        thunk()


def _constants():
    hb = np.arange(GROUP) // HEAD
    bdmask = (hb[:, None] == hb[None, :]).astype(np.float32)
    ti = np.arange(CHUNK)[:, None]
    si = (np.arange(2 * GROUP) % HEAD)[None, :]
    trimask = np.concatenate([(ti > si), (ti >= si)], axis=0).astype(np.float32)
    sg = (np.arange(GROUP) % HEAD)[None, :]
    eye = (ti == sg).astype(np.float32)
    same = [(ti // (INV_BLOCK << m)) == (sg // (INV_BLOCK << m)) for m in range(N_MERGES + 1)]
    tmask = np.stack([same[0]] + [same[m] & ~same[m - 1] for m in range(1, N_MERGES + 1)],
                     axis=0).astype(np.float32)
    rr = np.arange(CUM_ROWS)
    same = (rr[:, None] // CHUNK) == (rr[None, :] // CHUNK)
    tril = same & (rr[None, :] <= rr[:, None])
    cmat = tril.astype(np.float32)
    return bdmask, trimask, eye, tmask, cmat


def kernel(x, c, w_ada, b_ada, norm_g, w_in, pool_w, pool_scale, mu_shift, w0, w_up, a0, a_up,
           k_k, k_a, r_k, ln_w, ln_b, w_out, final_g):
    B, T, _ = x.shape
    assert w_ada.shape[0] == 1 and T % TB == 0
    l = 0

    mod, w_in_p, w_out_b = _prologue(c, w_ada, b_ada, w_in, w_out, l)
    mu_p =jnp.concatenate([mu_shift[l, a - _S0:b - _S0] for a, b in W_IN_SPANS[1:]], axis=0)[None, :]

    bdmask, trimask, eye, tmask, cmat = _constants()
    bdmask_b = jnp.asarray(bdmask, BF16)
    segones = bdmask_b
    trimask = jnp.asarray(trimask)
    eye = jnp.asarray(eye)
    tmask = jnp.asarray(tmask)
    cmat = jnp.asarray(cmat, BF16)

    def full(a):
        nd = a.ndim
        return pl.BlockSpec(a.shape, lambda b, t, _nd=nd: (0,) * _nd)

    def layer(a):
        a = a.reshape(a.shape[0], 1, a.shape[1]) if a.ndim == 2 else a
        nd = a.ndim
        return a, pl.BlockSpec((None,) + a.shape[1:], lambda b, t, _nd=nd: (l,) + (0,) * (_nd - 1))

    operands = [
        (x, pl.BlockSpec((None, TB, D_MODEL), lambda b, t: (b, t, 0))),
        (mod, pl.BlockSpec((None, 3, D_MODEL), lambda b, t: (b, 0, 0))),
        layer(norm_g), (w_in_p, full(w_in_p)), (mu_p, full(mu_p)), layer(w_up), layer(a_up),
        layer(w0), layer(a0), layer(k_k), layer(k_a), layer(r_k), layer(ln_w), layer(ln_b),
        layer(pool_scale), layer(pool_w), (w_out_b, full(w_out_b)),
        (final_g[None, :], full(final_g[None, :])),
    ] + [(a, full(a)) for a in (bdmask_b, trimask, eye, tmask, cmat, segones)]
    in_specs = [spec for _, spec in operands]

    blk = lambda: pltpu.VMEM((TB, D_RWKV), F32)
    scratch = [
        pltpu.VMEM((TB + PAD, N_IN), F32),
        pltpu.VMEM((TB + PAD, D_POOL), F32),
        pltpu.VMEM((TB + PAD, D_POOL), F32),
        blk(), blk(), blk(), blk(), blk(), blk(),
        blk(), blk(),
        pltpu.VMEM((TB, D_MODEL), BF16),
        pltpu.VMEM((TB, D_MODEL), BF16),
        pltpu.VMEM((N_GROUPS, HEAD, GROUP), F32),
    ]
    return pl.pallas_call(
        _fused_kernel,
        grid=(B, T // TB),
        in_specs=in_specs,
        out_specs=pl.BlockSpec((None, TB, D_MODEL), lambda b, t: (b, t, 0)),
        out_shape=jax.ShapeDtypeStruct((B, T, D_MODEL), x.dtype),
        scratch_shapes=scratch,
        compiler_params=pltpu.CompilerParams(
            dimension_semantics=("arbitrary", "arbitrary"),
            vmem_limit_bytes=VMEM_LIMIT_BYTES),
        name="hybrid_block",
    )(*[a for a, _ in operands])
```

```python
import numpy as np
import jax
import jax.numpy as jnp
from jax import lax
from jax.experimental import pallas as pl
from jax.experimental.pallas import tpu as pltpu

F32 = jnp.float32
BF16 = jnp.bfloat16

D_MODEL = 1024
D_POOL = 512
D_RWKV = 512
HEAD = 64
LORA = 64
POOL_WINDOWS = (2, 4, 8, 16)
POOL_GROUP = 128
NORM_EPS = 1e-6
GN_EPS = 64e-5
L2_EPS = 1e-12

COL_U = 0
COL_PZ = 512
COL_R = 1024
COL_K = 1536
COL_V = 2048
COL_Z = 2560
COL_LO = 3072
N_IN = 3200
SEG0 = 1024

CHUNK = 64
GROUP = 256
HEADS_PER_GROUP = GROUP // HEAD
N_GROUPS = D_RWKV // GROUP
TB = 512
CUM_ROWS = 256
INV_BLOCK = 32
N_MERGES = (CHUNK // INV_BLOCK).bit_length() - 1
PAD = 24

VMEM_LIMIT_BYTES = 56 * 1024 * 1024


def _split_hi_lo(x):
    hi = x.astype(BF16)
    lo = (x - hi.astype(F32)).astype(BF16)
    return hi, lo


def _dot(a, b):
    return jnp.dot(a, b, preferred_element_type=F32)


def _sigmoid(x):
    return 0.5 * jnp.tanh(0.5 * x) + 0.5


ADA_ROWS = 256
LANES = 128


def _ada_accumulate(cb_ref, w_ref, b_ref, o_ref):
    nb, n_mod, d = o_ref.shape

    @pl.when(pl.program_id(0) == 0)
    def _():
        for m in range(n_mod):
            o_ref[:, m, :] = jnp.broadcast_to(b_ref[:, m * d:(m + 1) * d], (nb, d))

    w = w_ref[...]
    rows = []
    for b in range(nb):
        cb = cb_ref[b]
        cols = [jnp.sum(w[:, j:j + LANES] * cb, axis=0, keepdims=True)
                for j in range(0, w.shape[1], LANES)]
        rows.append(jnp.concatenate(cols, axis=1))
    acc = jnp.concatenate(rows, axis=0)
    for m in range(n_mod):
        o_ref[:, m, :] += acc[:, m * d:(m + 1) * d]


_S0 = 2 * D_POOL
W_IN_SPANS = ((0, _S0), (_S0, _S0 + 512), (_S0 + 576, _S0 + 1088), (_S0 + 1088, _S0 + 1600),
              (_S0 + 1664, _S0 + 2176), (_S0 + 512, _S0 + 576), (_S0 + 1600, _S0 + 1664))


def _prologue_kernel(cb_ref, wa_ref, b_ref, wi_ref, wo_ref, mod_ref, wip_ref, wob_ref):
    _ada_accumulate(cb_ref, wa_ref, b_ref, mod_ref)
    w = wi_ref[...]
    wip_ref[...] = jnp.concatenate([w[:, a:b] for a, b in W_IN_SPANS], axis=1).astype(BF16)
    wob_ref[...] = wo_ref[...].astype(BF16)


def _prologue(c, w_ada, b_ada, w_in, w_out, l):
    nb, n = c.shape[0], w_ada.shape[2]
    d, n_in = w_in.shape[1], w_in.shape[2]
    n_out = w_out.shape[2]
    cb = jnp.broadcast_to(c[:, :, None], (nb, D_MODEL, LANES))
    return pl.pallas_call(
        _prologue_kernel,
        grid=(D_MODEL // ADA_ROWS,),
        in_specs=[
            pl.BlockSpec((nb, ADA_ROWS, LANES), lambda j: (0, j, 0)),
            pl.BlockSpec((None, ADA_ROWS, n), lambda j: (l, j, 0)),
            pl.BlockSpec((None, 1, n), lambda j: (l, 0, 0)),
            pl.BlockSpec((None, ADA_ROWS, n_in), lambda j: (l, j, 0)),
            pl.BlockSpec((None, ADA_ROWS, n_out), lambda j: (l, j, 0)),
        ],
        out_specs=[
            pl.BlockSpec((nb, n // D_MODEL, D_MODEL), lambda j: (0, 0, 0)),
            pl.BlockSpec((ADA_ROWS, n_in), lambda j: (j, 0)),
            pl.BlockSpec((ADA_ROWS, n_out), lambda j: (j, 0)),
        ],
        out_shape=[
            jax.ShapeDtypeStruct((nb, n // D_MODEL, D_MODEL), F32),
            jax.ShapeDtypeStruct((d, n_in), BF16),
            jax.ShapeDtypeStruct((d, n_out), BF16),
        ],
        compiler_params=pltpu.CompilerParams(dimension_semantics=("arbitrary",)),
        name="prologue",
    )(cb, w_ada, b_ada.reshape(b_ada.shape[0], 1, n), w_in, w_out)


def _head_blocks(x, bdmask):
    xb = x.astype(BF16)
    heads_per_tile = LANES // HEAD
    zero_tile = jnp.zeros((x.shape[0], LANES), BF16)
    out = []
    for h in range(HEADS_PER_GROUP):
        tiles = [xb[:, j * LANES:(j + 1) * LANES] * bdmask[h * HEAD:(h + 1) * HEAD, j * LANES:(j + 1) * LANES]
                 if j == h // heads_per_tile else zero_tile
                 for j in range(GROUP // LANES)]
        out.append(jnp.concatenate(tiles, axis=1))
    return out


def _block_diag(x, bdmask):
    return jnp.concatenate(_head_blocks(x, bdmask), axis=0)


T_HEAD_ORDER = (0, 2, 1, 3)


def _heads_to_rows_t(x):
    xt = jnp.concatenate([x[:, 0:2 * HEAD], x[:, 2 * HEAD:4 * HEAD]], axis=0).T
    return jnp.concatenate([xt[0:HEAD, :], xt[HEAD:2 * HEAD, :]], axis=1)


def _diag2(m0, m1):
    z = jnp.zeros_like(m0)
    return jnp.concatenate([jnp.concatenate([m0, z], axis=1),
                            jnp.concatenate([z, m1], axis=1)], axis=0)


def _fused_kernel(x_ref, mod_ref, ng_ref, win_ref, mu_ref, wup_ref, aup_ref,
                  w0_ref, a0_ref, kk_ref, ka_ref, rk_ref, lnw_ref, lnb_ref, pscale_ref,
                  pw_ref, wout_ref, fg_ref, bdmask_ref, trimask_ref, eye_ref, tmask_ref, cmat_ref,
                  segones_ref, o_ref,
                  p_ref, sa_ref, sb_ref, at_ref, rt_ref, bt_ref, kt_ref, bdc_ref, kdc_ref,
                  v_ref, bon_ref, mix_ref, hb_ref, h_ref):
    t = pl.program_id(1)

    def normalise_into_hb(xin_ref):
        xb = xin_ref[...]
        ms = jnp.mean(xb * xb, axis=-1, keepdims=True)
        gs = ng_ref[...] * (1.0 + mod_ref[1:2, :])
        hb_ref[...] = (xb * lax.rsqrt(ms + NORM_EPS) * gs + mod_ref[0:1, :]).astype(BF16)

    @pl.when(t == 0)
    def _():
        p_ref[0:PAD, :] = jnp.zeros((PAD, N_IN), F32)
        sa_ref[0:8, :] = jnp.zeros((8, D_POOL), F32)
        sb_ref[0:8, :] = jnp.zeros((8, D_POOL), F32)
        h_ref[...] = jnp.zeros(h_ref.shape, F32)

    bdmask = bdmask_ref[...]
    eye = eye_ref[...]
    n_chunks = TB // CHUNK

    def blocks(xv):
        return _head_blocks(xv, bdmask)

    def bd(xv):
        return _block_diag(xv, bdmask)

    def bdt(blks):
        return jnp.concatenate([blks[h] for h in T_HEAD_ORDER], axis=0)

    def rows_of(c):
        return slice(c * CHUNK, (c + 1) * CHUNK)

    def lanes_of(g):
        return slice(GROUP * g, GROUP * (g + 1))

    def segsum(xb):
        parts = []
        for half in range(N_GROUPS):
            parts.append(_dot(xb[:, GROUP * half:GROUP * (half + 1)], segones_ref[...]))
        return jnp.concatenate(parts, axis=1)

    def build():
        xs_ref, os_ref = x_ref, o_ref
        gate = mod_ref[2:3, :]
        st = {}

        def project(col, width):
            p_ref[PAD:PAD + TB, col:col + width] = _dot(hb_ref[...], win_ref[:, col:col + width])

        def lerp(col, width, row0=0, nrows=TB):
            ext = p_ref[PAD + row0 - 8:PAD + row0 + nrows, col:col + width]
            cur = ext[8:, :]
            prev = pltpu.roll(ext, 1, axis=0)[8:, :]
            return cur + (prev - cur) * mu_ref[0:1, col - SEG0:col - SEG0 + width]

        def f_norm():
            normalise_into_hb(x_ref)
            project(COL_LO, 2 * LORA)
            project(COL_R, D_RWKV)

        def f_decay():
            lo = lerp(COL_LO, 2 * LORA)
            lane = lax.broadcasted_iota(jnp.int32, lo.shape, 1)
            lo = jnp.where(lane < LORA, jnp.tanh(lo), lo)
            lo_hi, lo_lo = _split_hi_lo(lo)
            lora = _diag2(wup_ref[...].astype(BF16), aup_ref[...].astype(BF16))
            lin = _dot(jnp.concatenate([lo_hi, lo_lo], axis=1), jnp.concatenate([lora, lora], axis=0))
            logw = -float(np.exp(-0.5) * np.log2(np.e)) * _sigmoid(w0_ref[...] + lin[:, 0:D_RWKV])
            st["a"] = _sigmoid(a0_ref[...] + lin[:, D_RWKV:2 * D_RWKV])
            lw_b = logw.astype(BF16)
            logw = lw_b.astype(F32)
            cum = jnp.concatenate(
                [_dot(cmat_ref[...], lw_b[rb:rb + CUM_ROWS, :]) for rb in range(0, TB, CUM_ROWS)], axis=0)
            cum_end = [cum[(c + 1) * CHUNK - 1:(c + 1) * CHUNK, :] for c in range(n_chunks)]
            st["p_chunk"] = [jnp.exp2(ce) for ce in cum_end]
            st["p_tot"] = jnp.concatenate(
                [jnp.broadcast_to(pc, (CHUNK, D_RWKV)) for pc in st["p_chunk"]], axis=0)
            st["cum"], st["logw"] = cum, logw
            project(COL_K, D_RWKV)

        def f_receptance():
            r = lerp(COL_R, D_RWKV)
            rt_ref[...] = r * jnp.exp2(st["cum"])
            bon_ref[...] = r * rk_ref[...]
            project(COL_V, D_RWKV)

        def store_transposed(ref, val):
            for c in range(n_chunks):
                for g in range(N_GROUPS):
                    ref[rows_of(c), lanes_of(g)] = _heads_to_rows_t(val[rows_of(c), lanes_of(g)])

        def f_key():
            a, cum = st["a"], st["cum"]
            k = lerp(COL_K, D_RWKV)
            kkr = k * kk_ref[...]
            ssq = segsum((kkr * kkr).astype(BF16))
            kk = kkr * lax.rsqrt(jnp.maximum(ssq, L2_EPS * L2_EPS))
            k2 = k * (a * ka_ref[...] + (1.0 - ka_ref[...]))
            p_inv = jnp.exp2(-cum)
            bt = kk * a * p_inv
            kt = k2 * p_inv
            store_transposed(bt_ref, bt)
            store_transposed(kt_ref, kt)
            store_transposed(bdc_ref, bt * st["p_tot"])
            store_transposed(kdc_ref, kt * st["p_tot"])
            at_ref[...] = -kk * jnp.exp2(cum - st["logw"])
            bon_ref[...] = segsum((bon_ref[...] * k2).astype(BF16)) * v_ref[...]

        def f_value():
            v_ref[...] = lerp(COL_V, D_RWKV)

        front = [f_norm, f_decay, f_receptance, f_value, f_key]

        inst = [(c, g) for g in range(N_GROUPS) for c in range(n_chunks)]
        lab, lak, mrb, mrk = {}, {}, {}, {}
        tinv, lpow, lakv, mrkv, kv, xm = {}, {}, {}, {}, {}, {}
        zt, q, y0, a_mat, g_mat = {}, {}, {}, {}, {}

        def st_scores(i):
            rows, lanes = rows_of(i[0]), lanes_of(i[1])
            ar = jnp.concatenate([at_ref[rows, lanes], rt_ref[rows, lanes]], axis=0).astype(BF16)
            rhs1 = jnp.concatenate([bdt(blocks(bt_ref[rows, lanes])),
                                    bdt(blocks(kt_ref[rows, lanes]))], axis=1)
            sc = _dot(ar, rhs1) * trimask_ref[...]
            lab[i] = sc[0:CHUNK, 0:GROUP]
            lak[i] = sc[0:CHUNK, GROUP:2 * GROUP]
            mrb[i] = sc[CHUNK:2 * CHUNK, 0:GROUP]
            mrk[i] = sc[CHUNK:2 * CHUNK, GROUP:2 * GROUP]

        def st_inv_start(i):
            rows, lanes = rows_of(i[0]), lanes_of(i[1])
            ldiag = lab[i] * tmask_ref[0]
            tinv[i] = eye + ldiag
            lpow[i] = _dot(ldiag.astype(BF16), bd(ldiag))
            xv = _dot(jnp.concatenate([lak[i], mrk[i], kdc_ref[rows, lanes]], axis=0).astype(BF16),
                      bdt(blocks(v_ref[rows, lanes])))
            lakv[i] = xv[0:CHUNK, :]
            mrkv[i] = xv[CHUNK:2 * CHUNK, :]
            kv[i] = xv[2 * CHUNK:3 * CHUNK, :]

        def st_neumann(i):
            res = _dot(jnp.concatenate([lpow[i], tinv[i]], axis=0).astype(BF16), bd(lpow[i]))
            lpow[i] = res[0:CHUNK, :]
            tinv[i] = tinv[i] + res[CHUNK:2 * CHUNK, :]

        def st_neumann_last(i):
            tinv[i] = tinv[i] + _dot(tinv[i].astype(BF16), bd(lpow[i]))

        def st_merge_x(level):
            def stage(i):
                xm[i] = _dot((lab[i] * tmask_ref[level]).astype(BF16), bd(tinv[i]))
            return stage

        def st_merge_t(i):
            tinv[i] = tinv[i] + _dot(tinv[i].astype(BF16), bd(xm[i]))

        def st_last_merge_x(i):
            rows, lanes = rows_of(i[0]), lanes_of(i[1])
            res = _dot(jnp.concatenate([lab[i] * tmask_ref[N_MERGES], mrb[i], bdc_ref[rows, lanes]],
                                       axis=0).astype(BF16),
                       bd(tinv[i]))
            xm[i] = res[0:CHUNK, :]
            zt[i] = res[CHUNK:3 * CHUNK, :]

        def st_last_merge_t(i):
            zt[i] = zt[i] + _dot(zt[i].astype(BF16), bd(xm[i]))

        def st_transition(i):
            rows, lanes = rows_of(i[0]), lanes_of(i[1])
            res = _dot(zt[i].astype(BF16),
                       jnp.concatenate([bdt(blocks(at_ref[rows, lanes])), bdt(blocks(lakv[i]))], axis=1))
            q[i] = rt_ref[rows, lanes] + res[0:CHUNK, 0:GROUP]
            y0[i] = res[0:CHUNK, GROUP:2 * GROUP] + mrkv[i]
            a_mat[i] = eye * st["p_chunk"][i[0]][:, lanes] + res[CHUNK:2 * CHUNK, 0:GROUP]
            g_mat[i] = res[CHUNK:2 * CHUNK, GROUP:2 * GROUP] + kv[i]

        stage_fns = ([st_scores, st_inv_start] + [st_neumann] * (INV_BLOCK.bit_length() - 3)
                     + [st_neumann_last])
        for level in range(1, N_MERGES):
            stage_fns += [st_merge_x(level), st_merge_t]
        stage_fns += [st_last_merge_x, st_last_merge_t, st_transition]

        def run_stage(fn):
            def thunk():
                for i in inst:
                    fn(i)
            return thunk

        chunk_stages = [run_stage(fn) for fn in stage_fns]

        y_chunks = {}

        def chain_step(c):
            def thunk():
                if c == 0:
                    st["hs"] = [h_ref[g] for g in range(N_GROUPS)]
                hs, ys = st["hs"], []
                for g in range(N_GROUPS):
                    i = (c, g)
                    res = _dot(jnp.concatenate([a_mat[i], q[i]], axis=0).astype(BF16), bd(hs[g]))
                    hs[g] = res[0:CHUNK, :] + g_mat[i]
                    ys.append(res[CHUNK:2 * CHUNK, :] + y0[i])
                y_chunks[c] = jnp.concatenate(ys, axis=1)
                if c == n_chunks - 1:
                    for g in range(N_GROUPS):
                        h_ref[g] = hs[g]
            return thunk

        diffs = []

        def pool_windows():
            n_ext = TB + PAD
            sa_ref[8:n_ext, :] = p_ref[8:n_ext, 0:512] + p_ref[7:n_ext - 1, 0:512]
            sb_ref[8:n_ext, 128:512] = sa_ref[8:n_ext, 128:512] + sa_ref[6:n_ext - 2, 128:512]
            sa_ref[8:n_ext, 256:512] = sb_ref[8:n_ext, 256:512] + sb_ref[4:n_ext - 4, 256:512]
            sb_ref[8:n_ext, 384:512] = sa_ref[8:n_ext, 384:512] + sa_ref[0:n_ext - 8, 384:512]

        def pool_diffs():
            pos = t * TB + lax.broadcasted_iota(jnp.int32, (TB, 1), 0) + 1
            wsum_refs = (sa_ref, sb_ref, sa_ref, sb_ref)
            for g, win in enumerate(POOL_WINDOWS):
                lanes = slice(g * POOL_GROUP, (g + 1) * POOL_GROUP)
                cnt = jnp.minimum(pos, win).astype(F32)
                mean = wsum_refs[g][PAD:PAD + TB, lanes] / cnt
                diffs.append(mean - p_ref[PAD:PAD + TB, lanes])

        def pool_out(half):
            lanes = slice(256 * half, 256 * half + 256)
            d2 = jnp.concatenate(diffs[2 * half:2 * half + 2], axis=1).astype(BF16)
            yp = _dot(d2, _diag2(pw_ref[2 * half].astype(BF16), pw_ref[2 * half + 1].astype(BF16)))
            zz = p_ref[PAD:PAD + TB, COL_PZ + 256 * half:COL_PZ + 256 * half + 256]
            yp = yp * pscale_ref[:, lanes] * (zz * _sigmoid(zz))
            mix_ref[:, lanes] = yp.astype(BF16)

        half_w = D_POOL // 2
        fillers = [
            lambda: project(COL_U, half_w),
            lambda: project(COL_U + half_w, half_w),
            lambda: (project(COL_PZ, half_w), pool_windows()),
            lambda: (project(COL_PZ + half_w, half_w), pool_diffs()),
            lambda: (project(COL_Z, half_w), pool_out(0)),
            lambda: (project(COL_Z + half_w, half_w), pool_out(1)),
        ]

        def finish():
            yh = jnp.concatenate([y_chunks[c] for c in range(n_chunks)], axis=0)
            mu = segsum(yh.astype(BF16)) * (1.0 / HEAD)
            dlt = yh - mu
            var = segsum((dlt * dlt).astype(BF16)) * (1.0 / HEAD)
            yn = dlt * lax.rsqrt(var + GN_EPS) * lnw_ref[...] + lnb_ref[...]
            z = lerp(COL_Z, D_RWKV)
            y_rwkv = (yn + bon_ref[...]) * (z * _sigmoid(z))
            mix_ref[:, D_POOL:D_POOL + D_RWKV] = y_rwkv.astype(BF16)
            out = _dot(mix_ref[...], wout_ref[...])
            xo = xs_ref[...] + gate * out
            ms2 = jnp.mean(xo * xo, axis=-1, keepdims=True)
            os_ref[...] = xo * lax.rsqrt(ms2 + NORM_EPS) * fg_ref[...]
            p_ref[8:PAD, :] = p_ref[TB + 8:TB + PAD, :]

        tail = []
        for c in range(n_chunks):
            tail += fillers[c:c + 1]
            tail.append(chain_step(c))
        tail += fillers[n_chunks:] + [finish]
        return front, chunk_stages, tail

    front, chunk_stages, tail = build()
    for thunk in front + chunk_stages + tail:
        thunk()


def _constants():
    hb = np.arange(GROUP) // HEAD
    bdmask = (hb[:, None] == hb[None, :]).astype(np.float32)
    ti = np.arange(CHUNK)[:, None]
    si = (np.arange(2 * GROUP) % HEAD)[None, :]
    trimask = np.concatenate([(ti > si), (ti >= si)], axis=0).astype(np.float32)
    sg = (np.arange(GROUP) % HEAD)[None, :]
    eye = (ti == sg).astype(np.float32)
    same = [(ti // (INV_BLOCK << m)) == (sg // (INV_BLOCK << m)) for m in range(N_MERGES + 1)]
    tmask = np.stack([same[0]] + [same[m] & ~same[m - 1] for m in range(1, N_MERGES + 1)],
                     axis=0).astype(np.float32)
    rr = np.arange(CUM_ROWS)
    same = (rr[:, None] // CHUNK) == (rr[None, :] // CHUNK)
    tril = same & (rr[None, :] <= rr[:, None])
    cmat = tril.astype(np.float32)
    return bdmask, trimask, eye, tmask, cmat


def kernel(x, c, w_ada, b_ada, norm_g, w_in, pool_w, pool_scale, mu_shift, w0, w_up, a0, a_up,
           k_k, k_a, r_k, ln_w, ln_b, w_out, final_g):
    B, T, _ = x.shape
    assert w_ada.shape[0] == 1 and T % TB == 0
    l = 0

    mod, w_in_p, w_out_b = _prologue(c, w_ada, b_ada, w_in, w_out, l)
    mu_p =jnp.concatenate([mu_shift[l, a - _S0:b - _S0] for a, b in W_IN_SPANS[1:]], axis=0)[None, :]

    bdmask, trimask, eye, tmask, cmat = _constants()
    bdmask_b = jnp.asarray(bdmask, BF16)
    segones = bdmask_b
    trimask = jnp.asarray(trimask)
    eye = jnp.asarray(eye)
    tmask = jnp.asarray(tmask)
    cmat = jnp.asarray(cmat, BF16)

    def full(a):
        nd = a.ndim
        return pl.BlockSpec(a.shape, lambda b, t, _nd=nd: (0,) * _nd)

    def layer(a):
        a = a.reshape(a.shape[0], 1, a.shape[1]) if a.ndim == 2 else a
        nd = a.ndim
        return a, pl.BlockSpec((None,) + a.shape[1:], lambda b, t, _nd=nd: (l,) + (0,) * (_nd - 1))

    operands = [
        (x, pl.BlockSpec((None, TB, D_MODEL), lambda b, t: (b, t, 0))),
        (mod, pl.BlockSpec((None, 3, D_MODEL), lambda b, t: (b, 0, 0))),
        layer(norm_g), (w_in_p, full(w_in_p)), (mu_p, full(mu_p)), layer(w_up), layer(a_up),
        layer(w0), layer(a0), layer(k_k), layer(k_a), layer(r_k), layer(ln_w), layer(ln_b),
        layer(pool_scale), layer(pool_w), (w_out_b, full(w_out_b)),
        (final_g[None, :], full(final_g[None, :])),
    ] + [(a, full(a)) for a in (bdmask_b, trimask, eye, tmask, cmat, segones)]
    in_specs = [spec for _, spec in operands]

    blk = lambda: pltpu.VMEM((TB, D_RWKV), F32)
    scratch = [
        pltpu.VMEM((TB + PAD, N_IN), F32),
        pltpu.VMEM((TB + PAD, D_POOL), F32),
        pltpu.VMEM((TB + PAD, D_POOL), F32),
        blk(), blk(), blk(), blk(), blk(), blk(),
        blk(), blk(),
        pltpu.VMEM((TB, D_MODEL), BF16),
        pltpu.VMEM((TB, D_MODEL), BF16),
        pltpu.VMEM((N_GROUPS, HEAD, GROUP), F32),
    ]
    return pl.pallas_call(
        _fused_kernel,
        grid=(B, T // TB),
        in_specs=in_specs,
        out_specs=pl.BlockSpec((None, TB, D_MODEL), lambda b, t: (b, t, 0)),
        out_shape=jax.ShapeDtypeStruct((B, T, D_MODEL), x.dtype),
        scratch_shapes=scratch,
        compiler_params=pltpu.CompilerParams(
            dimension_semantics=("arbitrary", "arbitrary"),
            vmem_limit_bytes=VMEM_LIMIT_BYTES),
        name="hybrid_block",
    )(*[a for a, _ in operands])
```

```python
import numpy as np
import jax
import jax.numpy as jnp
from jax import lax
from jax.experimental import pallas as pl
from jax.experimental.pallas import tpu as pltpu

F32 = jnp.float32
BF16 = jnp.bfloat16

D_MODEL = 1024
D_POOL = 512
D_RWKV = 512
HEAD = 64
LORA = 64
POOL_WINDOWS = (2, 4, 8, 16)
POOL_GROUP = 128
NORM_EPS = 1e-6
GN_EPS = 64e-5
L2_EPS = 1e-12

COL_U = 0
COL_PZ = 512
COL_R = 1024
COL_K = 1536
COL_V = 2048
COL_Z = 2560
COL_LO = 3072
N_IN = 3200
SEG0 = 1024

CHUNK = 64
GROUP = 256
HEADS_PER_GROUP = GROUP // HEAD
N_GROUPS = D_RWKV // GROUP
TB = 512
CUM_ROWS = 256
INV_BLOCK = 32
N_MERGES = (CHUNK // INV_BLOCK).bit_length() - 1
PAD = 24

VMEM_LIMIT_BYTES = 56 * 1024 * 1024


def _split_hi_lo(x):
    hi = x.astype(BF16)
    lo = (x - hi.astype(F32)).astype(BF16)
    return hi, lo


def _dot(a, b):
    return jnp.dot(a, b, preferred_element_type=F32)


def _sigmoid(x):
    return 0.5 * jnp.tanh(0.5 * x) + 0.5


ADA_ROWS = 256
LANES = 128


def _ada_accumulate(cb_ref, w_ref, b_ref, o_ref):
    nb, n_mod, d = o_ref.shape

    @pl.when(pl.program_id(0) == 0)
    def _():
        for m in range(n_mod):
            o_ref[:, m, :] = jnp.broadcast_to(b_ref[:, m * d:(m + 1) * d], (nb, d))

    w = w_ref[...]
    rows = []
    for b in range(nb):
        cb = cb_ref[b]
        cols = [jnp.sum(w[:, j:j + LANES] * cb, axis=0, keepdims=True)
                for j in range(0, w.shape[1], LANES)]
        rows.append(jnp.concatenate(cols, axis=1))
    acc = jnp.concatenate(rows, axis=0)
    for m in range(n_mod):
        o_ref[:, m, :] += acc[:, m * d:(m + 1) * d]


_S0 = 2 * D_POOL
W_IN_SPANS = ((0, _S0), (_S0, _S0 + 512), (_S0 + 576, _S0 + 1088), (_S0 + 1088, _S0 + 1600),
              (_S0 + 1664, _S0 + 2176), (_S0 + 512, _S0 + 576), (_S0 + 1600, _S0 + 1664))


def _prologue_kernel(cb_ref, wa_ref, b_ref, wi_ref, wo_ref, mod_ref, wip_ref, wob_ref):
    _ada_accumulate(cb_ref, wa_ref, b_ref, mod_ref)
    w = wi_ref[...]
    wip_ref[...] = jnp.concatenate([w[:, a:b] for a, b in W_IN_SPANS], axis=1).astype(BF16)
    wob_ref[...] = wo_ref[...].astype(BF16)


def _prologue(c, w_ada, b_ada, w_in, w_out, l):
    nb, n = c.shape[0], w_ada.shape[2]
    d, n_in = w_in.shape[1], w_in.shape[2]
    n_out = w_out.shape[2]
    cb = jnp.broadcast_to(c[:, :, None], (nb, D_MODEL, LANES))
    return pl.pallas_call(
        _prologue_kernel,
        grid=(D_MODEL // ADA_ROWS,),
        in_specs=[
            pl.BlockSpec((nb, ADA_ROWS, LANES), lambda j: (0, j, 0)),
            pl.BlockSpec((None, ADA_ROWS, n), lambda j: (l, j, 0)),
            pl.BlockSpec((None, 1, n), lambda j: (l, 0, 0)),
            pl.BlockSpec((None, ADA_ROWS, n_in), lambda j: (l, j, 0)),
            pl.BlockSpec((None, ADA_ROWS, n_out), lambda j: (l, j, 0)),
        ],
        out_specs=[
            pl.BlockSpec((nb, n // D_MODEL, D_MODEL), lambda j: (0, 0, 0)),
            pl.BlockSpec((ADA_ROWS, n_in), lambda j: (j, 0)),
            pl.BlockSpec((ADA_ROWS, n_out), lambda j: (j, 0)),
        ],
        out_shape=[
            jax.ShapeDtypeStruct((nb, n // D_MODEL, D_MODEL), F32),
            jax.ShapeDtypeStruct((d, n_in), BF16),
            jax.ShapeDtypeStruct((d, n_out), BF16),
        ],
        compiler_params=pltpu.CompilerParams(dimension_semantics=("arbitrary",)),
        name="prologue",
    )(cb, w_ada, b_ada.reshape(b_ada.shape[0], 1, n), w_in, w_out)


def _head_blocks(x, bdmask):
    xb = x.astype(BF16)
    heads_per_tile = LANES // HEAD
    zero_tile = jnp.zeros((x.shape[0], LANES), BF16)
    out = []
    for h in range(HEADS_PER_GROUP):
        tiles = [xb[:, j * LANES:(j + 1) * LANES] * bdmask[h * HEAD:(h + 1) * HEAD, j * LANES:(j + 1) * LANES]
                 if j == h // heads_per_tile else zero_tile
                 for j in range(GROUP // LANES)]
        out.append(jnp.concatenate(tiles, axis=1))
    return out


def _block_diag(x, bdmask):
    return jnp.concatenate(_head_blocks(x, bdmask), axis=0)


T_HEAD_ORDER = (0, 2, 1, 3)


def _heads_to_rows_t(x):
    xt = jnp.concatenate([x[:, 0:2 * HEAD], x[:, 2 * HEAD:4 * HEAD]], axis=0).T
    return jnp.concatenate([xt[0:HEAD, :], xt[HEAD:2 * HEAD, :]], axis=1)


def _diag2(m0, m1):
    z = jnp.zeros_like(m0)
    return jnp.concatenate([jnp.concatenate([m0, z], axis=1),
                            jnp.concatenate([z, m1], axis=1)], axis=0)


def _fused_kernel(x_ref, mod_ref, ng_ref, win_ref, mu_ref, wup_ref, aup_ref,
                  w0_ref, a0_ref, kk_ref, ka_ref, rk_ref, lnw_ref, lnb_ref, pscale_ref,
                  pw_ref, wout_ref, fg_ref, bdmask_ref, trimask_ref, eye_ref, tmask_ref, cmat_ref,
                  segones_ref, o_ref,
                  p_ref, sa_ref, sb_ref, at_ref, rt_ref, bt_ref, kt_ref, bdc_ref, kdc_ref,
                  v_ref, bon_ref, mix_ref, hb_ref, h_ref):
    t = pl.program_id(1)

    def normalise_into_hb(xin_ref):
        xb = xin_ref[...]
        ms = jnp.mean(xb * xb, axis=-1, keepdims=True)
        gs = ng_ref[...] * (1.0 + mod_ref[1:2, :])
        hb_ref[...] = (xb * lax.rsqrt(ms + NORM_EPS) * gs + mod_ref[0:1, :]).astype(BF16)

    @pl.when(t == 0)
    def _():
        p_ref[0:PAD, :] = jnp.zeros((PAD, N_IN), F32)
        sa_ref[0:8, :] = jnp.zeros((8, D_POOL), F32)
        sb_ref[0:8, :] = jnp.zeros((8, D_POOL), F32)
        h_ref[...] = jnp.zeros(h_ref.shape, F32)

    bdmask = bdmask_ref[...]
    eye = eye_ref[...]
    n_chunks = TB // CHUNK

    def blocks(xv):
        return _head_blocks(xv, bdmask)

    def bd(xv):
        return _block_diag(xv, bdmask)

    def bdt(blks):
        return jnp.concatenate([blks[h] for h in T_HEAD_ORDER], axis=0)

    def rows_of(c):
        return slice(c * CHUNK, (c + 1) * CHUNK)

    def lanes_of(g):
        return slice(GROUP * g, GROUP * (g + 1))

    def segsum(xb):
        parts = []
        for half in range(N_GROUPS):
            parts.append(_dot(xb[:, GROUP * half:GROUP * (half + 1)], segones_ref[...]))
        return jnp.concatenate(parts, axis=1)

    def build():
        xs_ref, os_ref = x_ref, o_ref
        gate = mod_ref[2:3, :]
        st = {}

        def project(col, width):
            p_ref[PAD:PAD + TB, col:col + width] = _dot(hb_ref[...], win_ref[:, col:col + width])

        def lerp(col, width, row0=0, nrows=TB):
            ext = p_ref[PAD + row0 - 8:PAD + row0 + nrows, col:col + width]
            cur = ext[8:, :]
            prev = pltpu.roll(ext, 1, axis=0)[8:, :]
            return cur + (prev - cur) * mu_ref[0:1, col - SEG0:col - SEG0 + width]

        def f_norm():
            normalise_into_hb(x_ref)
            project(COL_LO, 2 * LORA)
            project(COL_R, D_RWKV)

        def f_decay():
            lo = lerp(COL_LO, 2 * LORA)
            lane = lax.broadcasted_iota(jnp.int32, lo.shape, 1)
            lo = jnp.where(lane < LORA, jnp.tanh(lo), lo)
            lo_hi, lo_lo = _split_hi_lo(lo)
            lora = _diag2(wup_ref[...].astype(BF16), aup_ref[...].astype(BF16))
            lin = _dot(jnp.concatenate([lo_hi, lo_lo], axis=1), jnp.concatenate([lora, lora], axis=0))
            logw = -float(np.exp(-0.5) * np.log2(np.e)) * _sigmoid(w0_ref[...] + lin[:, 0:D_RWKV])
            st["a"] = _sigmoid(a0_ref[...] + lin[:, D_RWKV:2 * D_RWKV])
            lw_b = logw.astype(BF16)
            logw = lw_b.astype(F32)
            cum = jnp.concatenate(
                [_dot(cmat_ref[...], lw_b[rb:rb + CUM_ROWS, :]) for rb in range(0, TB, CUM_ROWS)], axis=0)
            cum_end = [cum[(c + 1) * CHUNK - 1:(c + 1) * CHUNK, :] for c in range(n_chunks)]
            st["p_chunk"] = [jnp.exp2(ce) for ce in cum_end]
            st["p_tot"] = jnp.concatenate(
                [jnp.broadcast_to(pc, (CHUNK, D_RWKV)) for pc in st["p_chunk"]], axis=0)
            st["cum"], st["logw"] = cum, logw
            project(COL_K, D_RWKV)

        def f_receptance():
            r = lerp(COL_R, D_RWKV)
            rt_ref[...] = r * jnp.exp2(st["cum"])
            bon_ref[...] = r * rk_ref[...]
            project(COL_V, D_RWKV)

        def store_transposed(ref, val):
            for c in range(n_chunks):
                for g in range(N_GROUPS):
                    ref[rows_of(c), lanes_of(g)] = _heads_to_rows_t(val[rows_of(c), lanes_of(g)])

        def f_key():
            a, cum = st["a"], st["cum"]
            k = lerp(COL_K, D_RWKV)
            kkr = k * kk_ref[...]
            ssq = segsum((kkr * kkr).astype(BF16))
            kk = kkr * lax.rsqrt(jnp.maximum(ssq, L2_EPS * L2_EPS))
            k2 = k * (a * ka_ref[...] + (1.0 - ka_ref[...]))
            p_inv = jnp.exp2(-cum)
            bt = kk * a * p_inv
            kt = k2 * p_inv
            store_transposed(bt_ref, bt)
            store_transposed(kt_ref, kt)
            store_transposed(bdc_ref, bt * st["p_tot"])
            store_transposed(kdc_ref, kt * st["p_tot"])
            at_ref[...] = -kk * jnp.exp2(cum - st["logw"])
            bon_ref[...] = segsum((bon_ref[...] * k2).astype(BF16)) * v_ref[...]

        def f_value():
            v_ref[...] = lerp(COL_V, D_RWKV)

        front = [f_norm, f_decay, f_receptance, f_value, f_key]

        inst = [(c, g) for g in reversed(range(N_GROUPS)) for c in reversed(range(n_chunks))]
        lab, lak, mrb, mrk = {}, {}, {}, {}
        tinv, lpow, lakv, mrkv, kv, xm = {}, {}, {}, {}, {}, {}
        zt, q, y0, a_mat, g_mat = {}, {}, {}, {}, {}

        def st_scores(i):
            rows, lanes = rows_of(i[0]), lanes_of(i[1])
            ar = jnp.concatenate([at_ref[rows, lanes], rt_ref[rows, lanes]], axis=0).astype(BF16)
            rhs1 = jnp.concatenate([bdt(blocks(bt_ref[rows, lanes])),
                                    bdt(blocks(kt_ref[rows, lanes]))], axis=1)
            sc = _dot(ar, rhs1) * trimask_ref[...]
            lab[i] = sc[0:CHUNK, 0:GROUP]
            lak[i] = sc[0:CHUNK, GROUP:2 * GROUP]
            mrb[i] = sc[CHUNK:2 * CHUNK, 0:GROUP]
            mrk[i] = sc[CHUNK:2 * CHUNK, GROUP:2 * GROUP]

        def st_inv_start(i):
            rows, lanes = rows_of(i[0]), lanes_of(i[1])
            ldiag = lab[i] * tmask_ref[0]
            tinv[i] = eye + ldiag
            lpow[i] = _dot(ldiag.astype(BF16), bd(ldiag))
            xv = _dot(jnp.concatenate([lak[i], mrk[i], kdc_ref[rows, lanes]], axis=0).astype(BF16),
                      bdt(blocks(v_ref[rows, lanes])))
            lakv[i] = xv[0:CHUNK, :]
            mrkv[i] = xv[CHUNK:2 * CHUNK, :]
            kv[i] = xv[2 * CHUNK:3 * CHUNK, :]

        def st_neumann(i):
            res = _dot(jnp.concatenate([lpow[i], tinv[i]], axis=0).astype(BF16), bd(lpow[i]))
            lpow[i] = res[0:CHUNK, :]
            tinv[i] = tinv[i] + res[CHUNK:2 * CHUNK, :]

        def st_neumann_last(i):
            tinv[i] = tinv[i] + _dot(tinv[i].astype(BF16), bd(lpow[i]))

        def st_merge_x(level):
            def stage(i):
                xm[i] = _dot((lab[i] * tmask_ref[level]).astype(BF16), bd(tinv[i]))
            return stage

        def st_merge_t(i):
            tinv[i] = tinv[i] + _dot(tinv[i].astype(BF16), bd(xm[i]))

        def st_last_merge_x(i):
            rows, lanes = rows_of(i[0]), lanes_of(i[1])
            res = _dot(jnp.concatenate([lab[i] * tmask_ref[N_MERGES], mrb[i], bdc_ref[rows, lanes]],
                                       axis=0).astype(BF16),
                       bd(tinv[i]))
            xm[i] = res[0:CHUNK, :]
            zt[i] = res[CHUNK:3 * CHUNK, :]

        def st_last_merge_t(i):
            zt[i] = zt[i] + _dot(zt[i].astype(BF16), bd(xm[i]))

        def st_transition(i):
            rows, lanes = rows_of(i[0]), lanes_of(i[1])
            res = _dot(zt[i].astype(BF16),
                       jnp.concatenate([bdt(blocks(at_ref[rows, lanes])), bdt(blocks(lakv[i]))], axis=1))
            q[i] = rt_ref[rows, lanes] + res[0:CHUNK, 0:GROUP]
            y0[i] = res[0:CHUNK, GROUP:2 * GROUP] + mrkv[i]
            a_mat[i] = eye * st["p_chunk"][i[0]][:, lanes] + res[CHUNK:2 * CHUNK, 0:GROUP]
            g_mat[i] = res[CHUNK:2 * CHUNK, GROUP:2 * GROUP] + kv[i]

        stage_fns = ([st_scores, st_inv_start] + [st_neumann] * (INV_BLOCK.bit_length() - 3)
                     + [st_neumann_last])
        for level in range(1, N_MERGES):
            stage_fns += [st_merge_x(level), st_merge_t]
        stage_fns += [st_last_merge_x, st_last_merge_t, st_transition]

        def run_stage(fn):
            def thunk():
                for i in inst:
                    fn(i)
            return thunk

        chunk_stages = [run_stage(fn) for fn in stage_fns]

        y_chunks = {}

        def chain_step(c):
            def thunk():
                if c == 0:
                    st["hs"] = [h_ref[g] for g in range(N_GROUPS)]
                hs, ys = st["hs"], []
                for g in range(N_GROUPS):
                    i = (c, g)
                    res = _dot(jnp.concatenate([a_mat[i], q[i]], axis=0).astype(BF16), bd(hs[g]))
                    hs[g] = res[0:CHUNK, :] + g_mat[i]
                    ys.append(res[CHUNK:2 * CHUNK, :] + y0[i])
                y_chunks[c] = jnp.concatenate(ys, axis=1)
                if c == n_chunks - 1:
                    for g in range(N_GROUPS):
                        h_ref[g] = hs[g]
            return thunk

        diffs = []

        def pool_windows():
            n_ext = TB + PAD
            sa_ref[8:n_ext, :] = p_ref[8:n_ext, 0:512] + p_ref[7:n_ext - 1, 0:512]
            sb_ref[8:n_ext, 128:512] = sa_ref[8:n_ext, 128:512] + sa_ref[6:n_ext - 2, 128:512]
            sa_ref[8:n_ext, 256:512] = sb_ref[8:n_ext, 256:512] + sb_ref[4:n_ext - 4, 256:512]
            sb_ref[8:n_ext, 384:512] = sa_ref[8:n_ext, 384:512] + sa_ref[0:n_ext - 8, 384:512]

        def pool_diffs():
            pos = t * TB + lax.broadcasted_iota(jnp.int32, (TB, 1), 0) + 1
            wsum_refs = (sa_ref, sb_ref, sa_ref, sb_ref)
            for g, win in enumerate(POOL_WINDOWS):
                lanes = slice(g * POOL_GROUP, (g + 1) * POOL_GROUP)
                cnt = jnp.minimum(pos, win).astype(F32)
                mean = wsum_refs[g][PAD:PAD + TB, lanes] / cnt
                diffs.append(mean - p_ref[PAD:PAD + TB, lanes])

        def pool_out(half):
            lanes = slice(256 * half, 256 * half + 256)
            d2 = jnp.concatenate(diffs[2 * half:2 * half + 2], axis=1).astype(BF16)
            yp = _dot(d2, _diag2(pw_ref[2 * half].astype(BF16), pw_ref[2 * half + 1].astype(BF16)))
            zz = p_ref[PAD:PAD + TB, COL_PZ + 256 * half:COL_PZ + 256 * half + 256]
            yp = yp * pscale_ref[:, lanes] * (zz * _sigmoid(zz))
            mix_ref[:, lanes] = yp.astype(BF16)

        half_w = D_POOL // 2
        fillers = [
            lambda: project(COL_U, half_w),
            lambda: project(COL_U + half_w, half_w),
            lambda: (project(COL_PZ, half_w), pool_windows()),
            lambda: (project(COL_PZ + half_w, half_w), pool_diffs()),
            lambda: (project(COL_Z, half_w), pool_out(0)),
            lambda: (project(COL_Z + half_w, half_w), pool_out(1)),
        ]

        def finish():
            yh = jnp.concatenate([y_chunks[c] for c in range(n_chunks)], axis=0)
            mu = segsum(yh.astype(BF16)) * (1.0 / HEAD)
            dlt = yh - mu
            var = segsum((dlt * dlt).astype(BF16)) * (1.0 / HEAD)
            yn = dlt * lax.rsqrt(var + GN_EPS) * lnw_ref[...] + lnb_ref[...]
            z = lerp(COL_Z, D_RWKV)
            y_rwkv = (yn + bon_ref[...]) * (z * _sigmoid(z))
            mix_ref[:, D_POOL:D_POOL + D_RWKV] = y_rwkv.astype(BF16)
            out = _dot(mix_ref[...], wout_ref[...])
            xo = xs_ref[...] + gate * out
            ms2 = jnp.mean(xo * xo, axis=-1, keepdims=True)
            os_ref[...] = xo * lax.rsqrt(ms2 + NORM_EPS) * fg_ref[...]
            p_ref[8:PAD, :] = p_ref[TB + 8:TB + PAD, :]

        tail = []
        for c in range(n_chunks):
            tail += fillers[c:c + 1]
            tail.append(chain_step(c))
        tail += fillers[n_chunks:] + [finish]
        return front, chunk_stages, tail

    front, chunk_stages, tail = build()
    for thunk in front + chunk_stages + tail:
        thunk()


def _constants():
    hb = np.arange(GROUP) // HEAD
    bdmask = (hb[:, None] == hb[None, :]).astype(np.float32)
    ti = np.arange(CHUNK)[:, None]
    si = (np.arange(2 * GROUP) % HEAD)[None, :]
    trimask = np.concatenate([(ti > si), (ti >= si)], axis=0).astype(np.float32)
    sg = (np.arange(GROUP) % HEAD)[None, :]
    eye = (ti == sg).astype(np.float32)
    same = [(ti // (INV_BLOCK << m)) == (sg // (INV_BLOCK << m)) for m in range(N_MERGES + 1)]
    tmask = np.stack([same[0]] + [same[m] & ~same[m - 1] for m in range(1, N_MERGES + 1)],
                     axis=0).astype(np.float32)
    rr = np.arange(CUM_ROWS)
    same = (rr[:, None] // CHUNK) == (rr[None, :] // CHUNK)
    tril = same & (rr[None, :] <= rr[:, None])
    cmat = tril.astype(np.float32)
    return bdmask, trimask, eye, tmask, cmat


def kernel(x, c, w_ada, b_ada, norm_g, w_in, pool_w, pool_scale, mu_shift, w0, w_up, a0, a_up,
           k_k, k_a, r_k, ln_w, ln_b, w_out, final_g):
    B, T, _ = x.shape
    assert w_ada.shape[0] == 1 and T % TB == 0
    l = 0

    mod, w_in_p, w_out_b = _prologue(c, w_ada, b_ada, w_in, w_out, l)
    mu_p =jnp.concatenate([mu_shift[l, a - _S0:b - _S0] for a, b in W_IN_SPANS[1:]], axis=0)[None, :]

    bdmask, trimask, eye, tmask, cmat = _constants()
    bdmask_b = jnp.asarray(bdmask, BF16)
    segones = bdmask_b
    trimask = jnp.asarray(trimask)
    eye = jnp.asarray(eye)
    tmask = jnp.asarray(tmask)
    cmat = jnp.asarray(cmat, BF16)

    def full(a):
        nd = a.ndim
        return pl.BlockSpec(a.shape, lambda b, t, _nd=nd: (0,) * _nd)

    def layer(a):
        a = a.reshape(a.shape[0], 1, a.shape[1]) if a.ndim == 2 else a
        nd = a.ndim
        return a, pl.BlockSpec((None,) + a.shape[1:], lambda b, t, _nd=nd: (l,) + (0,) * (_nd - 1))

    operands = [
        (x, pl.BlockSpec((None, TB, D_MODEL), lambda b, t: (b, t, 0))),
        (mod, pl.BlockSpec((None, 3, D_MODEL), lambda b, t: (b, 0, 0))),
        layer(norm_g), (w_in_p, full(w_in_p)), (mu_p, full(mu_p)), layer(w_up), layer(a_up),
        layer(w0), layer(a0), layer(k_k), layer(k_a), layer(r_k), layer(ln_w), layer(ln_b),
        layer(pool_scale), layer(pool_w), (w_out_b, full(w_out_b)),
        (final_g[None, :], full(final_g[None, :])),
    ] + [(a, full(a)) for a in (bdmask_b, trimask, eye, tmask, cmat, segones)]
    in_specs = [spec for _, spec in operands]

    blk = lambda: pltpu.VMEM((TB, D_RWKV), F32)
    scratch = [
        pltpu.VMEM((TB + PAD, N_IN), F32),
        pltpu.VMEM((TB + PAD, D_POOL), F32),
        pltpu.VMEM((TB + PAD, D_POOL), F32),
        blk(), blk(), blk(), blk(), blk(), blk(),
        blk(), blk(),
        pltpu.VMEM((TB, D_MODEL), BF16),
        pltpu.VMEM((TB, D_MODEL), BF16),
        pltpu.VMEM((N_GROUPS, HEAD, GROUP), F32),
    ]
    return pl.pallas_call(
        _fused_kernel,
        grid=(B, T // TB),
        in_specs=in_specs,
        out_specs=pl.BlockSpec((None, TB, D_MODEL), lambda b, t: (b, t, 0)),
        out_shape=jax.ShapeDtypeStruct((B, T, D_MODEL), x.dtype),
        scratch_shapes=scratch,
        compiler_params=pltpu.CompilerParams(
            dimension_semantics=("arbitrary", "arbitrary"),
            vmem_limit_bytes=VMEM_LIMIT_BYTES),
        name="hybrid_block",
    )(*[a for a, _ in operands])
```

```python
import numpy as np
import jax
import jax.numpy as jnp
from jax import lax
from jax.experimental import pallas as pl
from jax.experimental.pallas import tpu as pltpu

F32 = jnp.float32
BF16 = jnp.bfloat16

D_MODEL = 1024
D_POOL = 512
D_RWKV = 512
HEAD = 64
LORA = 64
POOL_WINDOWS = (2, 4, 8, 16)
POOL_GROUP = 128
NORM_EPS = 1e-6
GN_EPS = 64e-5
L2_EPS = 1e-12

COL_U = 0
COL_PZ = 512
COL_R = 1024
COL_K = 1536
COL_V = 2048
COL_Z = 2560
COL_LO = 3072
N_IN = 3200
SEG0 = 1024

CHUNK = 64
GROUP = 256
HEADS_PER_GROUP = GROUP // HEAD
N_GROUPS = D_RWKV // GROUP
TB = 512
CUM_ROWS = 256
INV_BLOCK = 32
N_MERGES = (CHUNK // INV_BLOCK).bit_length() - 1
PAD = 24

VMEM_LIMIT_BYTES = 56 * 1024 * 1024


def _split_hi_lo(x):
    hi = x.astype(BF16)
    lo = (x - hi.astype(F32)).astype(BF16)
    return hi, lo


def _dot(a, b):
    return jnp.dot(a, b, preferred_element_type=F32)


def _sigmoid(x):
    return 0.5 * jnp.tanh(0.5 * x) + 0.5


ADA_ROWS = 256
LANES = 128


def _ada_accumulate(cb_ref, w_ref, b_ref, o_ref):
    nb, n_mod, d = o_ref.shape

    @pl.when(pl.program_id(0) == 0)
    def _():
        for m in range(n_mod):
            o_ref[:, m, :] = jnp.broadcast_to(b_ref[:, m * d:(m + 1) * d], (nb, d))

    w = w_ref[...]
    rows = []
    for b in range(nb):
        cb = cb_ref[b]
        cols = [jnp.sum(w[:, j:j + LANES] * cb, axis=0, keepdims=True)
                for j in range(0, w.shape[1], LANES)]
        rows.append(jnp.concatenate(cols, axis=1))
    acc = jnp.concatenate(rows, axis=0)
    for m in range(n_mod):
        o_ref[:, m, :] += acc[:, m * d:(m + 1) * d]


_S0 = 2 * D_POOL
W_IN_SPANS = ((0, _S0), (_S0, _S0 + 512), (_S0 + 576, _S0 + 1088), (_S0 + 1088, _S0 + 1600),
              (_S0 + 1664, _S0 + 2176), (_S0 + 512, _S0 + 576), (_S0 + 1600, _S0 + 1664))


def _prologue_kernel(cb_ref, wa_ref, b_ref, wi_ref, wo_ref, mod_ref, wip_ref, wob_ref):
    _ada_accumulate(cb_ref, wa_ref, b_ref, mod_ref)
    w = wi_ref[...]
    wip_ref[...] = jnp.concatenate([w[:, a:b] for a, b in W_IN_SPANS], axis=1).astype(BF16)
    wob_ref[...] = wo_ref[...].astype(BF16)


def _prologue(c, w_ada, b_ada, w_in, w_out, l):
    nb, n = c.shape[0], w_ada.shape[2]
    d, n_in = w_in.shape[1], w_in.shape[2]
    n_out = w_out.shape[2]
    cb = jnp.broadcast_to(c[:, :, None], (nb, D_MODEL, LANES))
    return pl.pallas_call(
        _prologue_kernel,
        grid=(D_MODEL // ADA_ROWS,),
        in_specs=[
            pl.BlockSpec((nb, ADA_ROWS, LANES), lambda j: (0, j, 0)),
            pl.BlockSpec((None, ADA_ROWS, n), lambda j: (l, j, 0)),
            pl.BlockSpec((None, 1, n), lambda j: (l, 0, 0)),
            pl.BlockSpec((None, ADA_ROWS, n_in), lambda j: (l, j, 0)),
            pl.BlockSpec((None, ADA_ROWS, n_out), lambda j: (l, j, 0)),
        ],
        out_specs=[
            pl.BlockSpec((nb, n // D_MODEL, D_MODEL), lambda j: (0, 0, 0)),
            pl.BlockSpec((ADA_ROWS, n_in), lambda j: (j, 0)),
            pl.BlockSpec((ADA_ROWS, n_out), lambda j: (j, 0)),
        ],
        out_shape=[
            jax.ShapeDtypeStruct((nb, n // D_MODEL, D_MODEL), F32),
            jax.ShapeDtypeStruct((d, n_in), BF16),
            jax.ShapeDtypeStruct((d, n_out), BF16),
        ],
        compiler_params=pltpu.CompilerParams(dimension_semantics=("arbitrary",)),
        name="prologue",
    )(cb, w_ada, b_ada.reshape(b_ada.shape[0], 1, n), w_in, w_out)


def _head_blocks(x, bdmask):
    xb = x.astype(BF16)
    heads_per_tile = LANES // HEAD
    zero_tile = jnp.zeros((x.shape[0], LANES), BF16)
    out = []
    for h in range(HEADS_PER_GROUP):
        tiles = [xb[:, j * LANES:(j + 1) * LANES] * bdmask[h * HEAD:(h + 1) * HEAD, j * LANES:(j + 1) * LANES]
                 if j == h // heads_per_tile else zero_tile
                 for j in range(GROUP // LANES)]
        out.append(jnp.concatenate(tiles, axis=1))
    return out


def _block_diag(x, bdmask):
    return jnp.concatenate(_head_blocks(x, bdmask), axis=0)


T_HEAD_ORDER = (0, 2, 1, 3)


def _heads_to_rows_t(x):
    xt = jnp.concatenate([x[:, 0:2 * HEAD], x[:, 2 * HEAD:4 * HEAD]], axis=0).T
    return jnp.concatenate([xt[0:HEAD, :], xt[HEAD:2 * HEAD, :]], axis=1)


def _diag2(m0, m1):
    z = jnp.zeros_like(m0)
    return jnp.concatenate([jnp.concatenate([m0, z], axis=1),
                            jnp.concatenate([z, m1], axis=1)], axis=0)


def _fused_kernel(x_ref, mod_ref, ng_ref, win_ref, mu_ref, wup_ref, aup_ref,
                  w0_ref, a0_ref, kk_ref, ka_ref, rk_ref, lnw_ref, lnb_ref, pscale_ref,
                  pw_ref, wout_ref, fg_ref, bdmask_ref, trimask_ref, eye_ref, tmask_ref, cmat_ref,
                  segones_ref, o_ref,
                  p_ref, sa_ref, sb_ref, at_ref, rt_ref, bt_ref, kt_ref, bdc_ref, kdc_ref,
                  v_ref, bon_ref, mix_ref, hb_ref, h_ref):
    t = pl.program_id(1)

    def normalise_into_hb(xin_ref):
        xb = xin_ref[...]
        ms = jnp.mean(xb * xb, axis=-1, keepdims=True)
        gs = ng_ref[...] * (1.0 + mod_ref[1:2, :])
        hb_ref[...] = (xb * lax.rsqrt(ms + NORM_EPS) * gs + mod_ref[0:1, :]).astype(BF16)

    @pl.when(t == 0)
    def _():
        p_ref[0:PAD, :] = jnp.zeros((PAD, N_IN), F32)
        sa_ref[0:8, :] = jnp.zeros((8, D_POOL), F32)
        sb_ref[0:8, :] = jnp.zeros((8, D_POOL), F32)
        h_ref[...] = jnp.zeros(h_ref.shape, F32)

    bdmask = bdmask_ref[...]
    eye = eye_ref[...]
    n_chunks = TB // CHUNK

    def blocks(xv):
        return _head_blocks(xv, bdmask)

    def bd(xv):
        return _block_diag(xv, bdmask)

    def bdt(blks):
        return jnp.concatenate([blks[h] for h in T_HEAD_ORDER], axis=0)

    def rows_of(c):
        return slice(c * CHUNK, (c + 1) * CHUNK)

    def lanes_of(g):
        return slice(GROUP * g, GROUP * (g + 1))

    def segsum(xb):
        parts = []
        for half in range(N_GROUPS):
            parts.append(_dot(xb[:, GROUP * half:GROUP * (half + 1)], segones_ref[...]))
        return jnp.concatenate(parts, axis=1)

    def build():
        xs_ref, os_ref = x_ref, o_ref
        gate = mod_ref[2:3, :]
        st = {}

        def project(col, width):
            p_ref[PAD:PAD + TB, col:col + width] = _dot(hb_ref[...], win_ref[:, col:col + width])

        def lerp(col, width, row0=0, nrows=TB):
            ext = p_ref[PAD + row0 - 8:PAD + row0 + nrows, col:col + width]
            cur = ext[8:, :]
            prev = pltpu.roll(ext, 1, axis=0)[8:, :]
            return cur + (prev - cur) * mu_ref[0:1, col - SEG0:col - SEG0 + width]

        def f_norm():
            normalise_into_hb(x_ref)
            project(COL_LO, 2 * LORA)
            project(COL_R, D_RWKV)

        def f_decay():
            lo = lerp(COL_LO, 2 * LORA)
            lane = lax.broadcasted_iota(jnp.int32, lo.shape, 1)
            lo = jnp.where(lane < LORA, jnp.tanh(lo), lo)
            lo_hi, lo_lo = _split_hi_lo(lo)
            lora = _diag2(wup_ref[...].astype(BF16), aup_ref[...].astype(BF16))
            lin = _dot(jnp.concatenate([lo_hi, lo_lo], axis=1), jnp.concatenate([lora, lora], axis=0))
            logw = -float(np.exp(-0.5) * np.log2(np.e)) * _sigmoid(w0_ref[...] + lin[:, 0:D_RWKV])
            st["a"] = _sigmoid(a0_ref[...] + lin[:, D_RWKV:2 * D_RWKV])
            lw_b = logw.astype(BF16)
            logw = lw_b.astype(F32)
            cum = jnp.concatenate(
                [_dot(cmat_ref[...], lw_b[rb:rb + CUM_ROWS, :]) for rb in range(0, TB, CUM_ROWS)], axis=0)
            cum_end = [cum[(c + 1) * CHUNK - 1:(c + 1) * CHUNK, :] for c in range(n_chunks)]
            st["p_chunk"] = [jnp.exp2(ce) for ce in cum_end]
            st["p_tot"] = jnp.concatenate(
                [jnp.broadcast_to(pc, (CHUNK, D_RWKV)) for pc in st["p_chunk"]], axis=0)
            st["cum"], st["logw"] = cum, logw
            project(COL_K, D_RWKV)

        def f_receptance():
            r = lerp(COL_R, D_RWKV)
            rt_ref[...] = r * jnp.exp2(st["cum"])
            bon_ref[...] = r * rk_ref[...]
            project(COL_V, D_RWKV)

        def store_transposed(ref, val):
            for c in range(n_chunks):
                for g in range(N_GROUPS):
                    ref[rows_of(c), lanes_of(g)] = _heads_to_rows_t(val[rows_of(c), lanes_of(g)])

        def f_key():
            a, cum = st["a"], st["cum"]
            k = lerp(COL_K, D_RWKV)
            kkr = k * kk_ref[...]
            ssq = segsum((kkr * kkr).astype(BF16))
            kk = kkr * lax.rsqrt(jnp.maximum(ssq, L2_EPS * L2_EPS))
            k2 = k * (a * ka_ref[...] + (1.0 - ka_ref[...]))
            p_inv = jnp.exp2(-cum)
            bt = kk * a * p_inv
            kt = k2 * p_inv
            store_transposed(bt_ref, bt)
            store_transposed(kt_ref, kt)
            store_transposed(bdc_ref, bt * st["p_tot"])
            store_transposed(kdc_ref, kt * st["p_tot"])
            at_ref[...] = -kk * jnp.exp2(cum - st["logw"])
            bon_ref[...] = segsum((bon_ref[...] * k2).astype(BF16)) * v_ref[...]

        def f_value():
            v_ref[...] = lerp(COL_V, D_RWKV)

        front = [f_norm, f_decay, f_receptance, f_value, f_key]

        inst = [(c, g) for g in range(N_GROUPS) for c in range(n_chunks)]
        lab, lak, mrb, mrk = {}, {}, {}, {}
        tinv, lpow, lakv, mrkv, kv, xm = {}, {}, {}, {}, {}, {}
        zt, q, y0, a_mat, g_mat = {}, {}, {}, {}, {}

        def st_scores(i):
            rows, lanes = rows_of(i[0]), lanes_of(i[1])
            ar = jnp.concatenate([at_ref[rows, lanes], rt_ref[rows, lanes]], axis=0).astype(BF16)
            rhs1 = jnp.concatenate([bdt(blocks(bt_ref[rows, lanes])),
                                    bdt(blocks(kt_ref[rows, lanes]))], axis=1)
            sc = _dot(ar, rhs1) * trimask_ref[...]
            lab[i] = sc[0:CHUNK, 0:GROUP]
            lak[i] = sc[0:CHUNK, GROUP:2 * GROUP]
            mrb[i] = sc[CHUNK:2 * CHUNK, 0:GROUP]
            mrk[i] = sc[CHUNK:2 * CHUNK, GROUP:2 * GROUP]

        def st_inv_start(i):
            rows, lanes = rows_of(i[0]), lanes_of(i[1])
            ldiag = lab[i] * tmask_ref[0]
            tinv[i] = eye + ldiag
            lpow[i] = _dot(ldiag.astype(BF16), bd(ldiag))
            xv = _dot(jnp.concatenate([lak[i], mrk[i], kdc_ref[rows, lanes]], axis=0).astype(BF16),
                      bdt(blocks(v_ref[rows, lanes])))
            lakv[i] = xv[0:CHUNK, :]
            mrkv[i] = xv[CHUNK:2 * CHUNK, :]
            kv[i] = xv[2 * CHUNK:3 * CHUNK, :]

        def st_neumann(i):
            res = _dot(jnp.concatenate([lpow[i], tinv[i]], axis=0).astype(BF16), bd(lpow[i]))
            lpow[i] = res[0:CHUNK, :]
            tinv[i] = tinv[i] + res[CHUNK:2 * CHUNK, :]

        def st_neumann_last(i):
            tinv[i] = tinv[i] + _dot(tinv[i].astype(BF16), bd(lpow[i]))

        def st_merge_x(level):
            def stage(i):
                xm[i] = _dot((lab[i] * tmask_ref[level]).astype(BF16), bd(tinv[i]))
            return stage

        def st_merge_t(i):
            tinv[i] = tinv[i] + _dot(tinv[i].astype(BF16), bd(xm[i]))

        def st_last_merge_x(i):
            rows, lanes = rows_of(i[0]), lanes_of(i[1])
            res = _dot(jnp.concatenate([lab[i] * tmask_ref[N_MERGES], mrb[i], bdc_ref[rows, lanes]],
                                       axis=0).astype(BF16),
                       bd(tinv[i]))
            xm[i] = res[0:CHUNK, :]
            zt[i] = res[CHUNK:3 * CHUNK, :]

        def st_last_merge_t(i):
            zt[i] = zt[i] + _dot(zt[i].astype(BF16), bd(xm[i]))

        def st_transition(i):
            rows, lanes = rows_of(i[0]), lanes_of(i[1])
            res = _dot(zt[i].astype(BF16),
                       jnp.concatenate([bdt(blocks(at_ref[rows, lanes])), bdt(blocks(lakv[i]))], axis=1))
            q[i] = rt_ref[rows, lanes] + res[0:CHUNK, 0:GROUP]
            y0[i] = res[0:CHUNK, GROUP:2 * GROUP] + mrkv[i]
            a_mat[i] = eye * st["p_chunk"][i[0]][:, lanes] + res[CHUNK:2 * CHUNK, 0:GROUP]
            g_mat[i] = res[CHUNK:2 * CHUNK, GROUP:2 * GROUP] + kv[i]

        stage_fns = ([st_scores, st_inv_start] + [st_neumann] * (INV_BLOCK.bit_length() - 3)
                     + [st_neumann_last])
        for level in range(1, N_MERGES):
            stage_fns += [st_merge_x(level), st_merge_t]
        stage_fns += [st_last_merge_x, st_last_merge_t, st_transition]

        def run_stage(fn):
            def thunk():
                for i in inst:
                    fn(i)
            return thunk

        chunk_stages = [run_stage(fn) for fn in stage_fns]

        y_chunks = {}

        def chain_step(c):
            def thunk():
                if c == 0:
                    st["hs"] = [h_ref[g] for g in range(N_GROUPS)]
                hs, ys = st["hs"], []
                for g in range(N_GROUPS):
                    i = (c, g)
                    res = _dot(jnp.concatenate([a_mat[i], q[i]], axis=0).astype(BF16), bd(hs[g]))
                    hs[g] = res[0:CHUNK, :] + g_mat[i]
                    ys.append(res[CHUNK:2 * CHUNK, :] + y0[i])
                y_chunks[c] = jnp.concatenate(ys, axis=1)
                if c == n_chunks - 1:
                    for g in range(N_GROUPS):
                        h_ref[g] = hs[g]
            return thunk

        diffs = []

        def pool_windows():
            n_ext = TB + PAD
            sa_ref[8:n_ext, :] = p_ref[8:n_ext, 0:512] + p_ref[7:n_ext - 1, 0:512]
            sb_ref[8:n_ext, 128:512] = sa_ref[8:n_ext, 128:512] + sa_ref[6:n_ext - 2, 128:512]
            sa_ref[8:n_ext, 256:512] = sb_ref[8:n_ext, 256:512] + sb_ref[4:n_ext - 4, 256:512]
            sb_ref[8:n_ext, 384:512] = sa_ref[8:n_ext, 384:512] + sa_ref[0:n_ext - 8, 384:512]

        def pool_diffs():
            pos = t * TB + lax.broadcasted_iota(jnp.int32, (TB, 1), 0) + 1
            wsum_refs = (sa_ref, sb_ref, sa_ref, sb_ref)
            for g, win in enumerate(POOL_WINDOWS):
                lanes = slice(g * POOL_GROUP, (g + 1) * POOL_GROUP)
                cnt = jnp.minimum(pos, win).astype(F32)
                mean = wsum_refs[g][PAD:PAD + TB, lanes] / cnt
                diffs.append(mean - p_ref[PAD:PAD + TB, lanes])

        def pool_out(half):
            lanes = slice(256 * half, 256 * half + 256)
            d2 = jnp.concatenate(diffs[2 * half:2 * half + 2], axis=1).astype(BF16)
            yp = _dot(d2, _diag2(pw_ref[2 * half].astype(BF16), pw_ref[2 * half + 1].astype(BF16)))
            zz = p_ref[PAD:PAD + TB, COL_PZ + 256 * half:COL_PZ + 256 * half + 256]
            yp = yp * pscale_ref[:, lanes] * (zz * _sigmoid(zz))
            mix_ref[:, lanes] = yp.astype(BF16)

        half_w = D_POOL // 2
        fillers = [
            lambda: project(COL_U, half_w),
            lambda: project(COL_U + half_w, half_w),
            lambda: (project(COL_PZ, half_w), pool_windows()),
            lambda: (project(COL_PZ + half_w, half_w), pool_diffs()),
            lambda: (project(COL_Z, half_w), pool_out(0)),
            lambda: (project(COL_Z + half_w, half_w), pool_out(1)),
        ]

        def finish():
            yh = jnp.concatenate([y_chunks[c] for c in range(n_chunks)], axis=0)
            mu = segsum(yh.astype(BF16)) * (1.0 / HEAD)
            dlt = yh - mu
            var = segsum((dlt * dlt).astype(BF16)) * (1.0 / HEAD)
            yn = dlt * lax.rsqrt(var + GN_EPS) * lnw_ref[...] + lnb_ref[...]
            z = lerp(COL_Z, D_RWKV)
            y_rwkv = (yn + bon_ref[...]) * (z * _sigmoid(z))
            mix_ref[:, D_POOL:D_POOL + D_RWKV] = y_rwkv.astype(BF16)
            out = _dot(mix_ref[...], wout_ref[...])
            xo = xs_ref[...] + gate * out
            ms2 = jnp.mean(xo * xo, axis=-1, keepdims=True)
            os_ref[...] = xo * lax.rsqrt(ms2 + NORM_EPS) * fg_ref[...]
            p_ref[8:PAD, :] = p_ref[TB + 8:TB + PAD, :]

        tail = []
        for c in range(n_chunks):
            tail += fillers[c:c + 1]
            tail.append(chain_step(c))
        tail += fillers[n_chunks:] + [finish]
        return front, chunk_stages, tail

    front, chunk_stages, tail = build()
    for thunk in front + chunk_stages + tail:
        thunk()


def _constants():
    hb = np.arange(GROUP) // HEAD
    bdmask = (hb[:, None] == hb[None, :]).astype(np.float32)
    ti = np.arange(CHUNK)[:, None]
    si = (np.arange(2 * GROUP) % HEAD)[None, :]
    trimask = np.concatenate([(ti > si), (ti >= si)], axis=0).astype(np.float32)
    sg = (np.arange(GROUP) % HEAD)[None, :]
    eye = (ti == sg).astype(np.float32)
    same = [(ti // (INV_BLOCK << m)) == (sg // (INV_BLOCK << m)) for m in range(N_MERGES + 1)]
    tmask = np.stack([same[0]] + [same[m] & ~same[m - 1] for m in range(1, N_MERGES + 1)],
                     axis=0).astype(np.float32)
    rr = np.arange(CUM_ROWS)
    same = (rr[:, None] // CHUNK) == (rr[None, :] // CHUNK)
    tril = same & (rr[None, :] <= rr[:, None])
    cmat = tril.astype(np.float32)
    return bdmask, trimask, eye, tmask, cmat


def kernel(x, c, w_ada, b_ada, norm_g, w_in, pool_w, pool_scale, mu_shift, w0, w_up, a0, a_up,
           k_k, k_a, r_k, ln_w, ln_b, w_out, final_g):
    B, T, _ = x.shape
    assert w_ada.shape[0] == 1 and T % TB == 0
    l = 0

    mod, w_in_p, w_out_b = _prologue(c, w_ada, b_ada, w_in, w_out, l)
    mu_p =jnp.concatenate([mu_shift[l, a - _S0:b - _S0] for a, b in W_IN_SPANS[1:]], axis=0)[None, :]

    bdmask, trimask, eye, tmask, cmat = _constants()
    bdmask_b = jnp.asarray(bdmask, BF16)
    segones = bdmask_b
    trimask = jnp.asarray(trimask)
    eye = jnp.asarray(eye)
    tmask = jnp.asarray(tmask)
    cmat = jnp.asarray(cmat, BF16)

    resident = pl.Buffered(1)

    def full(a):
        nd = a.ndim
        return pl.BlockSpec(a.shape, lambda b, t, _nd=nd: (0,) * _nd, pipeline_mode=resident)

    def layer(a):
        a = a.reshape(a.shape[0], 1, a.shape[1]) if a.ndim == 2 else a
        nd = a.ndim
        return a, pl.BlockSpec((None,) + a.shape[1:], lambda b, t, _nd=nd: (l,) + (0,) * (_nd - 1),
                               pipeline_mode=resident)

    operands = [
        (x, pl.BlockSpec((None, TB, D_MODEL), lambda b, t: (b, t, 0))),
        (mod, pl.BlockSpec((None, 3, D_MODEL), lambda b, t: (b, 0, 0))),
        layer(norm_g), (w_in_p, full(w_in_p)), (mu_p, full(mu_p)), layer(w_up), layer(a_up),
        layer(w0), layer(a0), layer(k_k), layer(k_a), layer(r_k), layer(ln_w), layer(ln_b),
        layer(pool_scale), layer(pool_w), (w_out_b, full(w_out_b)),
        (final_g[None, :], full(final_g[None, :])),
    ] + [(a, full(a)) for a in (bdmask_b, trimask, eye, tmask, cmat, segones)]
    in_specs = [spec for _, spec in operands]

    blk = lambda: pltpu.VMEM((TB, D_RWKV), F32)
    scratch = [
        pltpu.VMEM((TB + PAD, N_IN), F32),
        pltpu.VMEM((TB + PAD, D_POOL), F32),
        pltpu.VMEM((TB + PAD, D_POOL), F32),
        blk(), blk(), blk(), blk(), blk(), blk(),
        blk(), blk(),
        pltpu.VMEM((TB, D_MODEL), BF16),
        pltpu.VMEM((TB, D_MODEL), BF16),
        pltpu.VMEM((N_GROUPS, HEAD, GROUP), F32),
    ]
    return pl.pallas_call(
        _fused_kernel,
        grid=(B, T // TB),
        in_specs=in_specs,
        out_specs=pl.BlockSpec((None, TB, D_MODEL), lambda b, t: (b, t, 0)),
        out_shape=jax.ShapeDtypeStruct((B, T, D_MODEL), x.dtype),
        scratch_shapes=scratch,
        compiler_params=pltpu.CompilerParams(
            dimension_semantics=("arbitrary", "arbitrary"),
            vmem_limit_bytes=VMEM_LIMIT_BYTES),
        name="hybrid_block",
    )(*[a for a, _ in operands])
```

```python
import numpy as np
import jax
import jax.numpy as jnp
from jax import lax
from jax.experimental import pallas as pl
from jax.experimental.pallas import tpu as pltpu

F32 = jnp.float32
BF16 = jnp.bfloat16

D_MODEL = 1024
D_POOL = 512
D_RWKV = 512
HEAD = 64
LORA = 64
POOL_WINDOWS = (2, 4, 8, 16)
POOL_GROUP = 128
NORM_EPS = 1e-6
GN_EPS = 64e-5
L2_EPS = 1e-12

COL_U = 0
COL_PZ = 512
COL_R = 1024
COL_K = 1536
COL_V = 2048
COL_Z = 2560
COL_LO = 3072
N_IN = 3200
SEG0 = 1024

CHUNK = 64
GROUP = 256
HEADS_PER_GROUP = GROUP // HEAD
N_GROUPS = D_RWKV // GROUP
TB = 512
CUM_ROWS = 256
INV_BLOCK = 32
N_MERGES = (CHUNK // INV_BLOCK).bit_length() - 1
PAD = 24

VMEM_LIMIT_BYTES = 56 * 1024 * 1024


def _split_hi_lo(x):
    hi = x.astype(BF16)
    lo = (x - hi.astype(F32)).astype(BF16)
    return hi, lo


def _dot(a, b):
    return jnp.dot(a, b, preferred_element_type=F32)


def _sigmoid(x):
    return 0.5 * jnp.tanh(0.5 * x) + 0.5


ADA_ROWS = 256
LANES = 128


def _ada_accumulate(cb_ref, w_ref, b_ref, o_ref):
    nb, n_mod, d = o_ref.shape

    @pl.when(pl.program_id(0) == 0)
    def _():
        for m in range(n_mod):
            o_ref[:, m, :] = jnp.broadcast_to(b_ref[:, m * d:(m + 1) * d], (nb, d))

    w = w_ref[...]
    rows = []
    for b in range(nb):
        cb = cb_ref[b]
        cols = [jnp.sum(w[:, j:j + LANES] * cb, axis=0, keepdims=True)
                for j in range(0, w.shape[1], LANES)]
        rows.append(jnp.concatenate(cols, axis=1))
    acc = jnp.concatenate(rows, axis=0)
    for m in range(n_mod):
        o_ref[:, m, :] += acc[:, m * d:(m + 1) * d]


_S0 = 2 * D_POOL
W_IN_SPANS = ((0, _S0), (_S0, _S0 + 512), (_S0 + 576, _S0 + 1088), (_S0 + 1088, _S0 + 1600),
              (_S0 + 1664, _S0 + 2176), (_S0 + 512, _S0 + 576), (_S0 + 1600, _S0 + 1664))


def _prologue_kernel(cb_ref, wa_ref, b_ref, wi_ref, wo_ref, mod_ref, wip_ref, wob_ref):
    _ada_accumulate(cb_ref, wa_ref, b_ref, mod_ref)
    w = wi_ref[...]
    wip_ref[...] = jnp.concatenate([w[:, a:b] for a, b in W_IN_SPANS], axis=1).astype(BF16)
    wob_ref[...] = wo_ref[...].astype(BF16)


def _prologue(c, w_ada, b_ada, w_in, w_out, l):
    nb, n = c.shape[0], w_ada.shape[2]
    d, n_in = w_in.shape[1], w_in.shape[2]
    n_out = w_out.shape[2]
    cb = jnp.broadcast_to(c[:, :, None], (nb, D_MODEL, LANES))
    return pl.pallas_call(
        _prologue_kernel,
        grid=(D_MODEL // ADA_ROWS,),
        in_specs=[
            pl.BlockSpec((nb, ADA_ROWS, LANES), lambda j: (0, j, 0)),
            pl.BlockSpec((None, ADA_ROWS, n), lambda j: (l, j, 0)),
            pl.BlockSpec((None, 1, n), lambda j: (l, 0, 0)),
            pl.BlockSpec((None, ADA_ROWS, n_in), lambda j: (l, j, 0)),
            pl.BlockSpec((None, ADA_ROWS, n_out), lambda j: (l, j, 0)),
        ],
        out_specs=[
            pl.BlockSpec((nb, n // D_MODEL, D_MODEL), lambda j: (0, 0, 0)),
            pl.BlockSpec((ADA_ROWS, n_in), lambda j: (j, 0)),
            pl.BlockSpec((ADA_ROWS, n_out), lambda j: (j, 0)),
        ],
        out_shape=[
            jax.ShapeDtypeStruct((nb, n // D_MODEL, D_MODEL), F32),
            jax.ShapeDtypeStruct((d, n_in), BF16),
            jax.ShapeDtypeStruct((d, n_out), BF16),
        ],
        compiler_params=pltpu.CompilerParams(dimension_semantics=("arbitrary",)),
        name="prologue",
    )(cb, w_ada, b_ada.reshape(b_ada.shape[0], 1, n), w_in, w_out)


def _head_blocks(x, bdmask):
    xb = x.astype(BF16)
    heads_per_tile = LANES // HEAD
    zero_tile = jnp.zeros((x.shape[0], LANES), BF16)
    out = []
    for h in range(HEADS_PER_GROUP):
        tiles = [xb[:, j * LANES:(j + 1) * LANES] * bdmask[h * HEAD:(h + 1) * HEAD, j * LANES:(j + 1) * LANES]
                 if j == h // heads_per_tile else zero_tile
                 for j in range(GROUP // LANES)]
        out.append(jnp.concatenate(tiles, axis=1))
    return out


def _block_diag(x, bdmask):
    return jnp.concatenate(_head_blocks(x, bdmask), axis=0)


T_HEAD_ORDER = (0, 2, 1, 3)


def _heads_to_rows_t(x):
    xt = jnp.concatenate([x[:, 0:2 * HEAD], x[:, 2 * HEAD:4 * HEAD]], axis=0).T
    return jnp.concatenate([xt[0:HEAD, :], xt[HEAD:2 * HEAD, :]], axis=1)


def _diag2(m0, m1):
    z = jnp.zeros_like(m0)
    return jnp.concatenate([jnp.concatenate([m0, z], axis=1),
                            jnp.concatenate([z, m1], axis=1)], axis=0)


def _fused_kernel(x_ref, mod_ref, ng_ref, win_ref, mu_ref, wup_ref, aup_ref,
                  w0_ref, a0_ref, kk_ref, ka_ref, rk_ref, lnw_ref, lnb_ref, pscale_ref,
                  pw_ref, wout_ref, fg_ref, bdmask_ref, trimask_ref, eye_ref, tmask_ref, cmat_ref,
                  segones_ref, o_ref,
                  p_ref, sa_ref, sb_ref, at_ref, rt_ref, bt_ref, kt_ref, bdc_ref, kdc_ref,
                  v_ref, bon_ref, mix_ref, hb_ref, h_ref):
    t = pl.program_id(1)

    def normalise_into_hb(xin_ref):
        xb = xin_ref[...]
        ms = jnp.mean(xb * xb, axis=-1, keepdims=True)
        gs = ng_ref[...] * (1.0 + mod_ref[1:2, :])
        hb_ref[...] = (xb * lax.rsqrt(ms + NORM_EPS) * gs + mod_ref[0:1, :]).astype(BF16)

    @pl.when(t == 0)
    def _():
        p_ref[0:PAD, :] = jnp.zeros((PAD, N_IN), F32)
        sa_ref[0:8, :] = jnp.zeros((8, D_POOL), F32)
        sb_ref[0:8, :] = jnp.zeros((8, D_POOL), F32)
        h_ref[...] = jnp.zeros(h_ref.shape, F32)

    bdmask = bdmask_ref

    n_chunks = TB // CHUNK

    def blocks(xv):
        return _head_blocks(xv, bdmask)

    def bd(xv):
        return _block_diag(xv, bdmask)

    def bdt(blks):
        return jnp.concatenate([blks[h] for h in T_HEAD_ORDER], axis=0)

    def rows_of(c):
        return slice(c * CHUNK, (c + 1) * CHUNK)

    def lanes_of(g):
        return slice(GROUP * g, GROUP * (g + 1))

    def segsum(xb):
        parts = []
        for half in range(N_GROUPS):
            parts.append(_dot(xb[:, GROUP * half:GROUP * (half + 1)], segones_ref[...]))
        return jnp.concatenate(parts, axis=1)

    def build():
        xs_ref, os_ref = x_ref, o_ref
        gate = mod_ref[2:3, :]
        st = {}

        def project(col, width):
            p_ref[PAD:PAD + TB, col:col + width] = _dot(hb_ref[...], win_ref[:, col:col + width])

        def lerp(col, width, row0=0, nrows=TB):
            ext = p_ref[PAD + row0 - 8:PAD + row0 + nrows, col:col + width]
            cur = ext[8:, :]
            prev = pltpu.roll(ext, 1, axis=0)[8:, :]
            return cur + (prev - cur) * mu_ref[0:1, col - SEG0:col - SEG0 + width]

        def f_norm():
            normalise_into_hb(x_ref)
            project(COL_LO, 2 * LORA)
            project(COL_R, D_RWKV)

        def f_decay():
            lo = lerp(COL_LO, 2 * LORA)
            lane = lax.broadcasted_iota(jnp.int32, lo.shape, 1)
            lo = jnp.where(lane < LORA, jnp.tanh(lo), lo)
            lo_hi, lo_lo = _split_hi_lo(lo)
            lora = _diag2(wup_ref[...].astype(BF16), aup_ref[...].astype(BF16))
            lin = _dot(jnp.concatenate([lo_hi, lo_lo], axis=1), jnp.concatenate([lora, lora], axis=0))
            logw = -float(np.exp(-0.5) * np.log2(np.e)) * _sigmoid(w0_ref[...] + lin[:, 0:D_RWKV])
            st["a"] = _sigmoid(a0_ref[...] + lin[:, D_RWKV:2 * D_RWKV])
            lw_b = logw.astype(BF16)
            logw = lw_b.astype(F32)
            cum = jnp.concatenate(
                [_dot(cmat_ref[...], lw_b[rb:rb + CUM_ROWS, :]) for rb in range(0, TB, CUM_ROWS)], axis=0)
            cum_end = [cum[(c + 1) * CHUNK - 1:(c + 1) * CHUNK, :] for c in range(n_chunks)]
            st["p_chunk"] = [jnp.exp2(ce) for ce in cum_end]
            st["p_tot"] = jnp.concatenate(
                [jnp.broadcast_to(pc, (CHUNK, D_RWKV)) for pc in st["p_chunk"]], axis=0)
            st["cum"], st["logw"] = cum, logw
            project(COL_K, D_RWKV)

        def f_receptance():
            r = lerp(COL_R, D_RWKV)
            rt_ref[...] = r * jnp.exp2(st["cum"])
            bon_ref[...] = r * rk_ref[...]
            project(COL_V, D_RWKV)

        def store_transposed(ref, val):
            for c in range(n_chunks):
                for g in range(N_GROUPS):
                    ref[rows_of(c), lanes_of(g)] = _heads_to_rows_t(val[rows_of(c), lanes_of(g)])

        def f_key():
            a, cum = st["a"], st["cum"]
            k = lerp(COL_K, D_RWKV)
            kkr = k * kk_ref[...]
            ssq = segsum((kkr * kkr).astype(BF16))
            kk = kkr * lax.rsqrt(jnp.maximum(ssq, L2_EPS * L2_EPS))
            k2 = k * (a * ka_ref[...] + (1.0 - ka_ref[...]))
            p_inv = jnp.exp2(-cum)
            bt = kk * a * p_inv
            kt = k2 * p_inv
            store_transposed(bt_ref, bt)
            store_transposed(kt_ref, kt)
            store_transposed(bdc_ref, bt * st["p_tot"])
            store_transposed(kdc_ref, kt * st["p_tot"])
            at_ref[...] = -kk * jnp.exp2(cum - st["logw"])
            bon_ref[...] = segsum((bon_ref[...] * k2).astype(BF16)) * v_ref[...]

        def f_value():
            v_ref[...] = lerp(COL_V, D_RWKV)

        front = [f_norm, f_decay, f_receptance, f_value, f_key]

        inst = [(c, g) for g in range(N_GROUPS) for c in range(n_chunks)]
        lab, lak, mrb, mrk = {}, {}, {}, {}
        tinv, lpow, lakv, mrkv, kv, xm = {}, {}, {}, {}, {}, {}
        zt, q, y0, a_mat, g_mat = {}, {}, {}, {}, {}

        def st_scores(i):
            rows, lanes = rows_of(i[0]), lanes_of(i[1])
            ar = jnp.concatenate([at_ref[rows, lanes], rt_ref[rows, lanes]], axis=0).astype(BF16)
            rhs1 = jnp.concatenate([bdt(blocks(bt_ref[rows, lanes])),
                                    bdt(blocks(kt_ref[rows, lanes]))], axis=1)
            sc = _dot(ar, rhs1) * trimask_ref[...]
            lab[i] = sc[0:CHUNK, 0:GROUP]
            lak[i] = sc[0:CHUNK, GROUP:2 * GROUP]
            mrb[i] = sc[CHUNK:2 * CHUNK, 0:GROUP]
            mrk[i] = sc[CHUNK:2 * CHUNK, GROUP:2 * GROUP]

        def st_inv_start(i):
            rows, lanes = rows_of(i[0]), lanes_of(i[1])
            ldiag = lab[i] * tmask_ref[0]
            tinv[i] = eye_ref[...] + ldiag
            lpow[i] = _dot(ldiag.astype(BF16), bd(ldiag))
            xv = _dot(jnp.concatenate([lak[i], mrk[i], kdc_ref[rows, lanes]], axis=0).astype(BF16),
                      bdt(blocks(v_ref[rows, lanes])))
            lakv[i] = xv[0:CHUNK, :]
            mrkv[i] = xv[CHUNK:2 * CHUNK, :]
            kv[i] = xv[2 * CHUNK:3 * CHUNK, :]

        def st_neumann(i):
            res = _dot(jnp.concatenate([lpow[i], tinv[i]], axis=0).astype(BF16), bd(lpow[i]))
            lpow[i] = res[0:CHUNK, :]
            tinv[i] = tinv[i] + res[CHUNK:2 * CHUNK, :]

        def st_neumann_last(i):
            tinv[i] = tinv[i] + _dot(tinv[i].astype(BF16), bd(lpow[i]))

        def st_merge_x(level):
            def stage(i):
                xm[i] = _dot((lab[i] * tmask_ref[level]).astype(BF16), bd(tinv[i]))
            return stage

        def st_merge_t(i):
            tinv[i] = tinv[i] + _dot(tinv[i].astype(BF16), bd(xm[i]))

        def st_last_merge_x(i):
            rows, lanes = rows_of(i[0]), lanes_of(i[1])
            res = _dot(jnp.concatenate([lab[i] * tmask_ref[N_MERGES], mrb[i], bdc_ref[rows, lanes]],
                                       axis=0).astype(BF16),
                       bd(tinv[i]))
            xm[i] = res[0:CHUNK, :]
            zt[i] = res[CHUNK:3 * CHUNK, :]

        def st_last_merge_t(i):
            zt[i] = zt[i] + _dot(zt[i].astype(BF16), bd(xm[i]))

        def st_transition(i):
            rows, lanes = rows_of(i[0]), lanes_of(i[1])
            res = _dot(zt[i].astype(BF16),
                       jnp.concatenate([bdt(blocks(at_ref[rows, lanes])), bdt(blocks(lakv[i]))], axis=1))
            q[i] = rt_ref[rows, lanes] + res[0:CHUNK, 0:GROUP]
            y0[i] = res[0:CHUNK, GROUP:2 * GROUP] + mrkv[i]
            a_mat[i] = eye_ref[...] * st["p_chunk"][i[0]][:, lanes] + res[CHUNK:2 * CHUNK, 0:GROUP]
            g_mat[i] = res[CHUNK:2 * CHUNK, GROUP:2 * GROUP] + kv[i]

        stage_fns = ([st_scores, st_inv_start] + [st_neumann] * (INV_BLOCK.bit_length() - 3)
                     + [st_neumann_last])
        for level in range(1, N_MERGES):
            stage_fns += [st_merge_x(level), st_merge_t]
        stage_fns += [st_last_merge_x, st_last_merge_t, st_transition]

        def run_stage(fn):
            def thunk():
                for i in inst:
                    fn(i)
            return thunk

        chunk_stages = [run_stage(fn) for fn in stage_fns]

        y_chunks = {}

        def chain_step(c):
            def thunk():
                if c == 0:
                    st["hs"] = [h_ref[g] for g in range(N_GROUPS)]
                hs, ys = st["hs"], []
                for g in range(N_GROUPS):
                    i = (c, g)
                    res = _dot(jnp.concatenate([a_mat[i], q[i]], axis=0).astype(BF16), bd(hs[g]))
                    hs[g] = res[0:CHUNK, :] + g_mat[i]
                    ys.append(res[CHUNK:2 * CHUNK, :] + y0[i])
                y_chunks[c] = jnp.concatenate(ys, axis=1)
                if c == n_chunks - 1:
                    for g in range(N_GROUPS):
                        h_ref[g] = hs[g]
            return thunk

        diffs = []

        def pool_windows():
            n_ext = TB + PAD
            sa_ref[8:n_ext, :] = p_ref[8:n_ext, 0:512] + p_ref[7:n_ext - 1, 0:512]
            sb_ref[8:n_ext, 128:512] = sa_ref[8:n_ext, 128:512] + sa_ref[6:n_ext - 2, 128:512]
            sa_ref[8:n_ext, 256:512] = sb_ref[8:n_ext, 256:512] + sb_ref[4:n_ext - 4, 256:512]
            sb_ref[8:n_ext, 384:512] = sa_ref[8:n_ext, 384:512] + sa_ref[0:n_ext - 8, 384:512]

        def pool_diffs():
            pos = t * TB + lax.broadcasted_iota(jnp.int32, (TB, 1), 0) + 1
            wsum_refs = (sa_ref, sb_ref, sa_ref, sb_ref)
            for g, win in enumerate(POOL_WINDOWS):
                lanes = slice(g * POOL_GROUP, (g + 1) * POOL_GROUP)
                cnt = jnp.minimum(pos, win).astype(F32)
                mean = wsum_refs[g][PAD:PAD + TB, lanes] / cnt
                diffs.append(mean - p_ref[PAD:PAD + TB, lanes])

        def pool_out(half):
            lanes = slice(256 * half, 256 * half + 256)
            d2 = jnp.concatenate(diffs[2 * half:2 * half + 2], axis=1).astype(BF16)
            yp = _dot(d2, _diag2(pw_ref[2 * half].astype(BF16), pw_ref[2 * half + 1].astype(BF16)))
            zz = p_ref[PAD:PAD + TB, COL_PZ + 256 * half:COL_PZ + 256 * half + 256]
            yp = yp * pscale_ref[:, lanes] * (zz * _sigmoid(zz))
            mix_ref[:, lanes] = yp.astype(BF16)

        half_w = D_POOL // 2
        fillers = [
            lambda: project(COL_U, half_w),
            lambda: project(COL_U + half_w, half_w),
            lambda: (project(COL_PZ, half_w), pool_windows()),
            lambda: (project(COL_PZ + half_w, half_w), pool_diffs()),
            lambda: (project(COL_Z, half_w), pool_out(0)),
            lambda: (project(COL_Z + half_w, half_w), pool_out(1)),
        ]

        def finish():
            yh = jnp.concatenate([y_chunks[c] for c in range(n_chunks)], axis=0)
            mu = segsum(yh.astype(BF16)) * (1.0 / HEAD)
            dlt = yh - mu
            var = segsum((dlt * dlt).astype(BF16)) * (1.0 / HEAD)
            yn = dlt * lax.rsqrt(var + GN_EPS) * lnw_ref[...] + lnb_ref[...]
            z = lerp(COL_Z, D_RWKV)
            y_rwkv = (yn + bon_ref[...]) * (z * _sigmoid(z))
            mix_ref[:, D_POOL:D_POOL + D_RWKV] = y_rwkv.astype(BF16)
            out = _dot(mix_ref[...], wout_ref[...])
            xo = xs_ref[...] + gate * out
            ms2 = jnp.mean(xo * xo, axis=-1, keepdims=True)
            os_ref[...] = xo * lax.rsqrt(ms2 + NORM_EPS) * fg_ref[...]
            p_ref[8:PAD, :] = p_ref[TB + 8:TB + PAD, :]

        tail = []
        for c in range(n_chunks):
            tail += fillers[c:c + 1]
            tail.append(chain_step(c))
        tail += fillers[n_chunks:] + [finish]
        return front, chunk_stages, tail

    front, chunk_stages, tail = build()
    for thunk in front + chunk_stages + tail:
        thunk()


def _constants():
    hb = np.arange(GROUP) // HEAD
    bdmask = (hb[:, None] == hb[None, :]).astype(np.float32)
    ti = np.arange(CHUNK)[:, None]
    si = (np.arange(2 * GROUP) % HEAD)[None, :]
    trimask = np.concatenate([(ti > si), (ti >= si)], axis=0).astype(np.float32)
    sg = (np.arange(GROUP) % HEAD)[None, :]
    eye = (ti == sg).astype(np.float32)
    same = [(ti // (INV_BLOCK << m)) == (sg // (INV_BLOCK << m)) for m in range(N_MERGES + 1)]
    tmask = np.stack([same[0]] + [same[m] & ~same[m - 1] for m in range(1, N_MERGES + 1)],
                     axis=0).astype(np.float32)
    rr = np.arange(CUM_ROWS)
    same = (rr[:, None] // CHUNK) == (rr[None, :] // CHUNK)
    tril = same & (rr[None, :] <= rr[:, None])
    cmat = tril.astype(np.float32)
    return bdmask, trimask, eye, tmask, cmat


def kernel(x, c, w_ada, b_ada, norm_g, w_in, pool_w, pool_scale, mu_shift, w0, w_up, a0, a_up,
           k_k, k_a, r_k, ln_w, ln_b, w_out, final_g):
    B, T, _ = x.shape
    assert w_ada.shape[0] == 1 and T % TB == 0
    l = 0

    mod, w_in_p, w_out_b = _prologue(c, w_ada, b_ada, w_in, w_out, l)
    mu_p =jnp.concatenate([mu_shift[l, a - _S0:b - _S0] for a, b in W_IN_SPANS[1:]], axis=0)[None, :]

    bdmask, trimask, eye, tmask, cmat = _constants()
    bdmask_b = jnp.asarray(bdmask, BF16)
    segones = bdmask_b
    trimask = jnp.asarray(trimask)
    eye = jnp.asarray(eye)
    tmask = jnp.asarray(tmask)
    cmat = jnp.asarray(cmat, BF16)

    resident = pl.Buffered(1)

    def full(a):
        nd = a.ndim
        return pl.BlockSpec(a.shape, lambda b, t, _nd=nd: (0,) * _nd, pipeline_mode=resident)

    def layer(a):
        a = a.reshape(a.shape[0], 1, a.shape[1]) if a.ndim == 2 else a
        nd = a.ndim
        return a, pl.BlockSpec((None,) + a.shape[1:], lambda b, t, _nd=nd: (l,) + (0,) * (_nd - 1),
                               pipeline_mode=resident)

    operands = [
        (x, pl.BlockSpec((None, TB, D_MODEL), lambda b, t: (b, t, 0))),
        (mod, pl.BlockSpec((None, 3, D_MODEL), lambda b, t: (b, 0, 0))),
        layer(norm_g), (w_in_p, full(w_in_p)), (mu_p, full(mu_p)), layer(w_up), layer(a_up),
        layer(w0), layer(a0), layer(k_k), layer(k_a), layer(r_k), layer(ln_w), layer(ln_b),
        layer(pool_scale), layer(pool_w), (w_out_b, full(w_out_b)),
        (final_g[None, :], full(final_g[None, :])),
    ] + [(a, full(a)) for a in (bdmask_b, trimask, eye, tmask, cmat, segones)]
    in_specs = [spec for _, spec in operands]

    blk = lambda: pltpu.VMEM((TB, D_RWKV), F32)
    scratch = [
        pltpu.VMEM((TB + PAD, N_IN), F32),
        pltpu.VMEM((TB + PAD, D_POOL), F32),
        pltpu.VMEM((TB + PAD, D_POOL), F32),
        blk(), blk(), blk(), blk(), blk(), blk(),
        blk(), blk(),
        pltpu.VMEM((TB, D_MODEL), BF16),
        pltpu.VMEM((TB, D_MODEL), BF16),
        pltpu.VMEM((N_GROUPS, HEAD, GROUP), F32),
    ]
    return pl.pallas_call(
        _fused_kernel,
        grid=(B, T // TB),
        in_specs=in_specs,
        out_specs=pl.BlockSpec((None, TB, D_MODEL), lambda b, t: (b, t, 0)),
        out_shape=jax.ShapeDtypeStruct((B, T, D_MODEL), x.dtype),
        scratch_shapes=scratch,
        compiler_params=pltpu.CompilerParams(
            dimension_semantics=("arbitrary", "arbitrary"),
            vmem_limit_bytes=VMEM_LIMIT_BYTES),
        name="hybrid_block",
    )(*[a for a, _ in operands])
```
